```python
import jax, jax.numpy as jnp
from jax import lax
import numpy as np

D_MODEL = 2048
BATCH = 4
SEQ = 2048
DEPTH = 1

HEAD_DIM = 64
SWA_Q_HEADS = 16
SWA_KV_HEADS = 4
SWA_GROUP = SWA_Q_HEADS // SWA_KV_HEADS
WINDOW = 128
FOX_HEADS = 16
BLOCK_Q = 128
ROPE_THETA = 10000.0
FGATE_BIAS_INIT = 3.0

N_EXPERTS = 32
TOP_K = 4
D_EXPERT = D_MODEL
SWIGLU_LIMIT = 7.0
SWIGLU_ALPHA = 1.702
MOE_BLOCK = 128

RMS_EPS = 1e-5
NEG_INF = -1e30

SWA_Q_W = SWA_Q_HEADS * HEAD_DIM
SWA_KV_W = SWA_KV_HEADS * HEAD_DIM
FOX_W = FOX_HEADS * HEAD_DIM
IN_WIDTHS = [SWA_Q_W, SWA_KV_W, SWA_KV_W, FOX_W, FOX_W, FOX_W, FOX_HEADS, D_MODEL, D_MODEL]
IN_W = sum(IN_WIDTHS)
IN_SPLITS = [int(v) for v in np.cumsum(IN_WIDTHS)[:-1]]
FGATE_OFFSET = SWA_Q_W + 2 * SWA_KV_W + 3 * FOX_W

kernel_name = "hybrid_swa_sink_fox_gated_moe_adaln"


def rms_norm(x, g):
    xf = x.astype(jnp.float32)
    y = xf * lax.rsqrt(jnp.mean(xf * xf, axis=-1, keepdims=True) + RMS_EPS)
    return (y * g.astype(jnp.float32)).astype(x.dtype)


def rope(x, positions):
    hd = x.shape[-1]
    inv_freq = 1.0 / (ROPE_THETA ** (jnp.arange(0, hd, 2, dtype=jnp.float32) / hd))
    ang = positions.astype(jnp.float32)[..., None] * inv_freq
    cos = jnp.cos(ang)[:, :, None, :]
    sin = jnp.sin(ang)[:, :, None, :]
    xf = x.astype(jnp.float32)
    x1, x2 = jnp.split(xf, 2, axis=-1)
    out = jnp.concatenate([x1 * cos - x2 * sin, x2 * cos + x1 * sin], axis=-1)
    return out.astype(x.dtype)


def swa_sink_attention(q, k, v, sinks):
    B, S, _, hd = q.shape
    W = WINDOW
    nb = S // W
    scale = hd ** -0.5
    qb = q.reshape(B, nb, W, SWA_KV_HEADS, SWA_GROUP, hd)
    kb = k.reshape(B, nb, W, SWA_KV_HEADS, hd)
    vb = v.reshape(B, nb, W, SWA_KV_HEADS, hd)
    pad = ((0, 0), (1, 0), (0, 0), (0, 0), (0, 0))
    kk = jnp.concatenate([jnp.pad(kb, pad)[:, :-1], kb], axis=2)
    vv = jnp.concatenate([jnp.pad(vb, pad)[:, :-1], vb], axis=2)
    scores = jnp.einsum('bnqhgd,bnkhd->bnhgqk', qb, kk).astype(jnp.float32) * scale
    qi = jnp.arange(W)[:, None]
    kj = jnp.arange(2 * W)[None, :]
    delta = qi + W - kj
    band = (delta >= 0) & (delta < WINDOW)
    key_pos = jnp.arange(nb)[:, None, None] * W - W + kj[None]
    valid = band[None] & (key_pos >= 0)
    scores = jnp.where(valid[None, :, None, None], scores, NEG_INF)
    sink_col = jnp.broadcast_to(
        sinks.astype(jnp.float32).reshape(1, 1, SWA_KV_HEADS, SWA_GROUP, 1, 1),
        scores.shape[:-1] + (1,))
    probs = jax.nn.softmax(jnp.concatenate([scores, sink_col], axis=-1), axis=-1)[..., :-1]
    out = jnp.einsum('bnhgqk,bnkhd->bnqhgd', probs.astype(v.dtype), vv)
    return out.reshape(B, S, SWA_Q_HEADS * hd)


def forgetting_attention(q, k, v, log_f):
    B, S, H, hd = q.shape
    nb = S // BLOCK_Q
    scale = hd ** -0.5
    cum = jnp.transpose(jnp.cumsum(log_f, axis=1), (0, 2, 1))
    q_blocks = jnp.transpose(q.reshape(B, nb, BLOCK_Q, H, hd), (1, 0, 2, 3, 4))
    cum_q = jnp.transpose(cum.reshape(B, H, nb, BLOCK_Q), (2, 0, 1, 3))
    starts = jnp.arange(nb, dtype=jnp.int32) * BLOCK_Q
    key_pos = jnp.arange(S, dtype=jnp.int32)

    def one_block(args):
        qb, cq, start = args
        s = jnp.einsum('bqhd,bkhd->bhqk', qb, k).astype(jnp.float32) * scale
        s = s + cq[..., :, None] - cum[:, :, None, :]
        q_pos = start + jnp.arange(BLOCK_Q, dtype=jnp.int32)
        causal = key_pos[None, :] <= q_pos[:, None]
        p = jax.nn.softmax(jnp.where(causal, s, NEG_INF), axis=-1)
        return jnp.einsum('bhqk,bkhd->bqhd', p.astype(v.dtype), v)

    out = lax.map(one_block, (q_blocks, cum_q, starts))
    return jnp.transpose(out, (1, 0, 2, 3, 4)).reshape(B, S, H * hd)


def hybrid_mixer(xn, positions, w_in, b_in, sinks, w_branch_a, w_branch_b, w_out):
    B, S, _ = xn.shape
    proj = xn @ w_in + b_in
    qa, ka, va, qb, kb, vb, fb, ga, gb = jnp.split(proj, IN_SPLITS, axis=-1)
    qa = rope(qa.reshape(B, S, SWA_Q_HEADS, HEAD_DIM), positions)
    ka = rope(ka.reshape(B, S, SWA_KV_HEADS, HEAD_DIM), positions)
    va = va.reshape(B, S, SWA_KV_HEADS, HEAD_DIM)
    ya = swa_sink_attention(qa, ka, va, sinks)
    log_f = jax.nn.log_sigmoid(fb.astype(jnp.float32))
    yb = forgetting_attention(qb.reshape(B, S, FOX_HEADS, HEAD_DIM),
                              kb.reshape(B, S, FOX_HEADS, HEAD_DIM),
                              vb.reshape(B, S, FOX_HEADS, HEAD_DIM), log_f)
    merged = jax.nn.sigmoid(ga) * (ya @ w_branch_a) + jax.nn.sigmoid(gb) * (yb @ w_branch_b)
    return merged @ w_out


def moe_ffn(xn, w_router, b_router, w_gu, b_gu, w_down, b_down):
    B, S, D = xn.shape
    N = B * S
    A = N * TOP_K
    xf = xn.reshape(N, D)
    logits = (xf @ w_router + b_router).astype(jnp.float32)
    top_val, top_idx = lax.top_k(logits, TOP_K)
    gates = jax.nn.softmax(top_val, axis=-1)
    eid = top_idx.reshape(A).astype(jnp.int32)
    tok = jnp.arange(A, dtype=jnp.int32) // TOP_K
    gw = gates.reshape(A)
    order = jnp.argsort(eid)
    s_eid, s_tok, s_w = eid[order], tok[order], gw[order]
    counts = jnp.bincount(eid, length=N_EXPERTS).astype(jnp.int32)
    starts = jnp.cumsum(counts) - counts
    pad_counts = ((counts + MOE_BLOCK - 1) // MOE_BLOCK) * MOE_BLOCK
    pad_ends = jnp.cumsum(pad_counts)
    pad_starts = pad_ends - pad_counts
    dest = pad_starts[s_eid] + (jnp.arange(A, dtype=jnp.int32) - starts[s_eid])
    P = A + N_EXPERTS * MOE_BLOCK
    n_blocks = P // MOE_BLOCK
    row_tok = jnp.zeros((P,), jnp.int32).at[dest].set(s_tok)
    row_w = jnp.zeros((P,), jnp.float32).at[dest].set(s_w)
    block_starts = jnp.arange(n_blocks, dtype=jnp.int32) * MOE_BLOCK
    block_e = jnp.minimum(jnp.searchsorted(pad_ends, block_starts, side='right'),
                          N_EXPERTS - 1).astype(jnp.int32)
    x_rows = xf[row_tok].reshape(n_blocks, MOE_BLOCK, D)

    def expert_block(args):
        xb, e = args
        hcat = xb @ w_gu[e] + b_gu[e]
        glu, lin = jnp.split(hcat, 2, axis=-1)
        glu = jnp.minimum(glu, SWIGLU_LIMIT)
        lin = jnp.clip(lin, -SWIGLU_LIMIT, SWIGLU_LIMIT)
        act = glu * jax.nn.sigmoid(SWIGLU_ALPHA * glu) * (lin + 1.0)
        return act @ w_down[e] + b_down[e]

    y_rows = lax.map(expert_block, (x_rows, block_e)).reshape(P, D)
    y = jax.ops.segment_sum(y_rows * row_w[:, None].astype(y_rows.dtype), row_tok,
                            num_segments=N)
    return y.reshape(B, S, D)


def setup_inputs(seed: int = 0) -> dict:
    key = jax.random.key(seed)
    ks = jax.random.split(key, 20)
    f32 = jnp.float32
    D, F, E, L = D_MODEL, D_EXPERT, N_EXPERTS, DEPTH
    x = jax.random.normal(ks[0], (BATCH, SEQ, D), f32)
    c = jax.random.normal(ks[1], (BATCH, D), f32)
    positions = jnp.broadcast_to(jnp.arange(SEQ, dtype=jnp.int32)[None, :], (BATCH, SEQ))
    w_ada = 0.5 * jax.random.normal(ks[2], (L, D, 6 * D), f32) * D ** -0.5
    b_ada = 0.02 * jax.random.normal(ks[3], (L, 6 * D), f32)
    g_mix = 1.0 + 0.02 * jax.random.normal(ks[4], (L, D), f32)
    w_in = jax.random.normal(ks[5], (L, D, IN_W), f32) * D ** -0.5
    b_in = 0.01 * jax.random.normal(ks[6], (L, IN_W), f32)
    b_in = b_in.at[:, FGATE_OFFSET:FGATE_OFFSET + FOX_HEADS].add(FGATE_BIAS_INIT)
    sinks = 0.5 * jax.random.normal(ks[7], (L, SWA_Q_HEADS), f32)
    w_branch_a = jax.random.normal(ks[8], (L, SWA_Q_W, D), f32) * SWA_Q_W ** -0.5
    w_branch_b = jax.random.normal(ks[9], (L, FOX_W, D), f32) * FOX_W ** -0.5
    w_out = jax.random.normal(ks[10], (L, D, D), f32) * D ** -0.5
    g_ffn = 1.0 + 0.02 * jax.random.normal(ks[11], (L, D), f32)
    w_router = jax.random.normal(ks[12], (L, D, E), f32) * D ** -0.5
    b_router = 0.01 * jax.random.normal(ks[13], (L, E), f32)
    w_gu = jax.random.normal(ks[14], (L, E, D, 2 * F), f32) * D ** -0.5
    b_gu = 0.01 * jax.random.normal(ks[15], (L, E, 2 * F), f32)
    w_down = jax.random.normal(ks[16], (L, E, F, D), f32) * F ** -0.5
    b_down = 0.01 * jax.random.normal(ks[17], (L, E, D), f32)
    g_final = 1.0 + 0.02 * jax.random.normal(ks[18], (D,), f32)
    return {"x": x, "c": c, "positions": positions, "w_ada": w_ada, "b_ada": b_ada,
            "g_mix": g_mix, "w_in": w_in, "b_in": b_in, "sinks": sinks,
            "w_branch_a": w_branch_a, "w_branch_b": w_branch_b, "w_out": w_out,
            "g_ffn": g_ffn, "w_router": w_router, "b_router": b_router,
            "w_gu": w_gu, "b_gu": b_gu, "w_down": w_down, "b_down": b_down,
            "g_final": g_final}


def reference(x, c, positions, w_ada, b_ada, g_mix, w_in, b_in, sinks, w_branch_a,
              w_branch_b, w_out, g_ffn, w_router, b_router, w_gu, b_gu, w_down, b_down,
              g_final):
    h = x
    c_act = jax.nn.silu(c)
    for l in range(DEPTH):
        mod = c_act @ w_ada[l] + b_ada[l]
        sh1, sc1, gt1, sh2, sc2, gt2 = [m[:, None, :] for m in jnp.split(mod, 6, axis=-1)]
        xn = rms_norm(h, g_mix[l]) * (1.0 + sc1) + sh1
        h = h + gt1 * hybrid_mixer(xn, positions, w_in[l], b_in[l], sinks[l],
                                   w_branch_a[l], w_branch_b[l], w_out[l])
        xn = rms_norm(h, g_ffn[l]) * (1.0 + sc2) + sh2
        h = h + gt2 * moe_ffn(xn, w_router[l], b_router[l], w_gu[l], b_gu[l],
                              w_down[l], b_down[l])
    return rms_norm(h, g_final)
```

```python
import functools

import jax
import jax.numpy as jnp
from jax import lax
from jax.experimental import pallas as pl
from jax.experimental.pallas import tpu as pltpu

f32 = jnp.float32
bf16 = jnp.bfloat16
i32 = jnp.int32

D = 2048
HD = 64
SWA_QW = 1024
SWA_KVW = 256
FOX_W = 1024
N_FGATE = 16
WINDOW = 128
N_EXPERTS = 32
TOP_K = 4
D_EXPERT = 2048
SWIGLU_LIMIT = 7.0
SWIGLU_ALPHA = 1.702
RMS_EPS = 1e-5
NEG_INF = -1e30
ROPE_THETA = 10000.0

QKV_W = SWA_QW + 2 * SWA_KVW + 3 * FOX_W
FGATE_OFF = QKV_W
GATE_OFF = QKV_W + N_FGATE

LANES = 128
P_SA = 0
P_SB = D
P_QA = 2 * D
P_KA = P_QA + SWA_QW
P_VA = P_KA + SWA_KVW
P_QB = P_VA + SWA_KVW
P_KB = P_QB + FOX_W
P_VB = P_KB + FOX_W
PROJ_W = P_VB + FOX_W

VMEM_LIMIT = 56 * 1024 * 1024


def _cparams(sem, vmem=VMEM_LIMIT):
    return pltpu.CompilerParams(dimension_semantics=sem, vmem_limit_bytes=vmem)


def _dot(a, b):
    return jnp.dot(a, b, preferred_element_type=f32)


ADA_TN = 1024


def _adaln_kernel(c_ref, w_ref, b_ref, o_ref):
    c = c_ref[...]
    ca = c * jax.nn.sigmoid(c)
    o_ref[...] = _dot(ca, w_ref[...]) + b_ref[...]


def _adaln(c_pad, w_ada, b_ada):
    rows = c_pad.shape[0]
    n = w_ada.shape[1]
    return pl.pallas_call(
        _adaln_kernel,
        grid=(n // ADA_TN,),
        in_specs=[
            pl.BlockSpec((rows, D), lambda j: (0, 0)),
            pl.BlockSpec((D, ADA_TN), lambda j: (0, j)),
            pl.BlockSpec((1, ADA_TN), lambda j: (0, j)),
        ],
        out_specs=pl.BlockSpec((rows, ADA_TN), lambda j: (0, j)),
        out_shape=jax.ShapeDtypeStruct((rows, n), f32),
        compiler_params=_cparams(("arbitrary",)),
        name="adaln",
    )(c_pad, w_ada, b_ada)


IN_TM = 512
IN_TN = 512
IN_NJ = PROJ_W // IN_TN
J_QA0 = P_QA // IN_TN
J_KVA = P_KA // IN_TN
J_QB0 = P_QB // IN_TN
J_KB0 = P_KB // IN_TN


def _rope_block(a, cos, sin_signed, first_half):
    fwd = pltpu.roll(a, LANES - HD // 2, axis=1)
    bwd = pltpu.roll(a, HD // 2, axis=1)
    partner = jnp.where(first_half, fwd, bwd)
    return a * cos + partner * sin_signed


def _inproj_kernel(x_ref, mod_ref, g_ref, w_ref, b_ref, wf_ref, bf_ref, cos_ref, sin_ref,
                   proj_ref, logf_ref, xn_ref):
    j = pl.program_id(1)

    @pl.when(j == 0)
    def _():
        x = x_ref[...]
        ms = jnp.mean(x * x, axis=-1, keepdims=True)
        y = x * lax.rsqrt(ms + RMS_EPS) * g_ref[...]
        xn = y * (1.0 + mod_ref[:, D:2 * D]) + mod_ref[:, 0:D]
        xnb = xn.astype(bf16)
        xn_ref[...] = xnb
        fl = _dot(xnb, wf_ref[...]) + bf_ref[...]
        logf_ref[...] = jnp.minimum(fl, 0.0) - jnp.log(1.0 + jnp.exp(-jnp.abs(fl)))

    acc = _dot(xn_ref[...], w_ref[...]) + b_ref[...]

    lane = lax.broadcasted_iota(i32, (IN_TM, LANES), 1)
    first_half = (lane % HD) < (HD // 2)
    scale = HD ** -0.5

    def rope_cols(n_blocks, mult):
        cos = cos_ref[...]
        sin = sin_ref[...]
        for g in range(n_blocks):
            sl = slice(g * LANES, (g + 1) * LANES)
            r = _rope_block(acc[:, sl], cos, sin, first_half)
            proj_ref[:, sl] = (r * mult).astype(bf16)

    @pl.when(j < J_QA0)
    def _():
        proj_ref[...] = jax.nn.sigmoid(acc).astype(bf16)

    @pl.when((j >= J_QA0) & (j < J_KVA))
    def _():
        rope_cols(IN_TN // LANES, scale)

    @pl.when(j == J_KVA)
    def _():
        rope_cols(SWA_KVW // LANES, 1.0)
        proj_ref[:, SWA_KVW:] = acc[:, SWA_KVW:].astype(bf16)

    @pl.when((j >= J_QB0) & (j < J_KB0))
    def _():
        proj_ref[...] = (acc * scale).astype(bf16)

    @pl.when(j >= J_KB0)
    def _():
        proj_ref[...] = acc.astype(bf16)


def _inproj(x2, mod3, g_mix, w_cat, b_cat, w_f, b_f, cos_t, sin_t, seq):
    n = x2.shape[0]
    tiles_per_batch = seq // IN_TM
    return pl.pallas_call(
        _inproj_kernel,
        grid=(n // IN_TM, IN_NJ),
        in_specs=[
            pl.BlockSpec((IN_TM, D), lambda i, j: (i, 0)),
            pl.BlockSpec((None, 1, 6 * D), lambda i, j: (i // tiles_per_batch, 0, 0)),
            pl.BlockSpec((1, D), lambda i, j: (0, 0)),
            pl.BlockSpec((D, IN_TN), lambda i, j: (0, j)),
            pl.BlockSpec((1, IN_TN), lambda i, j: (0, j)),
            pl.BlockSpec((D, N_FGATE), lambda i, j: (0, 0)),
            pl.BlockSpec((1, N_FGATE), lambda i, j: (0, 0)),
            pl.BlockSpec((IN_TM, LANES), lambda i, j: (i, 0)),
            pl.BlockSpec((IN_TM, LANES), lambda i, j: (i, 0)),
        ],
        out_specs=[
            pl.BlockSpec((IN_TM, IN_TN), lambda i, j: (i, j)),
            pl.BlockSpec((IN_TM, N_FGATE), lambda i, j: (i, 0)),
        ],
        out_shape=[
            jax.ShapeDtypeStruct((n, PROJ_W), bf16),
            jax.ShapeDtypeStruct((n, N_FGATE), f32),
        ],
        scratch_shapes=[pltpu.VMEM((IN_TM, D), bf16)],
        compiler_params=_cparams(("arbitrary", "arbitrary")),
        name="inproj",
    )(x2, mod3, g_mix, w_cat, b_cat, w_f, b_f, cos_t, sin_t)


CUM_CH = 256


def _split3(x):
    x1 = x.astype(bf16)
    r1 = x - x1.astype(f32)
    x2 = r1.astype(bf16)
    r2 = r1 - x2.astype(f32)
    return x1, x2, r2.astype(bf16)


def _cumsum_kernel(x_ref, o_ref):
    seq = x_ref.shape[0]
    row = lax.broadcasted_iota(i32, (CUM_CH, CUM_CH), 0)
    col = lax.broadcasted_iota(i32, (CUM_CH, CUM_CH), 1)
    tri = jnp.where(col <= row, 1.0, 0.0).astype(bf16)
    carry = jnp.zeros((1, N_FGATE), f32)
    for c in range(seq // CUM_CH):
        sl = slice(c * CUM_CH, (c + 1) * CUM_CH)
        x1, x2, x3 = _split3(x_ref[sl, :])
        cs = _dot(tri, x1) + _dot(tri, x2) + _dot(tri, x3) + carry
        o_ref[sl, :] = cs
        carry = cs[CUM_CH - 1:CUM_CH, :]


def _cumsum(logf3):
    b, seq, h = logf3.shape
    return pl.pallas_call(
        _cumsum_kernel,
        grid=(b,),
        in_specs=[pl.BlockSpec((None, seq, h), lambda i: (i, 0, 0))],
        out_specs=pl.BlockSpec((None, seq, h), lambda i: (i, 0, 0)),
        out_shape=jax.ShapeDtypeStruct((b, seq, h), f32),
        compiler_params=_cparams(("arbitrary",)),
        name="fgate_cumsum",
    )(logf3)


SWA_QCOLS = 512


def _swa_kernel(sinks_ref, q_ref, k_ref, v_ref, o_ref, vt_ref):
    jp = pl.program_id(1)
    seq = q_ref.shape[0]
    nblk = seq // WINDOW

    for c in range(nblk):
        vblk = v_ref[c * WINDOW:(c + 1) * WINDOW, :].astype(f32)
        vt_ref[c] = vblk.T.astype(bf16)

    kj = lax.broadcasted_iota(i32, (2 * WINDOW, WINDOW), 0)
    qi = lax.broadcasted_iota(i32, (2 * WINDOW, WINDOW), 1)
    delta = qi + WINDOW - kj
    band = (delta >= 0) & (delta < WINDOW)
    zeros_half = jnp.zeros((HD, WINDOW), bf16)

    def body(n, _):
        prev = jnp.maximum(n - 1, 0)
        r0 = pl.multiple_of(n * WINDOW, WINDOW)
        p0 = pl.multiple_of(prev * WINDOW, WINDOW)
        kwin = jnp.concatenate([k_ref[pl.ds(p0, WINDOW), :], k_ref[pl.ds(r0, WINDOW), :]], axis=0)
        valid = band & ((n - 1) * WINDOW + kj >= 0)
        qblk = q_ref[pl.ds(r0, WINDOW), :]
        for t in range(SWA_QCOLS // LANES):
            g = t // 2
            qt = qblk[:, t * LANES:(t + 1) * LANES].astype(f32).T
            v_prev = vt_ref[prev, g * HD:(g + 1) * HD, :]
            v_cur = vt_ref[n, g * HD:(g + 1) * HD, :]
            outs = []
            for u in range(2):
                sink = sinks_ref[jp * 8 + t * 2 + u]
                feat = qt[u * HD:(u + 1) * HD, :].astype(bf16)
                wq = jnp.concatenate([feat, zeros_half] if g == 0 else [zeros_half, feat], axis=0)
                st = _dot(kwin, wq)
                st = jnp.where(valid, st, NEG_INF)
                m = jnp.maximum(jnp.max(st, axis=0, keepdims=True), sink)
                pt = jnp.exp(st - m)
                denom = jnp.sum(pt, axis=0, keepdims=True) + jnp.exp(sink - m)
                ptb = pt.astype(bf16)
                ot = _dot(v_prev, ptb[:WINDOW, :]) + _dot(v_cur, ptb[WINDOW:, :])
                outs.append(ot / denom)
            o_ref[pl.ds(r0, WINDOW), t * LANES:(t + 1) * LANES] = (
                jnp.concatenate(outs, axis=0).T.astype(bf16))
        return 0

    lax.fori_loop(0, nblk, body, 0)


def _swa(sinks, proj, batch, seq):
    n = proj.shape[0]
    qb0 = P_QA // SWA_QCOLS
    kb0 = P_KA // LANES
    vb0 = P_VA // LANES
    return pl.pallas_call(
        _swa_kernel,
        grid=(batch, SWA_QW // SWA_QCOLS),
        in_specs=[
            pl.BlockSpec(memory_space=pltpu.SMEM),
            pl.BlockSpec((seq, SWA_QCOLS), lambda b, j: (b, qb0 + j)),
            pl.BlockSpec((seq, LANES), lambda b, j: (b, kb0 + j)),
            pl.BlockSpec((seq, LANES), lambda b, j: (b, vb0 + j)),
        ],
        out_specs=pl.BlockSpec((seq, SWA_QCOLS), lambda b, j: (b, j)),
        out_shape=jax.ShapeDtypeStruct((n, SWA_QW), bf16),
        scratch_shapes=[pltpu.VMEM((seq // WINDOW, LANES, WINDOW), bf16)],
        compiler_params=_cparams(("arbitrary", "arbitrary")),
        name="swa_attention",
    )(sinks, proj, proj, proj)


FOX_T = 256


def _fox_kernel(q_ref, k_ref, v_ref, cum_ref, o_ref, vt_ref, kb_ref):
    p = pl.program_id(1)
    seq = q_ref.shape[0]
    nblk = seq // FOX_T

    for c in range(nblk):
        vblk = v_ref[c * FOX_T:(c + 1) * FOX_T, :].astype(f32)
        vt_ref[c] = vblk.T.astype(bf16)

    cum = cum_ref[...]
    hl = lax.broadcasted_iota(i32, cum.shape, 1)
    for u in range(2):
        col = jnp.sum(jnp.where(hl == 2 * p + u, cum, 0.0), axis=1, keepdims=True)
        kb_ref[u] = jnp.broadcast_to(col, (seq, LANES))

    frow = lax.broadcasted_iota(i32, (LANES, FOX_T), 0)
    kidx = lax.broadcasted_iota(i32, (FOX_T, FOX_T), 0)
    qidx = lax.broadcasted_iota(i32, (FOX_T, FOX_T), 1)
    causal = kidx <= qidx

    def scores(u, wq, k0):
        kblk = k_ref[pl.ds(k0, FOX_T), :]
        kb = kb_ref[u, pl.ds(k0, FOX_T), :]
        return _dot(kblk, wq) - jnp.concatenate([kb] * (FOX_T // LANES), axis=1)

    def update(u, st, jk, state):
        m, l, acc = state
        m_new = jnp.maximum(m, jnp.max(st, axis=0, keepdims=True))
        alpha = jnp.exp(m - m_new)
        pt = jnp.exp(st - m_new)
        l_new = alpha * l + jnp.sum(pt, axis=0, keepdims=True)
        vt = vt_ref[jk, u * HD:(u + 1) * HD, :]
        acc_new = alpha * acc + _dot(vt, pt.astype(bf16))
        return m_new, l_new, acc_new

    def qblock(iq, _):
        q0 = pl.multiple_of(iq * FOX_T, FOX_T)
        qt = q_ref[pl.ds(q0, FOX_T), :].astype(f32).T
        wqs = [jnp.where((frow // HD) == u, qt, 0.0).astype(bf16) for u in range(2)]
        init = tuple((jnp.full((1, FOX_T), NEG_INF, f32), jnp.zeros((1, FOX_T), f32),
                      jnp.zeros((HD, FOX_T), f32)) for _ in range(2))

        def kblock(jk, states):
            k0 = pl.multiple_of(jk * FOX_T, FOX_T)
            return tuple(update(u, scores(u, wqs[u], k0), jk, states[u]) for u in range(2))

        states = lax.fori_loop(0, iq, kblock, init)
        outs = []
        for u in range(2):
            st = jnp.where(causal, scores(u, wqs[u], q0), NEG_INF)
            _, l, acc = update(u, st, iq, states[u])
            outs.append(acc / l)
        o_ref[pl.ds(q0, FOX_T), :] = jnp.concatenate(outs, axis=0).T.astype(bf16)
        return 0

    lax.fori_loop(0, nblk, qblock, 0)


def _fox(proj, cum, batch, seq):
    n = proj.shape[0]
    qb0 = P_QB // LANES
    kb0 = P_KB // LANES
    vb0 = P_VB // LANES
    return pl.pallas_call(
        _fox_kernel,
        grid=(batch, FOX_W // LANES),
        in_specs=[
            pl.BlockSpec((seq, LANES), lambda b, p: (b, qb0 + p)),
            pl.BlockSpec((seq, LANES), lambda b, p: (b, kb0 + p)),
            pl.BlockSpec((seq, LANES), lambda b, p: (b, vb0 + p)),
            pl.BlockSpec((None, seq, N_FGATE), lambda b, p: (b, 0, 0)),
        ],
        out_specs=pl.BlockSpec((seq, LANES), lambda b, p: (b, p)),
        out_shape=jax.ShapeDtypeStruct((n, FOX_W), bf16),
        scratch_shapes=[pltpu.VMEM((seq // FOX_T, LANES, FOX_T), bf16),
                        pltpu.VMEM((2, seq, LANES), f32)],
        compiler_params=_cparams(("arbitrary", "arbitrary")),
        name="fox_attention",
    )(proj, proj, proj, cum)


MG_TM = 256
MG_TN = 512
ROUTE_W = LANES
R_IDX, R_GATE, R_RANK = 0, TOP_K, 2 * TOP_K


def _merge_kernel(ya_ref, yb_ref, sa_ref, sb_ref, x_ref, mod_ref, wa_ref, wb_ref, wo_ref, gf_ref,
                  wrh_ref, wrl_ref, br_ref, h1_ref, xn2_ref, route_ref, merged_ref, cnt_ref):
    i = pl.program_id(0)

    @pl.when(i == 0)
    def _():
        cnt_ref[...] = jnp.zeros_like(cnt_ref)

    ya = ya_ref[...]
    yb = yb_ref[...]
    for c in range(D // MG_TN):
        sl = slice(c * MG_TN, (c + 1) * MG_TN)
        ta = _dot(ya, wa_ref[:, sl])
        tb = _dot(yb, wb_ref[:, sl])
        merged_ref[:, sl] = (sa_ref[:, sl].astype(f32) * ta + sb_ref[:, sl].astype(f32) * tb).astype(bf16)

    merged = merged_ref[...]
    ssq = jnp.zeros((MG_TM, 1), f32)
    for c in range(D // MG_TN):
        sl = slice(c * MG_TN, (c + 1) * MG_TN)
        o = _dot(merged, wo_ref[:, sl])
        h = x_ref[:, sl] + mod_ref[:, 2 * D + c * MG_TN:2 * D + (c + 1) * MG_TN] * o
        h1_ref[:, sl] = h
        ssq = ssq + jnp.sum(h * h, axis=1, keepdims=True)

    inv = lax.rsqrt(ssq * (1.0 / D) + RMS_EPS)
    xn2 = h1_ref[...] * inv * gf_ref[...] * (1.0 + mod_ref[:, 4 * D:5 * D]) + mod_ref[:, 3 * D:4 * D]
    xn2_ref[...] = xn2
    hi = xn2.astype(bf16)
    lo = (xn2 - hi.astype(f32)).astype(bf16)
    wrh = wrh_ref[...]
    logits = _dot(hi, wrh) + _dot(lo, wrh) + _dot(hi, wrl_ref[...]) + br_ref[...]

    elane = lax.broadcasted_iota(i32, (MG_TM, N_EXPERTS), 1)
    work = logits
    onehots, vals, idxs = [], [], []
    for _ in range(TOP_K):
        mk = jnp.max(work, axis=1, keepdims=True)
        idx = jnp.min(jnp.where(work == mk, elane, N_EXPERTS), axis=1, keepdims=True)
        one = elane == idx
        onehots.append(one)
        vals.append(mk)
        idxs.append(idx)
        work = jnp.where(one, -jnp.inf, work)
    exps = [jnp.exp(v - vals[0]) for v in vals]
    denom = exps[0] + exps[1] + exps[2] + exps[3]

    sel = jnp.where(onehots[0] | onehots[1] | onehots[2] | onehots[3], 1.0, 0.0)
    trow = lax.broadcasted_iota(i32, (MG_TM, MG_TM), 0)
    tcol = lax.broadcasted_iota(i32, (MG_TM, MG_TM), 1)
    strict = jnp.where(tcol < trow, 1.0, 0.0).astype(bf16)
    rank = _dot(strict, sel.astype(bf16)) + cnt_ref[...]
    cnt_ref[...] = cnt_ref[...] + jnp.sum(sel, axis=0, keepdims=True)

    rlane = lax.broadcasted_iota(i32, (MG_TM, ROUTE_W), 1)
    rec = jnp.zeros((MG_TM, ROUTE_W), f32)
    for k in range(TOP_K):
        rk = jnp.sum(jnp.where(onehots[k], rank, 0.0), axis=1, keepdims=True)
        rec = jnp.where(rlane == R_IDX + k, idxs[k].astype(f32), rec)
        rec = jnp.where(rlane == R_GATE + k, exps[k] / denom, rec)
        rec = jnp.where(rlane == R_RANK + k, rk, rec)
    route_ref[...] = rec


def _merge(ya, yb, proj, x2, mod3, wa, wb, wo, g_ffn, wr_hi, wr_lo, b_r, seq):
    n = x2.shape[0]
    tiles_per_batch = seq // MG_TM
    const = lambda i: (0, 0)
    return pl.pallas_call(
        _merge_kernel,
        grid=(n // MG_TM,),
        in_specs=[
            pl.BlockSpec((MG_TM, SWA_QW), lambda i: (i, 0)),
            pl.BlockSpec((MG_TM, FOX_W), lambda i: (i, 0)),
            pl.BlockSpec((MG_TM, D), lambda i: (i, P_SA // D)),
            pl.BlockSpec((MG_TM, D), lambda i: (i, P_SB // D)),
            pl.BlockSpec((MG_TM, D), lambda i: (i, 0)),
            pl.BlockSpec((None, 1, 6 * D), lambda i: (i // tiles_per_batch, 0, 0)),
            pl.BlockSpec((SWA_QW, D), const),
            pl.BlockSpec((FOX_W, D), const),
            pl.BlockSpec((D, D), const),
            pl.BlockSpec((1, D), const),
            pl.BlockSpec((D, N_EXPERTS), const),
            pl.BlockSpec((D, N_EXPERTS), const),
            pl.BlockSpec((1, N_EXPERTS), const),
        ],
        out_specs=[
            pl.BlockSpec((MG_TM, D), lambda i: (i, 0)),
            pl.BlockSpec((MG_TM, D), lambda i: (i, 0)),
            pl.BlockSpec((MG_TM, ROUTE_W), lambda i: (i, 0)),
        ],
        out_shape=[
            jax.ShapeDtypeStruct((n, D), f32),
            jax.ShapeDtypeStruct((n, D), f32),
            jax.ShapeDtypeStruct((n, ROUTE_W), f32),
        ],
        scratch_shapes=[pltpu.VMEM((MG_TM, D), bf16), pltpu.VMEM((1, N_EXPERTS), f32)],
        compiler_params=_cparams(("arbitrary",)),
        name="merge_router",
    )(ya, yb, proj, proj, x2, mod3, wa, wb, wo, g_ffn, wr_hi, wr_lo, b_r)


MOE_TM = 256
GATHER_CH = 256


def _gather_kernel(tok_ref, x_ref, o_ref, sem):
    base = pl.program_id(0) * GATHER_CH

    def copy(r):
        return pltpu.make_async_copy(x_ref.at[pl.ds(tok_ref[0, 0, r], 1), :],
                                     o_ref.at[pl.ds(base + r, 1), :], sem)

    for r in range(GATHER_CH):
        copy(r).start()
    for r in range(GATHER_CH):
        copy(r).wait()


def _gather_rows(row_tok3, xn2, p_alloc):
    return pl.pallas_call(
        _gather_kernel,
        grid=(p_alloc // GATHER_CH,),
        in_specs=[
            pl.BlockSpec((1, 1, GATHER_CH), lambda i: (i, 0, 0), memory_space=pltpu.SMEM),
            pl.BlockSpec(memory_space=pl.ANY),
        ],
        out_specs=pl.BlockSpec(memory_space=pl.ANY),
        out_shape=jax.ShapeDtypeStruct((p_alloc, D), f32),
        scratch_shapes=[pltpu.SemaphoreType.DMA(())],
        compiler_params=_cparams(("arbitrary",)),
        name="moe_gather",
    )(row_tok3, xn2)


MOE_RB = 8
MOE_TF = 256
MOE_NF = D_EXPERT // MOE_TF


def _moe_kernel(ge_ref, gs_ref, gn_ref, gz_ref, x_hbm, wg_ref, wl_ref, bg_ref, bl_ref, wd_ref, bd_ref, y_hbm,
                xbuf, yacc, sem_in, sem_out):
    s = pl.program_id(0)
    f = pl.program_id(1)
    nb = gn_ref[s]
    nz = gz_ref[s]
    blk0 = gs_ref[s]

    def x_copy(r):
        return pltpu.make_async_copy(x_hbm.at[pl.ds((blk0 + r) * MOE_TM, MOE_TM)], xbuf.at[r], sem_in)

    def y_copy(r):
        return pltpu.make_async_copy(yacc.at[r], y_hbm.at[pl.ds((blk0 + r) * MOE_TM, MOE_TM)], sem_out)

    def z_copy(r):
        return pltpu.make_async_copy(yacc.at[0], y_hbm.at[pl.ds((blk0 + r) * MOE_TM, MOE_TM)], sem_out)

    @pl.when((f == 0) & (nz > 0))
    def _():
        yacc[0] = jnp.zeros((MOE_TM, D), f32)
        for r in range(MOE_RB):
            @pl.when(r < nz)
            def _():
                z_copy(r).start()
        for r in range(MOE_RB):
            @pl.when(r < nz)
            def _():
                z_copy(r).wait()

    @pl.when(f == 0)
    def _():
        for r in range(MOE_RB):
            @pl.when(r < nb)
            def _():
                x_copy(r).start()
        for r in range(MOE_RB):
            @pl.when(r < nb)
            def _():
                x_copy(r).wait()

    def row_block(r, _):
        x = xbuf[r].astype(bf16)
        glu = _dot(x, wg_ref[...]) + bg_ref[...]
        lin = _dot(x, wl_ref[...]) + bl_ref[...]
        glu = jnp.minimum(glu, SWIGLU_LIMIT)
        lin = jnp.clip(lin, -SWIGLU_LIMIT, SWIGLU_LIMIT)
        act = glu * jax.nn.sigmoid(SWIGLU_ALPHA * glu) * (lin + 1.0)
        part = _dot(act.astype(bf16), wd_ref[...])

        @pl.when(f == 0)
        def _():
            yacc[r] = part + bd_ref[...]

        @pl.when(f > 0)
        def _():
            yacc[r] = yacc[r] + part

        return 0

    lax.fori_loop(0, nb, row_block, 0)

    @pl.when(f == MOE_NF - 1)
    def _():
        for r in range(MOE_RB):
            @pl.when(r < nb)
            def _():
                y_copy(r).start()
        for r in range(MOE_RB):
            @pl.when(r < nb)
            def _():
                y_copy(r).wait()


def _moe(g_expert, g_start, g_nblk, g_nzero, x_rows, w_gu, b_gu3, w_down, b_down3, n_groups):
    p_alloc = x_rows.shape[0]

    def f_eff(s, f, gn):
        return jnp.where(gn[s] > 0, f, MOE_NF - 1)

    grid_spec = pltpu.PrefetchScalarGridSpec(
        num_scalar_prefetch=4,
        grid=(n_groups, MOE_NF),
        in_specs=[
            pl.BlockSpec(memory_space=pl.ANY),
            pl.BlockSpec((None, D, MOE_TF), lambda s, f, ge, gs, gn, gz: (ge[s], 0, f_eff(s, f, gn))),
            pl.BlockSpec((None, D, MOE_TF), lambda s, f, ge, gs, gn, gz: (ge[s], 0, MOE_NF + f_eff(s, f, gn))),
            pl.BlockSpec((None, 1, MOE_TF), lambda s, f, ge, gs, gn, gz: (ge[s], 0, f_eff(s, f, gn))),
            pl.BlockSpec((None, 1, MOE_TF), lambda s, f, ge, gs, gn, gz: (ge[s], 0, MOE_NF + f_eff(s, f, gn))),
            pl.BlockSpec((None, MOE_TF, D), lambda s, f, ge, gs, gn, gz: (ge[s], f_eff(s, f, gn), 0)),
            pl.BlockSpec((None, 1, D), lambda s, f, ge, gs, gn, gz: (ge[s], 0, 0)),
        ],
        out_specs=pl.BlockSpec(memory_space=pl.ANY),
        scratch_shapes=[
            pltpu.VMEM((MOE_RB, MOE_TM, D), f32),
            pltpu.VMEM((MOE_RB, MOE_TM, D), f32),
            pltpu.SemaphoreType.DMA(()),
            pltpu.SemaphoreType.DMA(()),
        ],
    )
    return pl.pallas_call(
        _moe_kernel,
        grid_spec=grid_spec,
        out_shape=jax.ShapeDtypeStruct((p_alloc, D), f32),
        compiler_params=_cparams(("arbitrary", "arbitrary")),
        name="moe_experts",
    )(g_expert, g_start, g_nblk, g_nzero, x_rows, w_gu, w_gu, b_gu3, b_gu3, w_down, b_down3)


CB_TM = 64


def _combine_kernel(dest_ref, y_hbm, h1_ref, route_ref, mod_ref, g_ref, o_ref, ybuf, sem):
    def copy(t, k):
        return pltpu.make_async_copy(y_hbm.at[pl.ds(dest_ref[0, 0, t * TOP_K + k], 1), :],
                                     ybuf.at[k, pl.ds(t, 1), :], sem)

    for t in range(CB_TM):
        for k in range(TOP_K):
            copy(t, k).start()
    for t in range(CB_TM):
        for k in range(TOP_K):
            copy(t, k).wait()

    route = route_ref[...]
    moe = jnp.zeros((CB_TM, D), f32)
    for k in range(TOP_K):
        moe = moe + route[:, R_GATE + k:R_GATE + k + 1] * ybuf[k]
    h = h1_ref[...] + mod_ref[:, 5 * D:6 * D] * moe
    ms = jnp.mean(h * h, axis=-1, keepdims=True)
    o_ref[...] = h * lax.rsqrt(ms + RMS_EPS) * g_ref[...]


def _combine(dest3, y_rows, h1, route, mod3, g_final, seq):
    n = h1.shape[0]
    tiles_per_batch = seq // CB_TM
    return pl.pallas_call(
        _combine_kernel,
        grid=(n // CB_TM,),
        in_specs=[
            pl.BlockSpec((1, 1, CB_TM * TOP_K), lambda i: (i, 0, 0), memory_space=pltpu.SMEM),
            pl.BlockSpec(memory_space=pl.ANY),
            pl.BlockSpec((CB_TM, D), lambda i: (i, 0)),
            pl.BlockSpec((CB_TM, ROUTE_W), lambda i: (i, 0)),
            pl.BlockSpec((None, 1, 6 * D), lambda i: (i // tiles_per_batch, 0, 0)),
            pl.BlockSpec((1, D), lambda i: (0, 0)),
        ],
        out_specs=pl.BlockSpec((CB_TM, D), lambda i: (i, 0)),
        out_shape=jax.ShapeDtypeStruct((n, D), f32),
        scratch_shapes=[pltpu.VMEM((TOP_K, CB_TM, D), f32), pltpu.SemaphoreType.DMA(())],
        compiler_params=_cparams(("arbitrary",)),
        name="moe_combine",
    )(dest3, y_rows, h1, route, mod3, g_final)


def _layer(h, c_act_pad, positions, w_ada, b_ada, g_mix, w_in, b_in, sinks, w_branch_a, w_branch_b, w_out,
           g_ffn, w_router, b_router, w_gu, b_gu, w_down, b_down):
    batch, seq, _ = h.shape
    n = batch * seq
    x2 = h.reshape(n, D)

    mod = _adaln(c_act_pad, w_ada, b_ada.reshape(1, 6 * D))
    mod3 = mod[:batch].reshape(batch, 1, 6 * D)

    w_cat = jnp.concatenate([w_in[:, GATE_OFF:], w_in[:, :QKV_W]], axis=1).astype(bf16)
    b_cat = jnp.concatenate([b_in[GATE_OFF:], b_in[:QKV_W]]).reshape(1, PROJ_W)
    w_f = w_in[:, FGATE_OFF:GATE_OFF].astype(bf16)
    b_f = b_in[FGATE_OFF:GATE_OFF].reshape(1, N_FGATE)

    inv_freq = 1.0 / (ROPE_THETA ** (jnp.arange(0, HD, 2, dtype=f32) / HD))
    ang = positions.astype(f32).reshape(n, 1) * inv_freq[None, :]
    cos_t = jnp.tile(jnp.cos(ang), (1, LANES // (HD // 2)))
    sin_h = jnp.sin(ang)
    sin_t = jnp.tile(jnp.concatenate([-sin_h, sin_h], axis=1), (1, LANES // HD))

    proj, logf = _inproj(x2, mod3, g_mix.reshape(1, D), w_cat, b_cat, w_f, b_f, cos_t, sin_t, seq)
    cum = _cumsum(logf.reshape(batch, seq, N_FGATE))

    ya = _swa(sinks, proj, batch, seq)
    yb = _fox(proj, cum, batch, seq)

    wr_hi = w_router.astype(bf16)
    wr_lo = (w_router - wr_hi.astype(f32)).astype(bf16)
    h1, xn2, route = _merge(ya, yb, proj, x2, mod3, w_branch_a.astype(bf16), w_branch_b.astype(bf16),
                            w_out.astype(bf16), g_ffn.reshape(1, D), wr_hi, wr_lo,
                            b_router.reshape(1, N_EXPERTS), seq)

    a_total = n * TOP_K
    top_idx = route[:, R_IDX:R_IDX + TOP_K].astype(i32)
    rank = route[:, R_RANK:R_RANK + TOP_K].astype(i32)
    counts = jnp.sum((top_idx[..., None] == jnp.arange(N_EXPERTS, dtype=i32)).astype(i32), axis=(0, 1))
    nblk_e = (counts + MOE_TM - 1) // MOE_TM
    blk_end = jnp.cumsum(nblk_e)
    blk_start = blk_end - nblk_e
    dest = (blk_start[top_idx] * MOE_TM + rank).reshape(a_total)
    p_alloc = a_total + N_EXPERTS * MOE_TM
    tok = jnp.arange(a_total, dtype=i32) // TOP_K
    row_tok = jnp.zeros((p_alloc,), i32).at[dest].set(tok)

    n_groups = (a_total // MOE_TM + N_EXPERTS + N_EXPERTS * (MOE_RB - 1)) // MOE_RB
    ng_e = (nblk_e + MOE_RB - 1) // MOE_RB
    g_end = jnp.cumsum(ng_e)
    gid = jnp.arange(n_groups, dtype=i32)
    g_exp_raw = jnp.searchsorted(g_end, gid, side="right").astype(i32)
    active = gid < g_end[-1]
    last_e = jnp.max(jnp.where(counts > 0, jnp.arange(N_EXPERTS, dtype=i32), 0))
    g_expert = jnp.where(active, jnp.minimum(g_exp_raw, N_EXPERTS - 1), last_e).astype(i32)
    g_local = gid - (g_end - ng_e)[g_expert]
    g_nblk = jnp.where(active, jnp.minimum(nblk_e[g_expert] - g_local * MOE_RB, MOE_RB), 0).astype(i32)
    tail0 = blk_end[-1] + (gid - g_end[-1]) * MOE_RB
    g_nzero = jnp.where(active, 0, jnp.clip(p_alloc // MOE_TM - tail0, 0, MOE_RB)).astype(i32)
    g_start = jnp.where(active, blk_start[g_expert] + g_local * MOE_RB,
                        jnp.minimum(tail0, p_alloc // MOE_TM)).astype(i32)

    x_rows = _gather_rows(row_tok.reshape(p_alloc // GATHER_CH, 1, GATHER_CH), xn2, p_alloc)
    y_rows = _moe(g_expert, g_start, g_nblk, g_nzero, x_rows, w_gu, b_gu.reshape(N_EXPERTS, 1, 2 * D_EXPERT),
                  w_down, b_down.reshape(N_EXPERTS, 1, D), n_groups)
    return dest, y_rows, h1, route, mod3


def kernel(x, c, positions, w_ada, b_ada, g_mix, w_in, b_in, sinks, w_branch_a, w_branch_b, w_out, g_ffn,
           w_router, b_router, w_gu, b_gu, w_down, b_down, g_final):
    batch, seq, _ = x.shape
    depth = w_ada.shape[0]
    assert depth == 1, "single-layer trunk"
    c_pad = jnp.zeros((8, D), f32).at[:batch].set(c)
    l = 0
    dest, y_rows, h1, route, mod3 = _layer(
        x, c_pad, positions, w_ada[l], b_ada[l], g_mix[l], w_in[l], b_in[l], sinks[l], w_branch_a[l],
        w_branch_b[l], w_out[l], g_ffn[l], w_router[l], b_router[l], w_gu[l], b_gu[l], w_down[l], b_down[l])
    n = batch * seq
    out = _combine(dest.reshape(n // CB_TM, 1, CB_TM * TOP_K), y_rows, h1, route, mod3,
                   g_final.reshape(1, D), seq)
    return out.reshape(batch, seq, D)
```

```python
import functools

import jax
import jax.numpy as jnp
from jax import lax
from jax.experimental import pallas as pl
from jax.experimental.pallas import tpu as pltpu

f32 = jnp.float32
bf16 = jnp.bfloat16
i32 = jnp.int32

D = 2048
HD = 64
SWA_QW = 1024
SWA_KVW = 256
FOX_W = 1024
N_FGATE = 16
WINDOW = 128
N_EXPERTS = 32
TOP_K = 4
D_EXPERT = 2048
SWIGLU_LIMIT = 7.0
SWIGLU_ALPHA = 1.702
RMS_EPS = 1e-5
NEG_INF = -1e30
ROPE_THETA = 10000.0

QKV_W = SWA_QW + 2 * SWA_KVW + 3 * FOX_W
FGATE_OFF = QKV_W
GATE_OFF = QKV_W + N_FGATE

LANES = 128
P_SA = 0
P_SB = D
P_QA = 2 * D
P_KA = P_QA + SWA_QW
P_VA = P_KA + SWA_KVW
P_QB = P_VA + SWA_KVW
P_KB = P_QB + FOX_W
P_VB = P_KB + FOX_W
PROJ_W = P_VB + FOX_W

VMEM_LIMIT = 56 * 1024 * 1024


def _cparams(sem, vmem=VMEM_LIMIT):
    return pltpu.CompilerParams(dimension_semantics=sem, vmem_limit_bytes=vmem)


def _dot(a, b):
    return jnp.dot(a, b, preferred_element_type=f32)


ADA_TN = 1024


def _adaln_kernel(c_ref, w_ref, b_ref, o_ref):
    c = c_ref[...]
    ca = c * jax.nn.sigmoid(c)
    o_ref[...] = _dot(ca, w_ref[...]) + b_ref[...]


def _adaln(c_pad, w_ada, b_ada):
    rows = c_pad.shape[0]
    n = w_ada.shape[1]
    return pl.pallas_call(
        _adaln_kernel,
        grid=(n // ADA_TN,),
        in_specs=[
            pl.BlockSpec((rows, D), lambda j: (0, 0)),
            pl.BlockSpec((D, ADA_TN), lambda j: (0, j)),
            pl.BlockSpec((1, ADA_TN), lambda j: (0, j)),
        ],
        out_specs=pl.BlockSpec((rows, ADA_TN), lambda j: (0, j)),
        out_shape=jax.ShapeDtypeStruct((rows, n), f32),
        compiler_params=_cparams(("arbitrary",)),
        name="adaln",
    )(c_pad, w_ada, b_ada)


IN_TM = 512
IN_TN = 512
IN_NJ = PROJ_W // IN_TN
J_QA0 = P_QA // IN_TN
J_KVA = P_KA // IN_TN
J_QB0 = P_QB // IN_TN
J_KB0 = P_KB // IN_TN


def _rope_block(a, cos, sin_signed, first_half):
    fwd = pltpu.roll(a, LANES - HD // 2, axis=1)
    bwd = pltpu.roll(a, HD // 2, axis=1)
    partner = jnp.where(first_half, fwd, bwd)
    return a * cos + partner * sin_signed


def _inproj_kernel(x_ref, mod_ref, g_ref, w_ref, b_ref, wf_ref, bf_ref, cos_ref, sin_ref,
                   proj_ref, logf_ref, xn_ref):
    j = pl.program_id(1)

    @pl.when(j == 0)
    def _():
        x = x_ref[...]
        ms = jnp.mean(x * x, axis=-1, keepdims=True)
        y = x * lax.rsqrt(ms + RMS_EPS) * g_ref[...]
        xn = y * (1.0 + mod_ref[:, D:2 * D]) + mod_ref[:, 0:D]
        xnb = xn.astype(bf16)
        xn_ref[...] = xnb
        fl = _dot(xnb, wf_ref[...]) + bf_ref[...]
        logf_ref[...] = jnp.minimum(fl, 0.0) - jnp.log(1.0 + jnp.exp(-jnp.abs(fl)))

    acc = _dot(xn_ref[...], w_ref[...]) + b_ref[...]

    lane = lax.broadcasted_iota(i32, (IN_TM, LANES), 1)
    first_half = (lane % HD) < (HD // 2)
    scale = HD ** -0.5

    def rope_cols(n_blocks, mult):
        cos = cos_ref[...]
        sin = sin_ref[...]
        for g in range(n_blocks):
            sl = slice(g * LANES, (g + 1) * LANES)
            r = _rope_block(acc[:, sl], cos, sin, first_half)
            proj_ref[:, sl] = (r * mult).astype(bf16)

    @pl.when(j < J_QA0)
    def _():
        proj_ref[...] = jax.nn.sigmoid(acc).astype(bf16)

    @pl.when((j >= J_QA0) & (j < J_KVA))
    def _():
        rope_cols(IN_TN // LANES, scale)

    @pl.when(j == J_KVA)
    def _():
        rope_cols(SWA_KVW // LANES, 1.0)
        proj_ref[:, SWA_KVW:] = acc[:, SWA_KVW:].astype(bf16)

    @pl.when((j >= J_QB0) & (j < J_KB0))
    def _():
        proj_ref[...] = (acc * scale).astype(bf16)

    @pl.when(j >= J_KB0)
    def _():
        proj_ref[...] = acc.astype(bf16)


def _inproj(x2, mod3, g_mix, w_cat, b_cat, w_f, b_f, cos_t, sin_t, seq):
    n = x2.shape[0]
    tiles_per_batch = seq // IN_TM
    return pl.pallas_call(
        _inproj_kernel,
        grid=(n // IN_TM, IN_NJ),
        in_specs=[
            pl.BlockSpec((IN_TM, D), lambda i, j: (i, 0)),
            pl.BlockSpec((None, 1, 6 * D), lambda i, j: (i // tiles_per_batch, 0, 0)),
            pl.BlockSpec((1, D), lambda i, j: (0, 0)),
            pl.BlockSpec((D, IN_TN), lambda i, j: (0, j)),
            pl.BlockSpec((1, IN_TN), lambda i, j: (0, j)),
            pl.BlockSpec((D, N_FGATE), lambda i, j: (0, 0)),
            pl.BlockSpec((1, N_FGATE), lambda i, j: (0, 0)),
            pl.BlockSpec((IN_TM, LANES), lambda i, j: (i, 0)),
            pl.BlockSpec((IN_TM, LANES), lambda i, j: (i, 0)),
        ],
        out_specs=[
            pl.BlockSpec((IN_TM, IN_TN), lambda i, j: (i, j)),
            pl.BlockSpec((IN_TM, N_FGATE), lambda i, j: (i, 0)),
        ],
        out_shape=[
            jax.ShapeDtypeStruct((n, PROJ_W), bf16),
            jax.ShapeDtypeStruct((n, N_FGATE), f32),
        ],
        scratch_shapes=[pltpu.VMEM((IN_TM, D), bf16)],
        compiler_params=_cparams(("arbitrary", "arbitrary")),
        name="inproj",
    )(x2, mod3, g_mix, w_cat, b_cat, w_f, b_f, cos_t, sin_t)


CUM_CH = 256


def _split3(x):
    x1 = x.astype(bf16)
    r1 = x - x1.astype(f32)
    x2 = r1.astype(bf16)
    r2 = r1 - x2.astype(f32)
    return x1, x2, r2.astype(bf16)


def _cumsum_kernel(x_ref, o_ref):
    seq = x_ref.shape[0]
    row = lax.broadcasted_iota(i32, (CUM_CH, CUM_CH), 0)
    col = lax.broadcasted_iota(i32, (CUM_CH, CUM_CH), 1)
    tri = jnp.where(col <= row, 1.0, 0.0).astype(bf16)
    carry = jnp.zeros((1, N_FGATE), f32)
    for c in range(seq // CUM_CH):
        sl = slice(c * CUM_CH, (c + 1) * CUM_CH)
        x1, x2, x3 = _split3(x_ref[sl, :])
        cs = _dot(tri, x1) + _dot(tri, x2) + _dot(tri, x3) + carry
        o_ref[sl, :] = cs
        carry = cs[CUM_CH - 1:CUM_CH, :]


def _cumsum(logf3):
    b, seq, h = logf3.shape
    return pl.pallas_call(
        _cumsum_kernel,
        grid=(b,),
        in_specs=[pl.BlockSpec((None, seq, h), lambda i: (i, 0, 0))],
        out_specs=pl.BlockSpec((None, seq, h), lambda i: (i, 0, 0)),
        out_shape=jax.ShapeDtypeStruct((b, seq, h), f32),
        compiler_params=_cparams(("arbitrary",)),
        name="fgate_cumsum",
    )(logf3)


SWA_QCOLS = 512


def _swa_kernel(sinks_ref, q_ref, k_ref, v_ref, o_ref, vt_ref):
    jp = pl.program_id(1)
    seq = q_ref.shape[0]
    nblk = seq // WINDOW

    for c in range(nblk):
        vblk = v_ref[c * WINDOW:(c + 1) * WINDOW, :].astype(f32)
        vt_ref[c] = vblk.T.astype(bf16)

    kj = lax.broadcasted_iota(i32, (2 * WINDOW, WINDOW), 0)
    qi = lax.broadcasted_iota(i32, (2 * WINDOW, WINDOW), 1)
    delta = qi + WINDOW - kj
    band = (delta >= 0) & (delta < WINDOW)
    zeros_half = jnp.zeros((HD, WINDOW), bf16)

    def body(n, _):
        prev = jnp.maximum(n - 1, 0)
        r0 = pl.multiple_of(n * WINDOW, WINDOW)
        p0 = pl.multiple_of(prev * WINDOW, WINDOW)
        kwin = jnp.concatenate([k_ref[pl.ds(p0, WINDOW), :], k_ref[pl.ds(r0, WINDOW), :]], axis=0)
        valid = band & ((n - 1) * WINDOW + kj >= 0)
        qblk = q_ref[pl.ds(r0, WINDOW), :]
        for t in range(SWA_QCOLS // LANES):
            g = t // 2
            qt = qblk[:, t * LANES:(t + 1) * LANES].astype(f32).T
            v_prev = vt_ref[prev, g * HD:(g + 1) * HD, :]
            v_cur = vt_ref[n, g * HD:(g + 1) * HD, :]
            outs = []
            for u in range(2):
                sink = sinks_ref[jp * 8 + t * 2 + u]
                feat = qt[u * HD:(u + 1) * HD, :].astype(bf16)
                wq = jnp.concatenate([feat, zeros_half] if g == 0 else [zeros_half, feat], axis=0)
                st = _dot(kwin, wq)
                st = jnp.where(valid, st, NEG_INF)
                m = jnp.maximum(jnp.max(st, axis=0, keepdims=True), sink)
                pt = jnp.exp(st - m)
                denom = jnp.sum(pt, axis=0, keepdims=True) + jnp.exp(sink - m)
                ptb = pt.astype(bf16)
                ot = _dot(v_prev, ptb[:WINDOW, :]) + _dot(v_cur, ptb[WINDOW:, :])
                outs.append(ot / denom)
            o_ref[pl.ds(r0, WINDOW), t * LANES:(t + 1) * LANES] = (
                jnp.concatenate(outs, axis=0).T.astype(bf16))
        return 0

    lax.fori_loop(0, nblk, body, 0)


def _swa(sinks, proj, batch, seq):
    n = proj.shape[0]
    qb0 = P_QA // SWA_QCOLS
    kb0 = P_KA // LANES
    vb0 = P_VA // LANES
    return pl.pallas_call(
        _swa_kernel,
        grid=(batch, SWA_QW // SWA_QCOLS),
        in_specs=[
            pl.BlockSpec(memory_space=pltpu.SMEM),
            pl.BlockSpec((seq, SWA_QCOLS), lambda b, j: (b, qb0 + j)),
            pl.BlockSpec((seq, LANES), lambda b, j: (b, kb0 + j)),
            pl.BlockSpec((seq, LANES), lambda b, j: (b, vb0 + j)),
        ],
        out_specs=pl.BlockSpec((seq, SWA_QCOLS), lambda b, j: (b, j)),
        out_shape=jax.ShapeDtypeStruct((n, SWA_QW), bf16),
        scratch_shapes=[pltpu.VMEM((seq // WINDOW, LANES, WINDOW), bf16)],
        compiler_params=_cparams(("arbitrary", "arbitrary")),
        name="swa_attention",
    )(sinks, proj, proj, proj)


FOX_T = 256


def _fox_kernel(q_ref, k_ref, v_ref, cum_ref, o_ref, vt_ref, kb_ref):
    p = pl.program_id(1)
    seq = q_ref.shape[0]
    nblk = seq // FOX_T

    for c in range(nblk):
        vblk = v_ref[c * FOX_T:(c + 1) * FOX_T, :].astype(f32)
        vt_ref[c] = vblk.T.astype(bf16)

    cum = cum_ref[...]
    hl = lax.broadcasted_iota(i32, cum.shape, 1)
    for u in range(2):
        col = jnp.sum(jnp.where(hl == 2 * p + u, cum, 0.0), axis=1, keepdims=True)
        kb_ref[u] = jnp.broadcast_to(col, (seq, LANES))

    frow = lax.broadcasted_iota(i32, (LANES, FOX_T), 0)
    kidx = lax.broadcasted_iota(i32, (FOX_T, FOX_T), 0)
    qidx = lax.broadcasted_iota(i32, (FOX_T, FOX_T), 1)
    causal = kidx <= qidx

    def scores(u, wq, k0):
        kblk = k_ref[pl.ds(k0, FOX_T), :]
        kb = kb_ref[u, pl.ds(k0, FOX_T), :]
        return _dot(kblk, wq) - jnp.concatenate([kb] * (FOX_T // LANES), axis=1)

    def update(u, st, jk, state):
        m, l, acc = state
        m_new = jnp.maximum(m, jnp.max(st, axis=0, keepdims=True))
        alpha = jnp.exp(m - m_new)
        pt = jnp.exp(st - m_new)
        l_new = alpha * l + jnp.sum(pt, axis=0, keepdims=True)
        vt = vt_ref[jk, u * HD:(u + 1) * HD, :]
        acc_new = alpha * acc + _dot(vt, pt.astype(bf16))
        return m_new, l_new, acc_new

    def qblock(iq, _):
        q0 = pl.multiple_of(iq * FOX_T, FOX_T)
        qt = q_ref[pl.ds(q0, FOX_T), :].astype(f32).T
        wqs = [jnp.where((frow // HD) == u, qt, 0.0).astype(bf16) for u in range(2)]
        init = tuple((jnp.full((1, FOX_T), NEG_INF, f32), jnp.zeros((1, FOX_T), f32),
                      jnp.zeros((HD, FOX_T), f32)) for _ in range(2))

        def kblock(jk, carry):
            sts, states = carry
            k1 = pl.multiple_of((jk + 1) * FOX_T, FOX_T)
            nxt = tuple(scores(u, wqs[u], k1) for u in range(2))
            return nxt, tuple(update(u, sts[u], jk, states[u]) for u in range(2))

        first = tuple(scores(u, wqs[u], 0) for u in range(2))
        sts, states = lax.fori_loop(0, iq, kblock, (first, init))
        outs = []
        for u in range(2):
            _, l, acc = update(u, jnp.where(causal, sts[u], NEG_INF), iq, states[u])
            outs.append(acc / l)
        o_ref[pl.ds(q0, FOX_T), :] = jnp.concatenate(outs, axis=0).T.astype(bf16)
        return 0

    lax.fori_loop(0, nblk, qblock, 0)


def _fox(proj, cum, batch, seq):
    n = proj.shape[0]
    qb0 = P_QB // LANES
    kb0 = P_KB // LANES
    vb0 = P_VB // LANES
    return pl.pallas_call(
        _fox_kernel,
        grid=(batch, FOX_W // LANES),
        in_specs=[
            pl.BlockSpec((seq, LANES), lambda b, p: (b, qb0 + p)),
            pl.BlockSpec((seq, LANES), lambda b, p: (b, kb0 + p)),
            pl.BlockSpec((seq, LANES), lambda b, p: (b, vb0 + p)),
            pl.BlockSpec((None, seq, N_FGATE), lambda b, p: (b, 0, 0)),
        ],
        out_specs=pl.BlockSpec((seq, LANES), lambda b, p: (b, p)),
        out_shape=jax.ShapeDtypeStruct((n, FOX_W), bf16),
        scratch_shapes=[pltpu.VMEM((seq // FOX_T, LANES, FOX_T), bf16),
                        pltpu.VMEM((2, seq, LANES), f32)],
        compiler_params=_cparams(("arbitrary", "arbitrary")),
        name="fox_attention",
    )(proj, proj, proj, cum)


MG_TM = 256
MG_TN = 512
ROUTE_W = LANES
R_IDX, R_GATE, R_RANK = 0, TOP_K, 2 * TOP_K


def _merge_kernel(ya_ref, yb_ref, sa_ref, sb_ref, x_ref, mod_ref, wa_ref, wb_ref, wo_ref, gf_ref,
                  wrh_ref, wrl_ref, br_ref, h1_ref, xn2_ref, route_ref, merged_ref, cnt_ref):
    i = pl.program_id(0)

    @pl.when(i == 0)
    def _():
        cnt_ref[...] = jnp.zeros_like(cnt_ref)

    ya = ya_ref[...]
    yb = yb_ref[...]
    for c in range(D // MG_TN):
        sl = slice(c * MG_TN, (c + 1) * MG_TN)
        ta = _dot(ya, wa_ref[:, sl])
        tb = _dot(yb, wb_ref[:, sl])
        merged_ref[:, sl] = (sa_ref[:, sl].astype(f32) * ta + sb_ref[:, sl].astype(f32) * tb).astype(bf16)

    merged = merged_ref[...]
    ssq = jnp.zeros((MG_TM, 1), f32)
    for c in range(D // MG_TN):
        sl = slice(c * MG_TN, (c + 1) * MG_TN)
        o = _dot(merged, wo_ref[:, sl])
        h = x_ref[:, sl] + mod_ref[:, 2 * D + c * MG_TN:2 * D + (c + 1) * MG_TN] * o
        h1_ref[:, sl] = h
        ssq = ssq + jnp.sum(h * h, axis=1, keepdims=True)

    inv = lax.rsqrt(ssq * (1.0 / D) + RMS_EPS)
    xn2 = h1_ref[...] * inv * gf_ref[...] * (1.0 + mod_ref[:, 4 * D:5 * D]) + mod_ref[:, 3 * D:4 * D]
    xn2_ref[...] = xn2
    hi = xn2.astype(bf16)
    lo = (xn2 - hi.astype(f32)).astype(bf16)
    wrh = wrh_ref[...]
    logits = _dot(hi, wrh) + _dot(lo, wrh) + _dot(hi, wrl_ref[...]) + br_ref[...]

    elane = lax.broadcasted_iota(i32, (MG_TM, N_EXPERTS), 1)
    work = logits
    onehots, vals, idxs = [], [], []
    for _ in range(TOP_K):
        mk = jnp.max(work, axis=1, keepdims=True)
        idx = jnp.min(jnp.where(work == mk, elane, N_EXPERTS), axis=1, keepdims=True)
        one = elane == idx
        onehots.append(one)
        vals.append(mk)
        idxs.append(idx)
        work = jnp.where(one, -jnp.inf, work)
    exps = [jnp.exp(v - vals[0]) for v in vals]
    denom = exps[0] + exps[1] + exps[2] + exps[3]

    sel = jnp.where(onehots[0] | onehots[1] | onehots[2] | onehots[3], 1.0, 0.0)
    trow = lax.broadcasted_iota(i32, (MG_TM, MG_TM), 0)
    tcol = lax.broadcasted_iota(i32, (MG_TM, MG_TM), 1)
    strict = jnp.where(tcol < trow, 1.0, 0.0).astype(bf16)
    rank = _dot(strict, sel.astype(bf16)) + cnt_ref[...]
    cnt_ref[...] = cnt_ref[...] + jnp.sum(sel, axis=0, keepdims=True)

    rlane = lax.broadcasted_iota(i32, (MG_TM, ROUTE_W), 1)
    rec = jnp.zeros((MG_TM, ROUTE_W), f32)
    for k in range(TOP_K):
        rk = jnp.sum(jnp.where(onehots[k], rank, 0.0), axis=1, keepdims=True)
        rec = jnp.where(rlane == R_IDX + k, idxs[k].astype(f32), rec)
        rec = jnp.where(rlane == R_GATE + k, exps[k] / denom, rec)
        rec = jnp.where(rlane == R_RANK + k, rk, rec)
    route_ref[...] = rec


def _merge(ya, yb, proj, x2, mod3, wa, wb, wo, g_ffn, wr_hi, wr_lo, b_r, seq):
    n = x2.shape[0]
    tiles_per_batch = seq // MG_TM
    const = lambda i: (0, 0)
    return pl.pallas_call(
        _merge_kernel,
        grid=(n // MG_TM,),
        in_specs=[
            pl.BlockSpec((MG_TM, SWA_QW), lambda i: (i, 0)),
            pl.BlockSpec((MG_TM, FOX_W), lambda i: (i, 0)),
            pl.BlockSpec((MG_TM, D), lambda i: (i, P_SA // D)),
            pl.BlockSpec((MG_TM, D), lambda i: (i, P_SB // D)),
            pl.BlockSpec((MG_TM, D), lambda i: (i, 0)),
            pl.BlockSpec((None, 1, 6 * D), lambda i: (i // tiles_per_batch, 0, 0)),
            pl.BlockSpec((SWA_QW, D), const),
            pl.BlockSpec((FOX_W, D), const),
            pl.BlockSpec((D, D), const),
            pl.BlockSpec((1, D), const),
            pl.BlockSpec((D, N_EXPERTS), const),
            pl.BlockSpec((D, N_EXPERTS), const),
            pl.BlockSpec((1, N_EXPERTS), const),
        ],
        out_specs=[
            pl.BlockSpec((MG_TM, D), lambda i: (i, 0)),
            pl.BlockSpec((MG_TM, D), lambda i: (i, 0)),
            pl.BlockSpec((MG_TM, ROUTE_W), lambda i: (i, 0)),
        ],
        out_shape=[
            jax.ShapeDtypeStruct((n, D), f32),
            jax.ShapeDtypeStruct((n, D), f32),
            jax.ShapeDtypeStruct((n, ROUTE_W), f32),
        ],
        scratch_shapes=[pltpu.VMEM((MG_TM, D), bf16), pltpu.VMEM((1, N_EXPERTS), f32)],
        compiler_params=_cparams(("arbitrary",)),
        name="merge_router",
    )(ya, yb, proj, proj, x2, mod3, wa, wb, wo, g_ffn, wr_hi, wr_lo, b_r)


MOE_TM = 256
GATHER_CH = 256


def _gather_kernel(tok_ref, x_ref, o_ref, sem):
    def copy(r):
        return pltpu.make_async_copy(x_ref.at[pl.ds(tok_ref[0, 0, r], 1), :],
                                     o_ref.at[pl.ds(r, 1), :], sem)

    for r in range(GATHER_CH):
        copy(r).start()
    for r in range(GATHER_CH):
        copy(r).wait()


def _gather_rows(row_tok3, xn2, p_alloc):
    return pl.pallas_call(
        _gather_kernel,
        grid=(p_alloc // GATHER_CH,),
        in_specs=[
            pl.BlockSpec((1, 1, GATHER_CH), lambda i: (i, 0, 0), memory_space=pltpu.SMEM),
            pl.BlockSpec(memory_space=pl.ANY),
        ],
        out_specs=pl.BlockSpec((GATHER_CH, D), lambda i: (i, 0)),
        out_shape=jax.ShapeDtypeStruct((p_alloc, D), f32),
        scratch_shapes=[pltpu.SemaphoreType.DMA(())],
        compiler_params=_cparams(("arbitrary",)),
        name="moe_gather",
    )(row_tok3, xn2)


MOE_RB = 8
MOE_TF = 256
MOE_NF = D_EXPERT // MOE_TF


def _moe_kernel(ge_ref, gs_ref, gn_ref, gz_ref, x_hbm, wg_ref, wl_ref, bg_ref, bl_ref, wd_ref, bd_ref, y_hbm,
                xbuf, yacc, sem_in, sem_out):
    s = pl.program_id(0)
    f = pl.program_id(1)
    nb = gn_ref[s]
    nz = gz_ref[s]
    blk0 = gs_ref[s]

    def x_copy(r):
        return pltpu.make_async_copy(x_hbm.at[pl.ds((blk0 + r) * MOE_TM, MOE_TM)], xbuf.at[r], sem_in)

    def y_copy(r):
        return pltpu.make_async_copy(yacc.at[r], y_hbm.at[pl.ds((blk0 + r) * MOE_TM, MOE_TM)], sem_out)

    def z_copy(r):
        return pltpu.make_async_copy(yacc.at[0], y_hbm.at[pl.ds((blk0 + r) * MOE_TM, MOE_TM)], sem_out)

    @pl.when((f == 0) & (nz > 0))
    def _():
        yacc[0] = jnp.zeros((MOE_TM, D), f32)
        for r in range(MOE_RB):
            @pl.when(r < nz)
            def _():
                z_copy(r).start()
        for r in range(MOE_RB):
            @pl.when(r < nz)
            def _():
                z_copy(r).wait()

    @pl.when(f == 0)
    def _():
        for r in range(MOE_RB):
            @pl.when(r < nb)
            def _():
                x_copy(r).start()
        for r in range(MOE_RB):
            @pl.when(r < nb)
            def _():
                x_copy(r).wait()

    def row_block(r, _):
        x = xbuf[r].astype(bf16)
        glu = _dot(x, wg_ref[...]) + bg_ref[...]
        lin = _dot(x, wl_ref[...]) + bl_ref[...]
        glu = jnp.minimum(glu, SWIGLU_LIMIT)
        lin = jnp.clip(lin, -SWIGLU_LIMIT, SWIGLU_LIMIT)
        act = glu * jax.nn.sigmoid(SWIGLU_ALPHA * glu) * (lin + 1.0)
        part = _dot(act.astype(bf16), wd_ref[...])

        @pl.when(f == 0)
        def _():
            yacc[r] = part + bd_ref[...]

        @pl.when(f > 0)
        def _():
            yacc[r] = yacc[r] + part

        return 0

    lax.fori_loop(0, nb, row_block, 0)

    @pl.when(f == MOE_NF - 1)
    def _():
        for r in range(MOE_RB):
            @pl.when(r < nb)
            def _():
                y_copy(r).start()
        for r in range(MOE_RB):
            @pl.when(r < nb)
            def _():
                y_copy(r).wait()


def _moe(g_expert, g_start, g_nblk, g_nzero, x_rows, w_gu, b_gu3, w_down, b_down3, n_groups):
    p_alloc = x_rows.shape[0]

    def f_eff(s, f, gn):
        return jnp.where(gn[s] > 0, f, MOE_NF - 1)

    grid_spec = pltpu.PrefetchScalarGridSpec(
        num_scalar_prefetch=4,
        grid=(n_groups, MOE_NF),
        in_specs=[
            pl.BlockSpec(memory_space=pl.ANY),
            pl.BlockSpec((None, D, MOE_TF), lambda s, f, ge, gs, gn, gz: (ge[s], 0, f_eff(s, f, gn))),
            pl.BlockSpec((None, D, MOE_TF), lambda s, f, ge, gs, gn, gz: (ge[s], 0, MOE_NF + f_eff(s, f, gn))),
            pl.BlockSpec((None, 1, MOE_TF), lambda s, f, ge, gs, gn, gz: (ge[s], 0, f_eff(s, f, gn))),
            pl.BlockSpec((None, 1, MOE_TF), lambda s, f, ge, gs, gn, gz: (ge[s], 0, MOE_NF + f_eff(s, f, gn))),
            pl.BlockSpec((None, MOE_TF, D), lambda s, f, ge, gs, gn, gz: (ge[s], f_eff(s, f, gn), 0)),
            pl.BlockSpec((None, 1, D), lambda s, f, ge, gs, gn, gz: (ge[s], 0, 0)),
        ],
        out_specs=pl.BlockSpec(memory_space=pl.ANY),
        scratch_shapes=[
            pltpu.VMEM((MOE_RB, MOE_TM, D), f32),
            pltpu.VMEM((MOE_RB, MOE_TM, D), f32),
            pltpu.SemaphoreType.DMA(()),
            pltpu.SemaphoreType.DMA(()),
        ],
    )
    return pl.pallas_call(
        _moe_kernel,
        grid_spec=grid_spec,
        out_shape=jax.ShapeDtypeStruct((p_alloc, D), f32),
        compiler_params=_cparams(("arbitrary", "arbitrary")),
        name="moe_experts",
    )(g_expert, g_start, g_nblk, g_nzero, x_rows, w_gu, w_gu, b_gu3, b_gu3, w_down, b_down3)


CB_TM = 64


def _combine_kernel(dest_ref, y_hbm, h1_ref, route_ref, mod_ref, g_ref, o_ref, ybuf, sem):
    def copy(t, k):
        return pltpu.make_async_copy(y_hbm.at[pl.ds(dest_ref[0, 0, t * TOP_K + k], 1), :],
                                     ybuf.at[k, pl.ds(t, 1), :], sem)

    for t in range(CB_TM):
        for k in range(TOP_K):
            copy(t, k).start()
    for t in range(CB_TM):
        for k in range(TOP_K):
            copy(t, k).wait()

    route = route_ref[...]
    moe = jnp.zeros((CB_TM, D), f32)
    for k in range(TOP_K):
        moe = moe + route[:, R_GATE + k:R_GATE + k + 1] * ybuf[k]
    h = h1_ref[...] + mod_ref[:, 5 * D:6 * D] * moe
    ms = jnp.mean(h * h, axis=-1, keepdims=True)
    o_ref[...] = h * lax.rsqrt(ms + RMS_EPS) * g_ref[...]


def _combine(dest3, y_rows, h1, route, mod3, g_final, seq):
    n = h1.shape[0]
    tiles_per_batch = seq // CB_TM
    return pl.pallas_call(
        _combine_kernel,
        grid=(n // CB_TM,),
        in_specs=[
            pl.BlockSpec((1, 1, CB_TM * TOP_K), lambda i: (i, 0, 0), memory_space=pltpu.SMEM),
            pl.BlockSpec(memory_space=pl.ANY),
            pl.BlockSpec((CB_TM, D), lambda i: (i, 0)),
            pl.BlockSpec((CB_TM, ROUTE_W), lambda i: (i, 0)),
            pl.BlockSpec((None, 1, 6 * D), lambda i: (i // tiles_per_batch, 0, 0)),
            pl.BlockSpec((1, D), lambda i: (0, 0)),
        ],
        out_specs=pl.BlockSpec((CB_TM, D), lambda i: (i, 0)),
        out_shape=jax.ShapeDtypeStruct((n, D), f32),
        scratch_shapes=[pltpu.VMEM((TOP_K, CB_TM, D), f32), pltpu.SemaphoreType.DMA(())],
        compiler_params=_cparams(("arbitrary",)),
        name="moe_combine",
    )(dest3, y_rows, h1, route, mod3, g_final)


def _layer(h, c_act_pad, positions, w_ada, b_ada, g_mix, w_in, b_in, sinks, w_branch_a, w_branch_b, w_out,
           g_ffn, w_router, b_router, w_gu, b_gu, w_down, b_down):
    batch, seq, _ = h.shape
    n = batch * seq
    x2 = h.reshape(n, D)

    mod = _adaln(c_act_pad, w_ada, b_ada.reshape(1, 6 * D))
    mod3 = mod[:batch].reshape(batch, 1, 6 * D)

    w_cat = jnp.concatenate([w_in[:, GATE_OFF:], w_in[:, :QKV_W]], axis=1).astype(bf16)
    b_cat = jnp.concatenate([b_in[GATE_OFF:], b_in[:QKV_W]]).reshape(1, PROJ_W)
    w_f = w_in[:, FGATE_OFF:GATE_OFF].astype(bf16)
    b_f = b_in[FGATE_OFF:GATE_OFF].reshape(1, N_FGATE)

    inv_freq = 1.0 / (ROPE_THETA ** (jnp.arange(0, HD, 2, dtype=f32) / HD))
    ang = positions.astype(f32).reshape(n, 1) * inv_freq[None, :]
    cos_t = jnp.tile(jnp.cos(ang), (1, LANES // (HD // 2)))
    sin_h = jnp.sin(ang)
    sin_t = jnp.tile(jnp.concatenate([-sin_h, sin_h], axis=1), (1, LANES // HD))

    proj, logf = _inproj(x2, mod3, g_mix.reshape(1, D), w_cat, b_cat, w_f, b_f, cos_t, sin_t, seq)
    cum = _cumsum(logf.reshape(batch, seq, N_FGATE))

    ya = _swa(sinks, proj, batch, seq)
    yb = _fox(proj, cum, batch, seq)

    wr_hi = w_router.astype(bf16)
    wr_lo = (w_router - wr_hi.astype(f32)).astype(bf16)
    h1, xn2, route = _merge(ya, yb, proj, x2, mod3, w_branch_a.astype(bf16), w_branch_b.astype(bf16),
                            w_out.astype(bf16), g_ffn.reshape(1, D), wr_hi, wr_lo,
                            b_router.reshape(1, N_EXPERTS), seq)

    a_total = n * TOP_K
    top_idx = route[:, R_IDX:R_IDX + TOP_K].astype(i32)
    rank = route[:, R_RANK:R_RANK + TOP_K].astype(i32)
    counts = jnp.sum((top_idx[..., None] == jnp.arange(N_EXPERTS, dtype=i32)).astype(i32), axis=(0, 1))
    nblk_e = (counts + MOE_TM - 1) // MOE_TM
    blk_end = jnp.cumsum(nblk_e)
    blk_start = blk_end - nblk_e
    dest = (blk_start[top_idx] * MOE_TM + rank).reshape(a_total)
    p_alloc = a_total + N_EXPERTS * MOE_TM
    tok = jnp.arange(a_total, dtype=i32) // TOP_K
    row_tok = jnp.zeros((p_alloc,), i32).at[dest].set(tok)

    n_groups = (a_total // MOE_TM + N_EXPERTS + N_EXPERTS * (MOE_RB - 1)) // MOE_RB
    ng_e = (nblk_e + MOE_RB - 1) // MOE_RB
    g_end = jnp.cumsum(ng_e)
    gid = jnp.arange(n_groups, dtype=i32)
    g_exp_raw = jnp.searchsorted(g_end, gid, side="right").astype(i32)
    active = gid < g_end[-1]
    last_e = jnp.max(jnp.where(counts > 0, jnp.arange(N_EXPERTS, dtype=i32), 0))
    g_expert = jnp.where(active, jnp.minimum(g_exp_raw, N_EXPERTS - 1), last_e).astype(i32)
    g_local = gid - (g_end - ng_e)[g_expert]
    g_nblk = jnp.where(active, jnp.minimum(nblk_e[g_expert] - g_local * MOE_RB, MOE_RB), 0).astype(i32)
    tail0 = blk_end[-1] + (gid - g_end[-1]) * MOE_RB
    g_nzero = jnp.where(active, 0, jnp.clip(p_alloc // MOE_TM - tail0, 0, MOE_RB)).astype(i32)
    g_start = jnp.where(active, blk_start[g_expert] + g_local * MOE_RB,
                        jnp.minimum(tail0, p_alloc // MOE_TM)).astype(i32)

    x_rows = _gather_rows(row_tok.reshape(p_alloc // GATHER_CH, 1, GATHER_CH), xn2, p_alloc)
    y_rows = _moe(g_expert, g_start, g_nblk, g_nzero, x_rows, w_gu, b_gu.reshape(N_EXPERTS, 1, 2 * D_EXPERT),
                  w_down, b_down.reshape(N_EXPERTS, 1, D), n_groups)
    return dest, y_rows, h1, route, mod3


def kernel(x, c, positions, w_ada, b_ada, g_mix, w_in, b_in, sinks, w_branch_a, w_branch_b, w_out, g_ffn,
           w_router, b_router, w_gu, b_gu, w_down, b_down, g_final):
    batch, seq, _ = x.shape
    depth = w_ada.shape[0]
    assert depth == 1, "single-layer trunk"
    c_pad = jnp.zeros((8, D), f32).at[:batch].set(c)
    l = 0
    dest, y_rows, h1, route, mod3 = _layer(
        x, c_pad, positions, w_ada[l], b_ada[l], g_mix[l], w_in[l], b_in[l], sinks[l], w_branch_a[l],
        w_branch_b[l], w_out[l], g_ffn[l], w_router[l], b_router[l], w_gu[l], b_gu[l], w_down[l], b_down[l])
    n = batch * seq
    out = _combine(dest.reshape(n // CB_TM, 1, CB_TM * TOP_K), y_rows, h1, route, mod3,
                   g_final.reshape(1, D), seq)
    return out.reshape(batch, seq, D)
```

```python
import functools

import jax
import jax.numpy as jnp
from jax import lax
from jax.experimental import pallas as pl
from jax.experimental.pallas import tpu as pltpu

f32 = jnp.float32
bf16 = jnp.bfloat16
i32 = jnp.int32

D = 2048
HD = 64
SWA_QW = 1024
SWA_KVW = 256
FOX_W = 1024
N_FGATE = 16
WINDOW = 128
N_EXPERTS = 32
TOP_K = 4
D_EXPERT = 2048
SWIGLU_LIMIT = 7.0
SWIGLU_ALPHA = 1.702
RMS_EPS = 1e-5
NEG_INF = -1e30
ROPE_THETA = 10000.0

QKV_W = SWA_QW + 2 * SWA_KVW + 3 * FOX_W
FGATE_OFF = QKV_W
GATE_OFF = QKV_W + N_FGATE

LANES = 128
P_SA = 0
P_SB = D
P_QA = 2 * D
P_KA = P_QA + SWA_QW
P_VA = P_KA + SWA_KVW
P_QB = P_VA + SWA_KVW
P_KB = P_QB + FOX_W
P_VB = P_KB + FOX_W
PROJ_W = P_VB + FOX_W

VMEM_LIMIT = 56 * 1024 * 1024


def _cparams(sem, vmem=VMEM_LIMIT):
    return pltpu.CompilerParams(dimension_semantics=sem, vmem_limit_bytes=vmem)


def _dot(a, b):
    return jnp.dot(a, b, preferred_element_type=f32)


ADA_TN = 1024


def _adaln_kernel(c_ref, w_ref, b_ref, o_ref):
    c = c_ref[...]
    ca = c * jax.nn.sigmoid(c)
    o_ref[...] = _dot(ca, w_ref[...]) + b_ref[...]


def _adaln(c_pad, w_ada, b_ada):
    rows = c_pad.shape[0]
    n = w_ada.shape[1]
    return pl.pallas_call(
        _adaln_kernel,
        grid=(n // ADA_TN,),
        in_specs=[
            pl.BlockSpec((rows, D), lambda j: (0, 0)),
            pl.BlockSpec((D, ADA_TN), lambda j: (0, j)),
            pl.BlockSpec((1, ADA_TN), lambda j: (0, j)),
        ],
        out_specs=pl.BlockSpec((rows, ADA_TN), lambda j: (0, j)),
        out_shape=jax.ShapeDtypeStruct((rows, n), f32),
        compiler_params=_cparams(("arbitrary",)),
        name="adaln",
    )(c_pad, w_ada, b_ada)


IN_TM = 512
IN_TN = 512
IN_NJ = PROJ_W // IN_TN
J_QA0 = P_QA // IN_TN
J_KVA = P_KA // IN_TN
J_QB0 = P_QB // IN_TN
J_KB0 = P_KB // IN_TN


def _rope_block(a, cos, sin_signed, first_half):
    fwd = pltpu.roll(a, LANES - HD // 2, axis=1)
    bwd = pltpu.roll(a, HD // 2, axis=1)
    partner = jnp.where(first_half, fwd, bwd)
    return a * cos + partner * sin_signed


def _inproj_kernel(x_ref, mod_ref, g_ref, w_ref, b_ref, wf_ref, bf_ref, cos_ref, sin_ref,
                   proj_ref, logf_ref, xn_ref):
    j = pl.program_id(1)

    @pl.when(j == 0)
    def _():
        x = x_ref[...]
        ms = jnp.mean(x * x, axis=-1, keepdims=True)
        y = x * lax.rsqrt(ms + RMS_EPS) * g_ref[...]
        xn = y * (1.0 + mod_ref[:, D:2 * D]) + mod_ref[:, 0:D]
        xnb = xn.astype(bf16)
        xn_ref[...] = xnb
        fl = _dot(xnb, wf_ref[...]) + bf_ref[...]
        logf_ref[...] = jnp.minimum(fl, 0.0) - jnp.log(1.0 + jnp.exp(-jnp.abs(fl)))

    acc = _dot(xn_ref[...], w_ref[...]) + b_ref[...]

    lane = lax.broadcasted_iota(i32, (IN_TM, LANES), 1)
    first_half = (lane % HD) < (HD // 2)
    scale = HD ** -0.5

    def rope_cols(n_blocks, mult):
        cos = cos_ref[...]
        sin = sin_ref[...]
        for g in range(n_blocks):
            sl = slice(g * LANES, (g + 1) * LANES)
            r = _rope_block(acc[:, sl], cos, sin, first_half)
            proj_ref[:, sl] = (r * mult).astype(bf16)

    @pl.when(j < J_QA0)
    def _():
        proj_ref[...] = jax.nn.sigmoid(acc).astype(bf16)

    @pl.when((j >= J_QA0) & (j < J_KVA))
    def _():
        rope_cols(IN_TN // LANES, scale)

    @pl.when(j == J_KVA)
    def _():
        rope_cols(SWA_KVW // LANES, 1.0)
        proj_ref[:, SWA_KVW:] = acc[:, SWA_KVW:].astype(bf16)

    @pl.when((j >= J_QB0) & (j < J_KB0))
    def _():
        proj_ref[...] = (acc * scale).astype(bf16)

    @pl.when(j >= J_KB0)
    def _():
        proj_ref[...] = acc.astype(bf16)


def _inproj(x2, mod3, g_mix, w_cat, b_cat, w_f, b_f, cos_t, sin_t, seq):
    n = x2.shape[0]
    tiles_per_batch = seq // IN_TM
    return pl.pallas_call(
        _inproj_kernel,
        grid=(n // IN_TM, IN_NJ),
        in_specs=[
            pl.BlockSpec((IN_TM, D), lambda i, j: (i, 0)),
            pl.BlockSpec((None, 1, 6 * D), lambda i, j: (i // tiles_per_batch, 0, 0)),
            pl.BlockSpec((1, D), lambda i, j: (0, 0)),
            pl.BlockSpec((D, IN_TN), lambda i, j: (0, j)),
            pl.BlockSpec((1, IN_TN), lambda i, j: (0, j)),
            pl.BlockSpec((D, N_FGATE), lambda i, j: (0, 0)),
            pl.BlockSpec((1, N_FGATE), lambda i, j: (0, 0)),
            pl.BlockSpec((IN_TM, LANES), lambda i, j: (i, 0)),
            pl.BlockSpec((IN_TM, LANES), lambda i, j: (i, 0)),
        ],
        out_specs=[
            pl.BlockSpec((IN_TM, IN_TN), lambda i, j: (i, j)),
            pl.BlockSpec((IN_TM, N_FGATE), lambda i, j: (i, 0)),
        ],
        out_shape=[
            jax.ShapeDtypeStruct((n, PROJ_W), bf16),
            jax.ShapeDtypeStruct((n, N_FGATE), f32),
        ],
        scratch_shapes=[pltpu.VMEM((IN_TM, D), bf16)],
        compiler_params=_cparams(("arbitrary", "arbitrary")),
        name="inproj",
    )(x2, mod3, g_mix, w_cat, b_cat, w_f, b_f, cos_t, sin_t)


CUM_CH = 256


def _split3(x):
    x1 = x.astype(bf16)
    r1 = x - x1.astype(f32)
    x2 = r1.astype(bf16)
    r2 = r1 - x2.astype(f32)
    return x1, x2, r2.astype(bf16)


def _cumsum_kernel(x_ref, o_ref):
    seq = x_ref.shape[0]
    row = lax.broadcasted_iota(i32, (CUM_CH, CUM_CH), 0)
    col = lax.broadcasted_iota(i32, (CUM_CH, CUM_CH), 1)
    tri = jnp.where(col <= row, 1.0, 0.0).astype(bf16)
    carry = jnp.zeros((1, N_FGATE), f32)
    for c in range(seq // CUM_CH):
        sl = slice(c * CUM_CH, (c + 1) * CUM_CH)
        x1, x2, x3 = _split3(x_ref[sl, :])
        cs = _dot(tri, x1) + _dot(tri, x2) + _dot(tri, x3) + carry
        o_ref[sl, :] = cs
        carry = cs[CUM_CH - 1:CUM_CH, :]


def _cumsum(logf3):
    b, seq, h = logf3.shape
    return pl.pallas_call(
        _cumsum_kernel,
        grid=(b,),
        in_specs=[pl.BlockSpec((None, seq, h), lambda i: (i, 0, 0))],
        out_specs=pl.BlockSpec((None, seq, h), lambda i: (i, 0, 0)),
        out_shape=jax.ShapeDtypeStruct((b, seq, h), f32),
        compiler_params=_cparams(("arbitrary",)),
        name="fgate_cumsum",
    )(logf3)


SWA_QCOLS = 512


def _swa_kernel(sinks_ref, q_ref, k_ref, v_ref, o_ref, vt_ref):
    jp = pl.program_id(1)
    seq = q_ref.shape[0]
    nblk = seq // WINDOW

    for c in range(nblk):
        vblk = v_ref[c * WINDOW:(c + 1) * WINDOW, :].astype(f32)
        vt_ref[c] = vblk.T.astype(bf16)

    kj = lax.broadcasted_iota(i32, (2 * WINDOW, WINDOW), 0)
    qi = lax.broadcasted_iota(i32, (2 * WINDOW, WINDOW), 1)
    delta = qi + WINDOW - kj
    band = (delta >= 0) & (delta < WINDOW)
    zeros_half = jnp.zeros((HD, WINDOW), bf16)

    def body(n, _):
        prev = jnp.maximum(n - 1, 0)
        r0 = pl.multiple_of(n * WINDOW, WINDOW)
        p0 = pl.multiple_of(prev * WINDOW, WINDOW)
        kwin = jnp.concatenate([k_ref[pl.ds(p0, WINDOW), :], k_ref[pl.ds(r0, WINDOW), :]], axis=0)
        valid = band & ((n - 1) * WINDOW + kj >= 0)
        qblk = q_ref[pl.ds(r0, WINDOW), :]
        for t in range(SWA_QCOLS // LANES):
            g = t // 2
            qt = qblk[:, t * LANES:(t + 1) * LANES].astype(f32).T
            v_prev = vt_ref[prev, g * HD:(g + 1) * HD, :]
            v_cur = vt_ref[n, g * HD:(g + 1) * HD, :]
            outs = []
            for u in range(2):
                sink = sinks_ref[jp * 8 + t * 2 + u]
                feat = qt[u * HD:(u + 1) * HD, :].astype(bf16)
                wq = jnp.concatenate([feat, zeros_half] if g == 0 else [zeros_half, feat], axis=0)
                st = _dot(kwin, wq)
                st = jnp.where(valid, st, NEG_INF)
                m = jnp.maximum(jnp.max(st, axis=0, keepdims=True), sink)
                pt = jnp.exp(st - m)
                denom = jnp.sum(pt, axis=0, keepdims=True) + jnp.exp(sink - m)
                ptb = pt.astype(bf16)
                ot = _dot(v_prev, ptb[:WINDOW, :]) + _dot(v_cur, ptb[WINDOW:, :])
                outs.append(ot / denom)
            o_ref[pl.ds(r0, WINDOW), t * LANES:(t + 1) * LANES] = (
                jnp.concatenate(outs, axis=0).T.astype(bf16))
        return 0

    lax.fori_loop(0, nblk, body, 0)


def _swa(sinks, proj, batch, seq):
    n = proj.shape[0]
    qb0 = P_QA // SWA_QCOLS
    kb0 = P_KA // LANES
    vb0 = P_VA // LANES
    return pl.pallas_call(
        _swa_kernel,
        grid=(batch, SWA_QW // SWA_QCOLS),
        in_specs=[
            pl.BlockSpec(memory_space=pltpu.SMEM),
            pl.BlockSpec((seq, SWA_QCOLS), lambda b, j: (b, qb0 + j)),
            pl.BlockSpec((seq, LANES), lambda b, j: (b, kb0 + j)),
            pl.BlockSpec((seq, LANES), lambda b, j: (b, vb0 + j)),
        ],
        out_specs=pl.BlockSpec((seq, SWA_QCOLS), lambda b, j: (b, j)),
        out_shape=jax.ShapeDtypeStruct((n, SWA_QW), bf16),
        scratch_shapes=[pltpu.VMEM((seq // WINDOW, LANES, WINDOW), bf16)],
        compiler_params=_cparams(("arbitrary", "arbitrary")),
        name="swa_attention",
    )(sinks, proj, proj, proj)


FOX_T = 256


def _fox_kernel(q_ref, k_ref, v_ref, cum_ref, o_ref, vt_ref, kb_ref):
    p = pl.program_id(1)
    seq = q_ref.shape[0]
    nblk = seq // FOX_T

    for c in range(nblk):
        vblk = v_ref[c * FOX_T:(c + 1) * FOX_T, :].astype(f32)
        vt_ref[c] = vblk.T.astype(bf16)

    cum = cum_ref[...]
    hl = lax.broadcasted_iota(i32, cum.shape, 1)
    for u in range(2):
        col = jnp.sum(jnp.where(hl == 2 * p + u, cum, 0.0), axis=1, keepdims=True)
        kb_ref[u] = jnp.broadcast_to(col, (seq, LANES))

    frow = lax.broadcasted_iota(i32, (LANES, FOX_T), 0)
    kidx = lax.broadcasted_iota(i32, (FOX_T, FOX_T), 0)
    qidx = lax.broadcasted_iota(i32, (FOX_T, FOX_T), 1)
    causal = kidx <= qidx

    def scores(u, wq, k0):
        kblk = k_ref[pl.ds(k0, FOX_T), :]
        kb = kb_ref[u, pl.ds(k0, FOX_T), :]
        return _dot(kblk, wq) - jnp.concatenate([kb] * (FOX_T // LANES), axis=1)

    def update(u, st, jk, state):
        m, l, acc = state
        m_new = jnp.maximum(m, jnp.max(st, axis=0, keepdims=True))
        alpha = jnp.exp(m - m_new)
        pt = jnp.exp(st - m_new)
        l_new = alpha * l + jnp.sum(pt, axis=0, keepdims=True)
        vt = vt_ref[jk, u * HD:(u + 1) * HD, :]
        acc_new = alpha * acc + _dot(vt, pt.astype(bf16))
        return m_new, l_new, acc_new

    def qblock(iq, _):
        q0 = pl.multiple_of(iq * FOX_T, FOX_T)
        qt = q_ref[pl.ds(q0, FOX_T), :].astype(f32).T
        wqs = [jnp.where((frow // HD) == u, qt, 0.0).astype(bf16) for u in range(2)]
        init = tuple((jnp.full((1, FOX_T), NEG_INF, f32), jnp.zeros((1, FOX_T), f32),
                      jnp.zeros((HD, FOX_T), f32)) for _ in range(2))

        def kblock(jk, carry):
            sts, states = carry
            k1 = pl.multiple_of((jk + 1) * FOX_T, FOX_T)
            nxt = tuple(scores(u, wqs[u], k1) for u in range(2))
            return nxt, tuple(update(u, sts[u], jk, states[u]) for u in range(2))

        first = tuple(scores(u, wqs[u], 0) for u in range(2))
        sts, states = lax.fori_loop(0, iq, kblock, (first, init))
        outs = []
        for u in range(2):
            _, l, acc = update(u, jnp.where(causal, sts[u], NEG_INF), iq, states[u])
            outs.append(acc / l)
        o_ref[pl.ds(q0, FOX_T), :] = jnp.concatenate(outs, axis=0).T.astype(bf16)
        return 0

    lax.fori_loop(0, nblk, qblock, 0)


def _fox(proj, cum, batch, seq):
    n = proj.shape[0]
    qb0 = P_QB // LANES
    kb0 = P_KB // LANES
    vb0 = P_VB // LANES
    return pl.pallas_call(
        _fox_kernel,
        grid=(batch, FOX_W // LANES),
        in_specs=[
            pl.BlockSpec((seq, LANES), lambda b, p: (b, qb0 + p)),
            pl.BlockSpec((seq, LANES), lambda b, p: (b, kb0 + p)),
            pl.BlockSpec((seq, LANES), lambda b, p: (b, vb0 + p)),
            pl.BlockSpec((None, seq, N_FGATE), lambda b, p: (b, 0, 0)),
        ],
        out_specs=pl.BlockSpec((seq, LANES), lambda b, p: (b, p)),
        out_shape=jax.ShapeDtypeStruct((n, FOX_W), bf16),
        scratch_shapes=[pltpu.VMEM((seq // FOX_T, LANES, FOX_T), bf16),
                        pltpu.VMEM((2, seq, LANES), f32)],
        compiler_params=_cparams(("arbitrary", "arbitrary")),
        name="fox_attention",
    )(proj, proj, proj, cum)


MG_TM = 256
MG_TN = 512
ROUTE_W = LANES
R_IDX, R_GATE, R_RANK = 0, TOP_K, 2 * TOP_K


def _merge_kernel(ya_ref, yb_ref, sa_ref, sb_ref, x_ref, mod_ref, wa_ref, wb_ref, wo_ref, gf_ref,
                  wrh_ref, wrl_ref, br_ref, h1_ref, xn2_ref, route_ref, merged_ref, cnt_ref):
    i = pl.program_id(0)

    @pl.when(i == 0)
    def _():
        cnt_ref[...] = jnp.zeros_like(cnt_ref)

    ya = ya_ref[...]
    yb = yb_ref[...]
    for c in range(D // MG_TN):
        sl = slice(c * MG_TN, (c + 1) * MG_TN)
        ta = _dot(ya, wa_ref[:, sl])
        tb = _dot(yb, wb_ref[:, sl])
        merged_ref[:, sl] = (sa_ref[:, sl].astype(f32) * ta + sb_ref[:, sl].astype(f32) * tb).astype(bf16)

    merged = merged_ref[...]
    ssq = jnp.zeros((MG_TM, 1), f32)
    for c in range(D // MG_TN):
        sl = slice(c * MG_TN, (c + 1) * MG_TN)
        o = _dot(merged, wo_ref[:, sl])
        h = x_ref[:, sl] + mod_ref[:, 2 * D + c * MG_TN:2 * D + (c + 1) * MG_TN] * o
        h1_ref[:, sl] = h
        ssq = ssq + jnp.sum(h * h, axis=1, keepdims=True)

    inv = lax.rsqrt(ssq * (1.0 / D) + RMS_EPS)
    xn2 = h1_ref[...] * inv * gf_ref[...] * (1.0 + mod_ref[:, 4 * D:5 * D]) + mod_ref[:, 3 * D:4 * D]
    xn2_ref[...] = xn2.reshape(MG_TM, D // LANES, LANES)
    hi = xn2.astype(bf16)
    lo = (xn2 - hi.astype(f32)).astype(bf16)
    wrh = wrh_ref[...]
    logits = _dot(hi, wrh) + _dot(lo, wrh) + _dot(hi, wrl_ref[...]) + br_ref[...]

    elane = lax.broadcasted_iota(i32, (MG_TM, N_EXPERTS), 1)
    work = logits
    onehots, vals, idxs = [], [], []
    for _ in range(TOP_K):
        mk = jnp.max(work, axis=1, keepdims=True)
        idx = jnp.min(jnp.where(work == mk, elane, N_EXPERTS), axis=1, keepdims=True)
        one = elane == idx
        onehots.append(one)
        vals.append(mk)
        idxs.append(idx)
        work = jnp.where(one, -jnp.inf, work)
    exps = [jnp.exp(v - vals[0]) for v in vals]
    denom = exps[0] + exps[1] + exps[2] + exps[3]

    sel = jnp.where(onehots[0] | onehots[1] | onehots[2] | onehots[3], 1.0, 0.0)
    trow = lax.broadcasted_iota(i32, (MG_TM, MG_TM), 0)
    tcol = lax.broadcasted_iota(i32, (MG_TM, MG_TM), 1)
    strict = jnp.where(tcol < trow, 1.0, 0.0).astype(bf16)
    rank = _dot(strict, sel.astype(bf16)) + cnt_ref[...]
    cnt_ref[...] = cnt_ref[...] + jnp.sum(sel, axis=0, keepdims=True)

    rlane = lax.broadcasted_iota(i32, (MG_TM, ROUTE_W), 1)
    rec = jnp.zeros((MG_TM, ROUTE_W), f32)
    for k in range(TOP_K):
        rk = jnp.sum(jnp.where(onehots[k], rank, 0.0), axis=1, keepdims=True)
        rec = jnp.where(rlane == R_IDX + k, idxs[k].astype(f32), rec)
        rec = jnp.where(rlane == R_GATE + k, exps[k] / denom, rec)
        rec = jnp.where(rlane == R_RANK + k, rk, rec)
    route_ref[...] = rec


def _merge(ya, yb, proj, x2, mod3, wa, wb, wo, g_ffn, wr_hi, wr_lo, b_r, seq):
    n = x2.shape[0]
    tiles_per_batch = seq // MG_TM
    const = lambda i: (0, 0)
    return pl.pallas_call(
        _merge_kernel,
        grid=(n // MG_TM,),
        in_specs=[
            pl.BlockSpec((MG_TM, SWA_QW), lambda i: (i, 0)),
            pl.BlockSpec((MG_TM, FOX_W), lambda i: (i, 0)),
            pl.BlockSpec((MG_TM, D), lambda i: (i, P_SA // D)),
            pl.BlockSpec((MG_TM, D), lambda i: (i, P_SB // D)),
            pl.BlockSpec((MG_TM, D), lambda i: (i, 0)),
            pl.BlockSpec((None, 1, 6 * D), lambda i: (i // tiles_per_batch, 0, 0)),
            pl.BlockSpec((SWA_QW, D), const),
            pl.BlockSpec((FOX_W, D), const),
            pl.BlockSpec((D, D), const),
            pl.BlockSpec((1, D), const),
            pl.BlockSpec((D, N_EXPERTS), const),
            pl.BlockSpec((D, N_EXPERTS), const),
            pl.BlockSpec((1, N_EXPERTS), const),
        ],
        out_specs=[
            pl.BlockSpec((MG_TM, D), lambda i: (i, 0)),
            pl.BlockSpec((MG_TM, D // LANES, LANES), lambda i: (i, 0, 0)),
            pl.BlockSpec((MG_TM, ROUTE_W), lambda i: (i, 0)),
        ],
        out_shape=[
            jax.ShapeDtypeStruct((n, D), f32),
            jax.ShapeDtypeStruct((n, D // LANES, LANES), f32),
            jax.ShapeDtypeStruct((n, ROUTE_W), f32),
        ],
        scratch_shapes=[pltpu.VMEM((MG_TM, D), bf16), pltpu.VMEM((1, N_EXPERTS), f32)],
        compiler_params=_cparams(("arbitrary",)),
        name="merge_router",
    )(ya, yb, proj, proj, x2, mod3, wa, wb, wo, g_ffn, wr_hi, wr_lo, b_r)


MOE_TM = 256
GATHER_CH = 256


SLAB = D // LANES


def _gather_kernel(tok_ref, tok_next_ref, x_ref, o_ref, stage, sems):
    i = pl.program_id(0)
    n = pl.num_programs(0)
    slot = i % 2

    def copies(idx_ref, sl):
        return [pltpu.make_async_copy(x_ref.at[idx_ref[0, 0, r]], stage.at[sl, r], sems.at[sl])
                for r in range(GATHER_CH)]

    def start(idx_ref, sl):
        for c in copies(idx_ref, sl):
            c.start()

    @pl.when(i == 0)
    def _():
        start(tok_ref, 0)

    for sl in range(2):
        @pl.when((i + 1 < n) & (slot == 1 - sl))
        def _():
            start(tok_next_ref, sl)

    for sl in range(2):
        @pl.when(slot == sl)
        def _():
            for c in copies(tok_ref, sl):
                c.wait()
            o_ref[...] = stage[sl].reshape(GATHER_CH, D).astype(bf16)


def _gather_rows(row_tok3, xn2_slab, p_alloc):
    n_chunks = p_alloc // GATHER_CH
    return pl.pallas_call(
        _gather_kernel,
        grid=(n_chunks,),
        in_specs=[
            pl.BlockSpec((1, 1, GATHER_CH), lambda i: (i, 0, 0), memory_space=pltpu.SMEM),
            pl.BlockSpec((1, 1, GATHER_CH), lambda i: (jnp.minimum(i + 1, n_chunks - 1), 0, 0),
                         memory_space=pltpu.SMEM),
            pl.BlockSpec(memory_space=pl.ANY),
        ],
        out_specs=pl.BlockSpec((GATHER_CH, D), lambda i: (i, 0)),
        out_shape=jax.ShapeDtypeStruct((p_alloc, D), bf16),
        scratch_shapes=[pltpu.VMEM((2, GATHER_CH, SLAB, LANES), f32), pltpu.SemaphoreType.DMA((2,))],
        compiler_params=_cparams(("arbitrary",)),
        name="moe_gather",
    )(row_tok3, row_tok3, xn2_slab)


MOE_RB = 8
MOE_TF = 512
MOE_NF = D_EXPERT // MOE_TF
MOE_VMEM = 60 * 1024 * 1024


def _moe_kernel(ge_ref, gs_ref, gn_ref, gz_ref, x_hbm, wg_ref, wl_ref, bg_ref, bl_ref, wd_ref, bd_ref, y_hbm,
                xbuf, yacc, yst, sem_in, sem_out):
    s = pl.program_id(0)
    f = pl.program_id(1)
    nb = gn_ref[s]
    nz = gz_ref[s]
    blk0 = gs_ref[s]

    def x_copy(r):
        return pltpu.make_async_copy(x_hbm.at[pl.ds((blk0 + r) * MOE_TM, MOE_TM)], xbuf.at[r], sem_in)

    def y_copy(r, slot):
        return pltpu.make_async_copy(yst.at[slot], y_hbm.at[pl.ds((blk0 + r) * MOE_TM, MOE_TM)],
                                     sem_out.at[slot])

    @pl.when((f == 0) & (nz > 0))
    def _():
        yst[0] = jnp.zeros((MOE_TM, SLAB, LANES), f32)
        for r in range(MOE_RB):
            @pl.when(r < nz)
            def _():
                y_copy(r, 0).start()
        for r in range(MOE_RB):
            @pl.when(r < nz)
            def _():
                y_copy(r, 0).wait()

    @pl.when(f == 0)
    def _():
        for r in range(MOE_RB):
            @pl.when(r < nb)
            def _():
                x_copy(r).start()

        def init(r, _):
            yacc[r] = jnp.broadcast_to(bd_ref[...], (MOE_TM, D))
            return 0

        lax.fori_loop(0, nb, init, 0)
        for r in range(MOE_RB):
            @pl.when(r < nb)
            def _():
                x_copy(r).wait()

    def up(r):
        x = xbuf[r]
        glu = _dot(x, wg_ref[...]) + bg_ref[...]
        lin = _dot(x, wl_ref[...]) + bl_ref[...]
        glu = jnp.minimum(glu, SWIGLU_LIMIT)
        lin = jnp.clip(lin, -SWIGLU_LIMIT, SWIGLU_LIMIT)
        return (glu * jax.nn.sigmoid(SWIGLU_ALPHA * glu) * (lin + 1.0)).astype(bf16)

    def down(r, act):
        yacc[r] = yacc[r] + _dot(act, wd_ref[...])

    @pl.when(nb > 0)
    def _():
        def body(r, act_prev):
            down(r - 1, act_prev)
            return up(r)

        act_last = lax.fori_loop(1, nb, body, up(0))
        down(nb - 1, act_last)

    @pl.when((f == MOE_NF - 1) & (nb > 0))
    def _():
        def emit(r, _):
            slot = r % 2

            @pl.when(r >= 2)
            def _():
                y_copy(r - 2, slot).wait()

            yst[slot] = yacc[r].reshape(MOE_TM, SLAB, LANES)
            y_copy(r, slot).start()
            return 0

        lax.fori_loop(0, nb, emit, 0)

        @pl.when(nb >= 2)
        def _():
            y_copy(nb - 2, nb % 2).wait()

        y_copy(nb - 1, (nb - 1) % 2).wait()


def _moe(g_expert, g_start, g_nblk, g_nzero, x_rows, w_gu, b_gu3, w_down, b_down3, n_groups):
    p_alloc = x_rows.shape[0]

    def f_eff(s, f, gn):
        return jnp.where(gn[s] > 0, f, MOE_NF - 1)

    grid_spec = pltpu.PrefetchScalarGridSpec(
        num_scalar_prefetch=4,
        grid=(n_groups, MOE_NF),
        in_specs=[
            pl.BlockSpec(memory_space=pl.ANY),
            pl.BlockSpec((None, D, MOE_TF), lambda s, f, ge, gs, gn, gz: (ge[s], 0, f_eff(s, f, gn))),
            pl.BlockSpec((None, D, MOE_TF), lambda s, f, ge, gs, gn, gz: (ge[s], 0, MOE_NF + f_eff(s, f, gn))),
            pl.BlockSpec((None, 1, MOE_TF), lambda s, f, ge, gs, gn, gz: (ge[s], 0, f_eff(s, f, gn))),
            pl.BlockSpec((None, 1, MOE_TF), lambda s, f, ge, gs, gn, gz: (ge[s], 0, MOE_NF + f_eff(s, f, gn))),
            pl.BlockSpec((None, MOE_TF, D), lambda s, f, ge, gs, gn, gz: (ge[s], f_eff(s, f, gn), 0)),
            pl.BlockSpec((None, 1, D), lambda s, f, ge, gs, gn, gz: (ge[s], 0, 0)),
        ],
        out_specs=pl.BlockSpec(memory_space=pl.ANY),
        scratch_shapes=[
            pltpu.VMEM((MOE_RB, MOE_TM, D), bf16),
            pltpu.VMEM((MOE_RB, MOE_TM, D), f32),
            pltpu.VMEM((2, MOE_TM, SLAB, LANES), f32),
            pltpu.SemaphoreType.DMA(()),
            pltpu.SemaphoreType.DMA((2,)),
        ],
    )
    return pl.pallas_call(
        _moe_kernel,
        grid_spec=grid_spec,
        out_shape=jax.ShapeDtypeStruct((p_alloc, SLAB, LANES), f32),
        compiler_params=_cparams(("arbitrary", "arbitrary"), MOE_VMEM),
        name="moe_experts",
    )(g_expert, g_start, g_nblk, g_nzero, x_rows, w_gu, w_gu, b_gu3, b_gu3, w_down, b_down3)


CB_TM = 64


def _combine_kernel(dest_ref, dest_next_ref, y_hbm, h1_ref, route_ref, mod_ref, g_ref, o_ref, ybuf, sems):
    i = pl.program_id(0)
    n = pl.num_programs(0)
    slot = i % 2

    def copies(idx_ref, sl):
        return [pltpu.make_async_copy(y_hbm.at[idx_ref[0, 0, t * TOP_K + k]], ybuf.at[sl, k, t], sems.at[sl])
                for t in range(CB_TM) for k in range(TOP_K)]

    def start(idx_ref, sl):
        for c in copies(idx_ref, sl):
            c.start()

    @pl.when(i == 0)
    def _():
        start(dest_ref, 0)

    for sl in range(2):
        @pl.when((i + 1 < n) & (slot == 1 - sl))
        def _():
            start(dest_next_ref, sl)

    route = route_ref[...]
    gt2 = mod_ref[:, 5 * D:6 * D]
    for sl in range(2):
        @pl.when(slot == sl)
        def _():
            for c in copies(dest_ref, sl):
                c.wait()
            moe = jnp.zeros((CB_TM, D), f32)
            for k in range(TOP_K):
                moe = moe + route[:, R_GATE + k:R_GATE + k + 1] * ybuf[sl, k].reshape(CB_TM, D)
            h = h1_ref[...] + gt2 * moe
            ms = jnp.mean(h * h, axis=-1, keepdims=True)
            o_ref[...] = h * lax.rsqrt(ms + RMS_EPS) * g_ref[...]


def _combine(dest3, y_rows, h1, route, mod3, g_final, seq):
    n = h1.shape[0]
    tiles_per_batch = seq // CB_TM
    n_tiles = n // CB_TM
    return pl.pallas_call(
        _combine_kernel,
        grid=(n_tiles,),
        in_specs=[
            pl.BlockSpec((1, 1, CB_TM * TOP_K), lambda i: (i, 0, 0), memory_space=pltpu.SMEM),
            pl.BlockSpec((1, 1, CB_TM * TOP_K), lambda i: (jnp.minimum(i + 1, n_tiles - 1), 0, 0),
                         memory_space=pltpu.SMEM),
            pl.BlockSpec(memory_space=pl.ANY),
            pl.BlockSpec((CB_TM, D), lambda i: (i, 0)),
            pl.BlockSpec((CB_TM, ROUTE_W), lambda i: (i, 0)),
            pl.BlockSpec((None, 1, 6 * D), lambda i: (i // tiles_per_batch, 0, 0)),
            pl.BlockSpec((1, D), lambda i: (0, 0)),
        ],
        out_specs=pl.BlockSpec((CB_TM, D), lambda i: (i, 0)),
        out_shape=jax.ShapeDtypeStruct((n, D), f32),
        scratch_shapes=[pltpu.VMEM((2, TOP_K, CB_TM, SLAB, LANES), f32), pltpu.SemaphoreType.DMA((2,))],
        compiler_params=_cparams(("arbitrary",)),
        name="moe_combine",
    )(dest3, dest3, y_rows, h1, route, mod3, g_final)


def _layer(h, c_act_pad, positions, w_ada, b_ada, g_mix, w_in, b_in, sinks, w_branch_a, w_branch_b, w_out,
           g_ffn, w_router, b_router, w_gu, b_gu, w_down, b_down):
    batch, seq, _ = h.shape
    n = batch * seq
    x2 = h.reshape(n, D)

    mod = _adaln(c_act_pad, w_ada, b_ada.reshape(1, 6 * D))
    mod3 = mod[:batch].reshape(batch, 1, 6 * D)

    w_cat = jnp.concatenate([w_in[:, GATE_OFF:], w_in[:, :QKV_W]], axis=1).astype(bf16)
    b_cat = jnp.concatenate([b_in[GATE_OFF:], b_in[:QKV_W]]).reshape(1, PROJ_W)
    w_f = w_in[:, FGATE_OFF:GATE_OFF].astype(bf16)
    b_f = b_in[FGATE_OFF:GATE_OFF].reshape(1, N_FGATE)

    inv_freq = 1.0 / (ROPE_THETA ** (jnp.arange(0, HD, 2, dtype=f32) / HD))
    ang = positions.astype(f32).reshape(n, 1) * inv_freq[None, :]
    cos_t = jnp.tile(jnp.cos(ang), (1, LANES // (HD // 2)))
    sin_h = jnp.sin(ang)
    sin_t = jnp.tile(jnp.concatenate([-sin_h, sin_h], axis=1), (1, LANES // HD))

    proj, logf = _inproj(x2, mod3, g_mix.reshape(1, D), w_cat, b_cat, w_f, b_f, cos_t, sin_t, seq)
    cum = _cumsum(logf.reshape(batch, seq, N_FGATE))

    ya = _swa(sinks, proj, batch, seq)
    yb = _fox(proj, cum, batch, seq)

    wr_hi = w_router.astype(bf16)
    wr_lo = (w_router - wr_hi.astype(f32)).astype(bf16)
    h1, xn2, route = _merge(ya, yb, proj, x2, mod3, w_branch_a.astype(bf16), w_branch_b.astype(bf16),
                            w_out.astype(bf16), g_ffn.reshape(1, D), wr_hi, wr_lo,
                            b_router.reshape(1, N_EXPERTS), seq)

    a_total = n * TOP_K
    top_idx = route[:, R_IDX:R_IDX + TOP_K].astype(i32)
    rank = route[:, R_RANK:R_RANK + TOP_K].astype(i32)
    counts = jnp.sum((top_idx[..., None] == jnp.arange(N_EXPERTS, dtype=i32)).astype(i32), axis=(0, 1))
    nblk_e = (counts + MOE_TM - 1) // MOE_TM
    blk_end = jnp.cumsum(nblk_e)
    blk_start = blk_end - nblk_e
    dest = (blk_start[top_idx] * MOE_TM + rank).reshape(a_total)
    p_alloc = a_total + N_EXPERTS * MOE_TM
    tok = jnp.arange(a_total, dtype=i32) // TOP_K
    row_tok = jnp.zeros((p_alloc,), i32).at[dest].set(tok)

    n_groups = (a_total // MOE_TM + N_EXPERTS + N_EXPERTS * (MOE_RB - 1)) // MOE_RB
    ng_e = (nblk_e + MOE_RB - 1) // MOE_RB
    g_end = jnp.cumsum(ng_e)
    gid = jnp.arange(n_groups, dtype=i32)
    g_exp_raw = jnp.searchsorted(g_end, gid, side="right").astype(i32)
    active = gid < g_end[-1]
    last_e = jnp.max(jnp.where(counts > 0, jnp.arange(N_EXPERTS, dtype=i32), 0))
    g_expert = jnp.where(active, jnp.minimum(g_exp_raw, N_EXPERTS - 1), last_e).astype(i32)
    g_local = gid - (g_end - ng_e)[g_expert]
    g_nblk = jnp.where(active, jnp.minimum(nblk_e[g_expert] - g_local * MOE_RB, MOE_RB), 0).astype(i32)
    tail0 = blk_end[-1] + (gid - g_end[-1]) * MOE_RB
    g_nzero = jnp.where(active, 0, jnp.clip(p_alloc // MOE_TM - tail0, 0, MOE_RB)).astype(i32)
    g_start = jnp.where(active, blk_start[g_expert] + g_local * MOE_RB,
                        jnp.minimum(tail0, p_alloc // MOE_TM)).astype(i32)

    x_rows = _gather_rows(row_tok.reshape(p_alloc // GATHER_CH, 1, GATHER_CH), xn2, p_alloc)
    y_rows = _moe(g_expert, g_start, g_nblk, g_nzero, x_rows, w_gu, b_gu.reshape(N_EXPERTS, 1, 2 * D_EXPERT),
                  w_down, b_down.reshape(N_EXPERTS, 1, D), n_groups)
    return dest, y_rows, h1, route, mod3


def kernel(x, c, positions, w_ada, b_ada, g_mix, w_in, b_in, sinks, w_branch_a, w_branch_b, w_out, g_ffn,
           w_router, b_router, w_gu, b_gu, w_down, b_down, g_final):
    batch, seq, _ = x.shape
    depth = w_ada.shape[0]
    assert depth == 1, "single-layer trunk"
    c_pad = jnp.zeros((8, D), f32).at[:batch].set(c)
    l = 0
    dest, y_rows, h1, route, mod3 = _layer(
        x, c_pad, positions, w_ada[l], b_ada[l], g_mix[l], w_in[l], b_in[l], sinks[l], w_branch_a[l],
        w_branch_b[l], w_out[l], g_ffn[l], w_router[l], b_router[l], w_gu[l], b_gu[l], w_down[l], b_down[l])
    n = batch * seq
    out = _combine(dest.reshape(n // CB_TM, 1, CB_TM * TOP_K), y_rows, h1, route, mod3,
                   g_final.reshape(1, D), seq)
    return out.reshape(batch, seq, D)
```

```python
import functools

import jax
import jax.numpy as jnp
from jax import lax
from jax.experimental import pallas as pl
from jax.experimental.pallas import tpu as pltpu

f32 = jnp.float32
bf16 = jnp.bfloat16
i32 = jnp.int32

D = 2048
HD = 64
SWA_QW = 1024
SWA_KVW = 256
FOX_W = 1024
N_FGATE = 16
WINDOW = 128
N_EXPERTS = 32
TOP_K = 4
D_EXPERT = 2048
SWIGLU_LIMIT = 7.0
SWIGLU_ALPHA = 1.702
RMS_EPS = 1e-5
NEG_INF = -1e30
ROPE_THETA = 10000.0

QKV_W = SWA_QW + 2 * SWA_KVW + 3 * FOX_W
FGATE_OFF = QKV_W
GATE_OFF = QKV_W + N_FGATE

LANES = 128
P_SA = 0
P_SB = D
P_QA = 2 * D
P_KA = P_QA + SWA_QW
P_VA = P_KA + SWA_KVW
P_QB = P_VA + SWA_KVW
P_KB = P_QB + FOX_W
P_VB = P_KB + FOX_W
PROJ_W = P_VB + FOX_W

VMEM_LIMIT = 56 * 1024 * 1024


def _cparams(sem, vmem=VMEM_LIMIT):
    return pltpu.CompilerParams(dimension_semantics=sem, vmem_limit_bytes=vmem)


def _dot(a, b):
    return jnp.dot(a, b, preferred_element_type=f32)


ADA_TN = 1024


def _adaln_kernel(c_ref, w_ref, b_ref, o_ref):
    c = c_ref[...]
    ca = c * jax.nn.sigmoid(c)
    o_ref[...] = _dot(ca, w_ref[...]) + b_ref[...]


def _adaln(c_pad, w_ada, b_ada):
    rows = c_pad.shape[0]
    n = w_ada.shape[1]
    return pl.pallas_call(
        _adaln_kernel,
        grid=(n // ADA_TN,),
        in_specs=[
            pl.BlockSpec((rows, D), lambda j: (0, 0)),
            pl.BlockSpec((D, ADA_TN), lambda j: (0, j)),
            pl.BlockSpec((1, ADA_TN), lambda j: (0, j)),
        ],
        out_specs=pl.BlockSpec((rows, ADA_TN), lambda j: (0, j)),
        out_shape=jax.ShapeDtypeStruct((rows, n), f32),
        compiler_params=_cparams(("arbitrary",)),
        name="adaln",
    )(c_pad, w_ada, b_ada)


IN_TM = 1024
IN_TN = 512
IN_NJ = PROJ_W // IN_TN
J_QA0 = P_QA // IN_TN
J_KVA = P_KA // IN_TN
J_QB0 = P_QB // IN_TN
J_KB0 = P_KB // IN_TN


def _rope_block(a, cos, sin_signed, first_half):
    fwd = pltpu.roll(a, LANES - HD // 2, axis=1)
    bwd = pltpu.roll(a, HD // 2, axis=1)
    partner = jnp.where(first_half, fwd, bwd)
    return a * cos + partner * sin_signed


def _inproj_kernel(x_ref, mod_ref, g_ref, w_ref, b_ref, wf_ref, bf_ref, cos_ref, sin_ref,
                   proj_ref, logf_ref, xn_ref):
    j = pl.program_id(1)

    @pl.when(j == 0)
    def _():
        x = x_ref[...]
        ms = jnp.mean(x * x, axis=-1, keepdims=True)
        y = x * lax.rsqrt(ms + RMS_EPS) * g_ref[...]
        xn = y * (1.0 + mod_ref[:, D:2 * D]) + mod_ref[:, 0:D]
        xnb = xn.astype(bf16)
        xn_ref[...] = xnb
        fl = _dot(xnb, wf_ref[...]) + bf_ref[...]
        logf_ref[...] = jnp.minimum(fl, 0.0) - jnp.log(1.0 + jnp.exp(-jnp.abs(fl)))

    acc = _dot(xn_ref[...], w_ref[...]) + b_ref[...]

    lane = lax.broadcasted_iota(i32, (IN_TM, LANES), 1)
    first_half = (lane % HD) < (HD // 2)
    scale = HD ** -0.5

    def rope_cols(n_blocks, mult):
        cos = cos_ref[...]
        sin = sin_ref[...]
        for g in range(n_blocks):
            sl = slice(g * LANES, (g + 1) * LANES)
            r = _rope_block(acc[:, sl], cos, sin, first_half)
            proj_ref[:, sl] = (r * mult).astype(bf16)

    @pl.when(j < J_QA0)
    def _():
        proj_ref[...] = jax.nn.sigmoid(acc).astype(bf16)

    @pl.when((j >= J_QA0) & (j < J_KVA))
    def _():
        rope_cols(IN_TN // LANES, scale)

    @pl.when(j == J_KVA)
    def _():
        rope_cols(SWA_KVW // LANES, 1.0)
        proj_ref[:, SWA_KVW:] = acc[:, SWA_KVW:].astype(bf16)

    @pl.when((j >= J_QB0) & (j < J_KB0))
    def _():
        proj_ref[...] = (acc * scale).astype(bf16)

    @pl.when(j >= J_KB0)
    def _():
        proj_ref[...] = acc.astype(bf16)


def _inproj(x2, mod3, g_mix, w_cat, b_cat, w_f, b_f, cos_t, sin_t, seq):
    n = x2.shape[0]
    tiles_per_batch = seq // IN_TM
    return pl.pallas_call(
        _inproj_kernel,
        grid=(n // IN_TM, IN_NJ),
        in_specs=[
            pl.BlockSpec((IN_TM, D), lambda i, j: (i, 0)),
            pl.BlockSpec((None, 1, 6 * D), lambda i, j: (i // tiles_per_batch, 0, 0)),
            pl.BlockSpec((1, D), lambda i, j: (0, 0)),
            pl.BlockSpec((D, IN_TN), lambda i, j: (0, j)),
            pl.BlockSpec((1, IN_TN), lambda i, j: (0, j)),
            pl.BlockSpec((D, N_FGATE), lambda i, j: (0, 0)),
            pl.BlockSpec((1, N_FGATE), lambda i, j: (0, 0)),
            pl.BlockSpec((IN_TM, LANES), lambda i, j: (i, 0)),
            pl.BlockSpec((IN_TM, LANES), lambda i, j: (i, 0)),
        ],
        out_specs=[
            pl.BlockSpec((IN_TM, IN_TN), lambda i, j: (i, j)),
            pl.BlockSpec((IN_TM, N_FGATE), lambda i, j: (i, 0)),
        ],
        out_shape=[
            jax.ShapeDtypeStruct((n, PROJ_W), bf16),
            jax.ShapeDtypeStruct((n, N_FGATE), f32),
        ],
        scratch_shapes=[pltpu.VMEM((IN_TM, D), bf16)],
        compiler_params=_cparams(("arbitrary", "arbitrary")),
        name="inproj",
    )(x2, mod3, g_mix, w_cat, b_cat, w_f, b_f, cos_t, sin_t)


CUM_CH = 256


def _split3(x):
    x1 = x.astype(bf16)
    r1 = x - x1.astype(f32)
    x2 = r1.astype(bf16)
    r2 = r1 - x2.astype(f32)
    return x1, x2, r2.astype(bf16)


def _cumsum_kernel(x_ref, o_ref):
    seq = x_ref.shape[0]
    row = lax.broadcasted_iota(i32, (CUM_CH, CUM_CH), 0)
    col = lax.broadcasted_iota(i32, (CUM_CH, CUM_CH), 1)
    tri = jnp.where(col <= row, 1.0, 0.0).astype(bf16)
    carry = jnp.zeros((1, N_FGATE), f32)
    for c in range(seq // CUM_CH):
        sl = slice(c * CUM_CH, (c + 1) * CUM_CH)
        x1, x2, x3 = _split3(x_ref[sl, :])
        cs = _dot(tri, x1) + _dot(tri, x2) + _dot(tri, x3) + carry
        o_ref[sl, :] = cs
        carry = cs[CUM_CH - 1:CUM_CH, :]


def _cumsum(logf3):
    b, seq, h = logf3.shape
    return pl.pallas_call(
        _cumsum_kernel,
        grid=(b,),
        in_specs=[pl.BlockSpec((None, seq, h), lambda i: (i, 0, 0))],
        out_specs=pl.BlockSpec((None, seq, h), lambda i: (i, 0, 0)),
        out_shape=jax.ShapeDtypeStruct((b, seq, h), f32),
        compiler_params=_cparams(("arbitrary",)),
        name="fgate_cumsum",
    )(logf3)


SWA_QCOLS = 512


def _swa_kernel(sinks_ref, q_ref, k_ref, v_ref, o_ref, vt_ref):
    jp = pl.program_id(1)
    seq = q_ref.shape[0]
    nblk = seq // WINDOW

    for c in range(nblk):
        vblk = v_ref[c * WINDOW:(c + 1) * WINDOW, :].astype(f32)
        vt_ref[c] = vblk.T.astype(bf16)

    kj = lax.broadcasted_iota(i32, (2 * WINDOW, WINDOW), 0)
    qi = lax.broadcasted_iota(i32, (2 * WINDOW, WINDOW), 1)
    delta = qi + WINDOW - kj
    band = (delta >= 0) & (delta < WINDOW)
    zeros_half = jnp.zeros((HD, WINDOW), bf16)

    def body(n, _):
        prev = jnp.maximum(n - 1, 0)
        r0 = pl.multiple_of(n * WINDOW, WINDOW)
        p0 = pl.multiple_of(prev * WINDOW, WINDOW)
        kwin = jnp.concatenate([k_ref[pl.ds(p0, WINDOW), :], k_ref[pl.ds(r0, WINDOW), :]], axis=0)
        valid = band & ((n - 1) * WINDOW + kj >= 0)
        qblk = q_ref[pl.ds(r0, WINDOW), :]
        for t in range(SWA_QCOLS // LANES):
            g = t // 2
            qt = qblk[:, t * LANES:(t + 1) * LANES].astype(f32).T
            v_prev = vt_ref[prev, g * HD:(g + 1) * HD, :]
            v_cur = vt_ref[n, g * HD:(g + 1) * HD, :]
            outs = []
            for u in range(2):
                sink = sinks_ref[jp * 8 + t * 2 + u]
                feat = qt[u * HD:(u + 1) * HD, :].astype(bf16)
                wq = jnp.concatenate([feat, zeros_half] if g == 0 else [zeros_half, feat], axis=0)
                st = _dot(kwin, wq)
                st = jnp.where(valid, st, NEG_INF)
                m = jnp.maximum(jnp.max(st, axis=0, keepdims=True), sink)
                pt = jnp.exp(st - m)
                denom = jnp.sum(pt, axis=0, keepdims=True) + jnp.exp(sink - m)
                ptb = pt.astype(bf16)
                ot = _dot(v_prev, ptb[:WINDOW, :]) + _dot(v_cur, ptb[WINDOW:, :])
                outs.append(ot / denom)
            o_ref[pl.ds(r0, WINDOW), t * LANES:(t + 1) * LANES] = (
                jnp.concatenate(outs, axis=0).T.astype(bf16))
        return 0

    lax.fori_loop(0, nblk, body, 0)


def _swa(sinks, proj, batch, seq):
    n = proj.shape[0]
    qb0 = P_QA // SWA_QCOLS
    kb0 = P_KA // LANES
    vb0 = P_VA // LANES
    return pl.pallas_call(
        _swa_kernel,
        grid=(batch, SWA_QW // SWA_QCOLS),
        in_specs=[
            pl.BlockSpec(memory_space=pltpu.SMEM),
            pl.BlockSpec((seq, SWA_QCOLS), lambda b, j: (b, qb0 + j)),
            pl.BlockSpec((seq, LANES), lambda b, j: (b, kb0 + j)),
            pl.BlockSpec((seq, LANES), lambda b, j: (b, vb0 + j)),
        ],
        out_specs=pl.BlockSpec((seq, SWA_QCOLS), lambda b, j: (b, j)),
        out_shape=jax.ShapeDtypeStruct((n, SWA_QW), bf16),
        scratch_shapes=[pltpu.VMEM((seq // WINDOW, LANES, WINDOW), bf16)],
        compiler_params=_cparams(("arbitrary", "arbitrary")),
        name="swa_attention",
    )(sinks, proj, proj, proj)


FOX_T = 256
LOG2E = 1.4426950408889634


def _fox_kernel(q_ref, k_ref, v_ref, cum_ref, o_ref, vt_ref, kb_ref):
    p = pl.program_id(1)
    seq = q_ref.shape[0]
    nblk = seq // FOX_T

    for c in range(nblk):
        vblk = v_ref[c * FOX_T:(c + 1) * FOX_T, :].astype(f32)
        vt_ref[c] = vblk.T.astype(bf16)

    cum = cum_ref[...]
    hl = lax.broadcasted_iota(i32, cum.shape, 1)
    for u in range(2):
        col = jnp.sum(jnp.where(hl == 2 * p + u, cum, 0.0), axis=1, keepdims=True)
        kb_ref[u] = jnp.broadcast_to(col * LOG2E, (seq, LANES))

    frow = lax.broadcasted_iota(i32, (LANES, FOX_T), 0)
    kidx = lax.broadcasted_iota(i32, (FOX_T, FOX_T), 0)
    qidx = lax.broadcasted_iota(i32, (FOX_T, FOX_T), 1)
    causal = kidx <= qidx

    def scores(u, wq, k0):
        kblk = k_ref[pl.ds(k0, FOX_T), :]
        kb = kb_ref[u, pl.ds(k0, FOX_T), :]
        return _dot(kblk, wq) - jnp.concatenate([kb] * (FOX_T // LANES), axis=1)

    def update(u, st, jk, state):
        m, l, acc = state
        m_new = jnp.maximum(m, jnp.max(st, axis=0, keepdims=True))
        alpha = jnp.exp2(m - m_new)
        pt = jnp.exp2(st - m_new)
        l_new = alpha * l + jnp.sum(pt, axis=0, keepdims=True)
        vt = vt_ref[jk, u * HD:(u + 1) * HD, :]
        acc_new = alpha * acc + _dot(vt, pt.astype(bf16))
        return m_new, l_new, acc_new

    def qblock(iq, _):
        q0 = pl.multiple_of(iq * FOX_T, FOX_T)
        qt = q_ref[pl.ds(q0, FOX_T), :].astype(f32).T
        qt = qt * LOG2E
        wqs = [jnp.where((frow // HD) == u, qt, 0.0).astype(bf16) for u in range(2)]
        init = tuple((jnp.full((1, FOX_T), NEG_INF, f32), jnp.zeros((1, FOX_T), f32),
                      jnp.zeros((HD, FOX_T), f32)) for _ in range(2))

        def kblock(jk, carry):
            sts, states = carry
            k1 = pl.multiple_of((jk + 1) * FOX_T, FOX_T)
            nxt = tuple(scores(u, wqs[u], k1) for u in range(2))
            return nxt, tuple(update(u, sts[u], jk, states[u]) for u in range(2))

        first = tuple(scores(u, wqs[u], 0) for u in range(2))
        sts, states = lax.fori_loop(0, iq, kblock, (first, init))
        outs = []
        for u in range(2):
            _, l, acc = update(u, jnp.where(causal, sts[u], NEG_INF), iq, states[u])
            outs.append(acc / l)
        o_ref[pl.ds(q0, FOX_T), :] = jnp.concatenate(outs, axis=0).T.astype(bf16)
        return 0

    lax.fori_loop(0, nblk, qblock, 0)


def _fox(proj, cum, batch, seq):
    n = proj.shape[0]
    qb0 = P_QB // LANES
    kb0 = P_KB // LANES
    vb0 = P_VB // LANES
    return pl.pallas_call(
        _fox_kernel,
        grid=(batch, FOX_W // LANES),
        in_specs=[
            pl.BlockSpec((seq, LANES), lambda b, p: (b, qb0 + p)),
            pl.BlockSpec((seq, LANES), lambda b, p: (b, kb0 + p)),
            pl.BlockSpec((seq, LANES), lambda b, p: (b, vb0 + p)),
            pl.BlockSpec((None, seq, N_FGATE), lambda b, p: (b, 0, 0)),
        ],
        out_specs=pl.BlockSpec((seq, LANES), lambda b, p: (b, p)),
        out_shape=jax.ShapeDtypeStruct((n, FOX_W), bf16),
        scratch_shapes=[pltpu.VMEM((seq // FOX_T, LANES, FOX_T), bf16),
                        pltpu.VMEM((2, seq, LANES), f32)],
        compiler_params=_cparams(("arbitrary", "arbitrary")),
        name="fox_attention",
    )(proj, proj, proj, cum)


MG_TM = 256
MG_TN = 512
ROUTE_W = LANES
R_IDX, R_GATE, R_RANK = 0, TOP_K, 2 * TOP_K


def _merge_kernel(ya_ref, yb_ref, sa_ref, sb_ref, x_ref, mod_ref, wa_ref, wb_ref, wo_ref, gf_ref,
                  wrh_ref, wrl_ref, br_ref, h1_ref, xn2_ref, route_ref, merged_ref, cnt_ref):
    i = pl.program_id(0)

    @pl.when(i == 0)
    def _():
        cnt_ref[...] = jnp.zeros_like(cnt_ref)

    ya = ya_ref[...]
    yb = yb_ref[...]
    for c in range(D // MG_TN):
        sl = slice(c * MG_TN, (c + 1) * MG_TN)
        ta = _dot(ya, wa_ref[:, sl])
        tb = _dot(yb, wb_ref[:, sl])
        merged_ref[:, sl] = (sa_ref[:, sl].astype(f32) * ta + sb_ref[:, sl].astype(f32) * tb).astype(bf16)

    merged = merged_ref[...]
    ssq = jnp.zeros((MG_TM, 1), f32)
    for c in range(D // MG_TN):
        sl = slice(c * MG_TN, (c + 1) * MG_TN)
        o = _dot(merged, wo_ref[:, sl])
        h = x_ref[:, sl] + mod_ref[:, 2 * D + c * MG_TN:2 * D + (c + 1) * MG_TN] * o
        h1_ref[:, sl] = h
        ssq = ssq + jnp.sum(h * h, axis=1, keepdims=True)

    inv = lax.rsqrt(ssq * (1.0 / D) + RMS_EPS)
    xn2 = h1_ref[...] * inv * gf_ref[...] * (1.0 + mod_ref[:, 4 * D:5 * D]) + mod_ref[:, 3 * D:4 * D]
    xn2_ref[...] = xn2.reshape(MG_TM, D // LANES, LANES)
    hi = xn2.astype(bf16)
    lo = (xn2 - hi.astype(f32)).astype(bf16)
    wrh = wrh_ref[...]
    logits = _dot(hi, wrh) + _dot(lo, wrh) + _dot(hi, wrl_ref[...]) + br_ref[...]

    elane = lax.broadcasted_iota(i32, (MG_TM, N_EXPERTS), 1)
    work = logits
    onehots, vals, idxs = [], [], []
    for _ in range(TOP_K):
        mk = jnp.max(work, axis=1, keepdims=True)
        idx = jnp.min(jnp.where(work == mk, elane, N_EXPERTS), axis=1, keepdims=True)
        one = elane == idx
        onehots.append(one)
        vals.append(mk)
        idxs.append(idx)
        work = jnp.where(one, -jnp.inf, work)
    exps = [jnp.exp(v - vals[0]) for v in vals]
    denom = exps[0] + exps[1] + exps[2] + exps[3]

    sel = jnp.where(onehots[0] | onehots[1] | onehots[2] | onehots[3], 1.0, 0.0)
    trow = lax.broadcasted_iota(i32, (MG_TM, MG_TM), 0)
    tcol = lax.broadcasted_iota(i32, (MG_TM, MG_TM), 1)
    strict = jnp.where(tcol < trow, 1.0, 0.0).astype(bf16)
    rank = _dot(strict, sel.astype(bf16)) + cnt_ref[...]
    cnt_ref[...] = cnt_ref[...] + jnp.sum(sel, axis=0, keepdims=True)

    rlane = lax.broadcasted_iota(i32, (MG_TM, ROUTE_W), 1)
    rec = jnp.zeros((MG_TM, ROUTE_W), f32)
    for k in range(TOP_K):
        rk = jnp.sum(jnp.where(onehots[k], rank, 0.0), axis=1, keepdims=True)
        rec = jnp.where(rlane == R_IDX + k, idxs[k].astype(f32), rec)
        rec = jnp.where(rlane == R_GATE + k, exps[k] / denom, rec)
        rec = jnp.where(rlane == R_RANK + k, rk, rec)
    route_ref[...] = rec


def _merge(ya, yb, proj, x2, mod3, wa, wb, wo, g_ffn, wr_hi, wr_lo, b_r, seq):
    n = x2.shape[0]
    tiles_per_batch = seq // MG_TM
    const = lambda i: (0, 0)
    return pl.pallas_call(
        _merge_kernel,
        grid=(n // MG_TM,),
        in_specs=[
            pl.BlockSpec((MG_TM, SWA_QW), lambda i: (i, 0)),
            pl.BlockSpec((MG_TM, FOX_W), lambda i: (i, 0)),
            pl.BlockSpec((MG_TM, D), lambda i: (i, P_SA // D)),
            pl.BlockSpec((MG_TM, D), lambda i: (i, P_SB // D)),
            pl.BlockSpec((MG_TM, D), lambda i: (i, 0)),
            pl.BlockSpec((None, 1, 6 * D), lambda i: (i // tiles_per_batch, 0, 0)),
            pl.BlockSpec((SWA_QW, D), const),
            pl.BlockSpec((FOX_W, D), const),
            pl.BlockSpec((D, D), const),
            pl.BlockSpec((1, D), const),
            pl.BlockSpec((D, N_EXPERTS), const),
            pl.BlockSpec((D, N_EXPERTS), const),
            pl.BlockSpec((1, N_EXPERTS), const),
        ],
        out_specs=[
            pl.BlockSpec((MG_TM, D), lambda i: (i, 0)),
            pl.BlockSpec((MG_TM, D // LANES, LANES), lambda i: (i, 0, 0)),
            pl.BlockSpec((MG_TM, ROUTE_W), lambda i: (i, 0)),
        ],
        out_shape=[
            jax.ShapeDtypeStruct((n, D), f32),
            jax.ShapeDtypeStruct((n, D // LANES, LANES), f32),
            jax.ShapeDtypeStruct((n, ROUTE_W), f32),
        ],
        scratch_shapes=[pltpu.VMEM((MG_TM, D), bf16), pltpu.VMEM((1, N_EXPERTS), f32)],
        compiler_params=_cparams(("arbitrary",)),
        name="merge_router",
    )(ya, yb, proj, proj, x2, mod3, wa, wb, wo, g_ffn, wr_hi, wr_lo, b_r)


MOE_TM = 256
GATHER_CH = 256


SLAB = D // LANES


def _gather_kernel(tok_ref, tok_next_ref, x_ref, o_ref, stage, sems):
    i = pl.program_id(0)
    n = pl.num_programs(0)
    slot = i % 2

    def copies(idx_ref, sl):
        return [pltpu.make_async_copy(x_ref.at[idx_ref[0, 0, r]], stage.at[sl, r], sems.at[sl])
                for r in range(GATHER_CH)]

    def start(idx_ref, sl):
        for c in copies(idx_ref, sl):
            c.start()

    @pl.when(i == 0)
    def _():
        start(tok_ref, 0)

    for sl in range(2):
        @pl.when((i + 1 < n) & (slot == 1 - sl))
        def _():
            start(tok_next_ref, sl)

    for sl in range(2):
        @pl.when(slot == sl)
        def _():
            for c in copies(tok_ref, sl):
                c.wait()
            o_ref[...] = stage[sl].reshape(GATHER_CH, D).astype(bf16)


def _gather_rows(row_tok3, xn2_slab, p_alloc):
    n_chunks = p_alloc // GATHER_CH
    return pl.pallas_call(
        _gather_kernel,
        grid=(n_chunks,),
        in_specs=[
            pl.BlockSpec((1, 1, GATHER_CH), lambda i: (i, 0, 0), memory_space=pltpu.SMEM),
            pl.BlockSpec((1, 1, GATHER_CH), lambda i: (jnp.minimum(i + 1, n_chunks - 1), 0, 0),
                         memory_space=pltpu.SMEM),
            pl.BlockSpec(memory_space=pl.ANY),
        ],
        out_specs=pl.BlockSpec((GATHER_CH, D), lambda i: (i, 0)),
        out_shape=jax.ShapeDtypeStruct((p_alloc, D), bf16),
        scratch_shapes=[pltpu.VMEM((2, GATHER_CH, SLAB, LANES), f32), pltpu.SemaphoreType.DMA((2,))],
        compiler_params=_cparams(("arbitrary",)),
        name="moe_gather",
    )(row_tok3, row_tok3, xn2_slab)


MOE_RB = 8
MOE_TF = 512
MOE_NF = D_EXPERT // MOE_TF
MOE_VMEM = 60 * 1024 * 1024


def _moe_kernel(ge_ref, gs_ref, gn_ref, gz_ref, x_hbm, wg_ref, wl_ref, bg_ref, bl_ref, wd_ref, bd_ref, y_hbm,
                xbuf, yacc, yst, sem_in, sem_out):
    s = pl.program_id(0)
    f = pl.program_id(1)
    nb = gn_ref[s]
    nz = gz_ref[s]
    blk0 = gs_ref[s]

    def x_copy(r):
        return pltpu.make_async_copy(x_hbm.at[pl.ds((blk0 + r) * MOE_TM, MOE_TM)], xbuf.at[r], sem_in)

    def y_copy(r, slot):
        return pltpu.make_async_copy(yst.at[slot], y_hbm.at[pl.ds((blk0 + r) * MOE_TM, MOE_TM)],
                                     sem_out.at[slot])

    @pl.when((f == 0) & (nz > 0))
    def _():
        yst[0] = jnp.zeros((MOE_TM, SLAB, LANES), f32)
        for r in range(MOE_RB):
            @pl.when(r < nz)
            def _():
                y_copy(r, 0).start()
        for r in range(MOE_RB):
            @pl.when(r < nz)
            def _():
                y_copy(r, 0).wait()

    @pl.when(f == 0)
    def _():
        for r in range(MOE_RB):
            @pl.when(r < nb)
            def _():
                x_copy(r).start()

        def init(r, _):
            yacc[r] = jnp.broadcast_to(bd_ref[...], (MOE_TM, D))
            return 0

        lax.fori_loop(0, nb, init, 0)
        for r in range(MOE_RB):
            @pl.when(r < nb)
            def _():
                x_copy(r).wait()

    def up(r):
        x = xbuf[r]
        glu = _dot(x, wg_ref[...]) + bg_ref[...]
        lin = _dot(x, wl_ref[...]) + bl_ref[...]
        glu = jnp.minimum(glu, SWIGLU_LIMIT)
        lin = jnp.clip(lin, -SWIGLU_LIMIT, SWIGLU_LIMIT)
        return (glu * jax.nn.sigmoid(SWIGLU_ALPHA * glu) * (lin + 1.0)).astype(bf16)

    def down(r, act):
        yacc[r] = yacc[r] + _dot(act, wd_ref[...])

    @pl.when(nb > 0)
    def _():
        def body(r, act_prev):
            down(r - 1, act_prev)
            return up(r)

        act_last = lax.fori_loop(1, nb, body, up(0))
        down(nb - 1, act_last)

    @pl.when((f == MOE_NF - 1) & (nb > 0))
    def _():
        def emit(r, _):
            slot = r % 2

            @pl.when(r >= 2)
            def _():
                y_copy(r - 2, slot).wait()

            yst[slot] = yacc[r].reshape(MOE_TM, SLAB, LANES)
            y_copy(r, slot).start()
            return 0

        lax.fori_loop(0, nb, emit, 0)

        @pl.when(nb >= 2)
        def _():
            y_copy(nb - 2, nb % 2).wait()

        y_copy(nb - 1, (nb - 1) % 2).wait()


def _moe(g_expert, g_start, g_nblk, g_nzero, x_rows, w_gu, b_gu3, w_down, b_down3, n_groups):
    p_alloc = x_rows.shape[0]

    def f_eff(s, f, gn):
        return jnp.where(gn[s] > 0, f, MOE_NF - 1)

    grid_spec = pltpu.PrefetchScalarGridSpec(
        num_scalar_prefetch=4,
        grid=(n_groups, MOE_NF),
        in_specs=[
            pl.BlockSpec(memory_space=pl.ANY),
            pl.BlockSpec((None, D, MOE_TF), lambda s, f, ge, gs, gn, gz: (ge[s], 0, f_eff(s, f, gn))),
            pl.BlockSpec((None, D, MOE_TF), lambda s, f, ge, gs, gn, gz: (ge[s], 0, MOE_NF + f_eff(s, f, gn))),
            pl.BlockSpec((None, 1, MOE_TF), lambda s, f, ge, gs, gn, gz: (ge[s], 0, f_eff(s, f, gn))),
            pl.BlockSpec((None, 1, MOE_TF), lambda s, f, ge, gs, gn, gz: (ge[s], 0, MOE_NF + f_eff(s, f, gn))),
            pl.BlockSpec((None, MOE_TF, D), lambda s, f, ge, gs, gn, gz: (ge[s], f_eff(s, f, gn), 0)),
            pl.BlockSpec((None, 1, D), lambda s, f, ge, gs, gn, gz: (ge[s], 0, 0)),
        ],
        out_specs=pl.BlockSpec(memory_space=pl.ANY),
        scratch_shapes=[
            pltpu.VMEM((MOE_RB, MOE_TM, D), bf16),
            pltpu.VMEM((MOE_RB, MOE_TM, D), f32),
            pltpu.VMEM((2, MOE_TM, SLAB, LANES), f32),
            pltpu.SemaphoreType.DMA(()),
            pltpu.SemaphoreType.DMA((2,)),
        ],
    )
    return pl.pallas_call(
        _moe_kernel,
        grid_spec=grid_spec,
        out_shape=jax.ShapeDtypeStruct((p_alloc, SLAB, LANES), f32),
        compiler_params=_cparams(("arbitrary", "arbitrary"), MOE_VMEM),
        name="moe_experts",
    )(g_expert, g_start, g_nblk, g_nzero, x_rows, w_gu, w_gu, b_gu3, b_gu3, w_down, b_down3)


CB_TM = 64


def _combine_kernel(dest_ref, dest_next_ref, y_hbm, h1_ref, route_ref, mod_ref, g_ref, o_ref, ybuf, sems):
    i = pl.program_id(0)
    n = pl.num_programs(0)
    slot = i % 2

    def copies(idx_ref, sl):
        return [pltpu.make_async_copy(y_hbm.at[idx_ref[0, 0, t * TOP_K + k]], ybuf.at[sl, k, t], sems.at[sl])
                for t in range(CB_TM) for k in range(TOP_K)]

    def start(idx_ref, sl):
        for c in copies(idx_ref, sl):
            c.start()

    @pl.when(i == 0)
    def _():
        start(dest_ref, 0)

    for sl in range(2):
        @pl.when((i + 1 < n) & (slot == 1 - sl))
        def _():
            start(dest_next_ref, sl)

    route = route_ref[...]
    gt2 = mod_ref[:, 5 * D:6 * D]
    for sl in range(2):
        @pl.when(slot == sl)
        def _():
            for c in copies(dest_ref, sl):
                c.wait()
            moe = jnp.zeros((CB_TM, D), f32)
            for k in range(TOP_K):
                moe = moe + route[:, R_GATE + k:R_GATE + k + 1] * ybuf[sl, k].reshape(CB_TM, D)
            h = h1_ref[...] + gt2 * moe
            ms = jnp.mean(h * h, axis=-1, keepdims=True)
            o_ref[...] = h * lax.rsqrt(ms + RMS_EPS) * g_ref[...]


def _combine(dest3, y_rows, h1, route, mod3, g_final, seq):
    n = h1.shape[0]
    tiles_per_batch = seq // CB_TM
    n_tiles = n // CB_TM
    return pl.pallas_call(
        _combine_kernel,
        grid=(n_tiles,),
        in_specs=[
            pl.BlockSpec((1, 1, CB_TM * TOP_K), lambda i: (i, 0, 0), memory_space=pltpu.SMEM),
            pl.BlockSpec((1, 1, CB_TM * TOP_K), lambda i: (jnp.minimum(i + 1, n_tiles - 1), 0, 0),
                         memory_space=pltpu.SMEM),
            pl.BlockSpec(memory_space=pl.ANY),
            pl.BlockSpec((CB_TM, D), lambda i: (i, 0)),
            pl.BlockSpec((CB_TM, ROUTE_W), lambda i: (i, 0)),
            pl.BlockSpec((None, 1, 6 * D), lambda i: (i // tiles_per_batch, 0, 0)),
            pl.BlockSpec((1, D), lambda i: (0, 0)),
        ],
        out_specs=pl.BlockSpec((CB_TM, D), lambda i: (i, 0)),
        out_shape=jax.ShapeDtypeStruct((n, D), f32),
        scratch_shapes=[pltpu.VMEM((2, TOP_K, CB_TM, SLAB, LANES), f32), pltpu.SemaphoreType.DMA((2,))],
        compiler_params=_cparams(("arbitrary",)),
        name="moe_combine",
    )(dest3, dest3, y_rows, h1, route, mod3, g_final)


def _layer(h, c_act_pad, positions, w_ada, b_ada, g_mix, w_in, b_in, sinks, w_branch_a, w_branch_b, w_out,
           g_ffn, w_router, b_router, w_gu, b_gu, w_down, b_down):
    batch, seq, _ = h.shape
    n = batch * seq
    x2 = h.reshape(n, D)

    mod = _adaln(c_act_pad, w_ada, b_ada.reshape(1, 6 * D))
    mod3 = mod[:batch].reshape(batch, 1, 6 * D)

    w_cat = jnp.concatenate([w_in[:, GATE_OFF:], w_in[:, :QKV_W]], axis=1).astype(bf16)
    b_cat = jnp.concatenate([b_in[GATE_OFF:], b_in[:QKV_W]]).reshape(1, PROJ_W)
    w_f = w_in[:, FGATE_OFF:GATE_OFF].astype(bf16)
    b_f = b_in[FGATE_OFF:GATE_OFF].reshape(1, N_FGATE)

    inv_freq = 1.0 / (ROPE_THETA ** (jnp.arange(0, HD, 2, dtype=f32) / HD))
    ang = positions.astype(f32).reshape(n, 1) * inv_freq[None, :]
    cos_t = jnp.tile(jnp.cos(ang), (1, LANES // (HD // 2)))
    sin_h = jnp.sin(ang)
    sin_t = jnp.tile(jnp.concatenate([-sin_h, sin_h], axis=1), (1, LANES // HD))

    proj, logf = _inproj(x2, mod3, g_mix.reshape(1, D), w_cat, b_cat, w_f, b_f, cos_t, sin_t, seq)
    cum = _cumsum(logf.reshape(batch, seq, N_FGATE))

    ya = _swa(sinks, proj, batch, seq)
    yb = _fox(proj, cum, batch, seq)

    wr_hi = w_router.astype(bf16)
    wr_lo = (w_router - wr_hi.astype(f32)).astype(bf16)
    h1, xn2, route = _merge(ya, yb, proj, x2, mod3, w_branch_a.astype(bf16), w_branch_b.astype(bf16),
                            w_out.astype(bf16), g_ffn.reshape(1, D), wr_hi, wr_lo,
                            b_router.reshape(1, N_EXPERTS), seq)

    a_total = n * TOP_K
    top_idx = route[:, R_IDX:R_IDX + TOP_K].astype(i32)
    rank = route[:, R_RANK:R_RANK + TOP_K].astype(i32)
    counts = jnp.sum((top_idx[..., None] == jnp.arange(N_EXPERTS, dtype=i32)).astype(i32), axis=(0, 1))
    nblk_e = (counts + MOE_TM - 1) // MOE_TM
    blk_end = jnp.cumsum(nblk_e)
    blk_start = blk_end - nblk_e
    dest = (blk_start[top_idx] * MOE_TM + rank).reshape(a_total)
    p_alloc = a_total + N_EXPERTS * MOE_TM
    tok = jnp.arange(a_total, dtype=i32) // TOP_K
    row_tok = jnp.zeros((p_alloc,), i32).at[dest].set(tok, unique_indices=True, mode="promise_in_bounds")

    n_groups = (a_total // MOE_TM + N_EXPERTS + N_EXPERTS * (MOE_RB - 1)) // MOE_RB
    ng_e = (nblk_e + MOE_RB - 1) // MOE_RB
    g_end = jnp.cumsum(ng_e)
    gid = jnp.arange(n_groups, dtype=i32)
    g_exp_raw = jnp.searchsorted(g_end, gid, side="right").astype(i32)
    active = gid < g_end[-1]
    last_e = jnp.max(jnp.where(counts > 0, jnp.arange(N_EXPERTS, dtype=i32), 0))
    g_expert = jnp.where(active, jnp.minimum(g_exp_raw, N_EXPERTS - 1), last_e).astype(i32)
    g_local = gid - (g_end - ng_e)[g_expert]
    g_nblk = jnp.where(active, jnp.minimum(nblk_e[g_expert] - g_local * MOE_RB, MOE_RB), 0).astype(i32)
    tail0 = blk_end[-1] + (gid - g_end[-1]) * MOE_RB
    g_nzero = jnp.where(active, 0, jnp.clip(p_alloc // MOE_TM - tail0, 0, MOE_RB)).astype(i32)
    g_start = jnp.where(active, blk_start[g_expert] + g_local * MOE_RB,
                        jnp.minimum(tail0, p_alloc // MOE_TM)).astype(i32)

    x_rows = _gather_rows(row_tok.reshape(p_alloc // GATHER_CH, 1, GATHER_CH), xn2, p_alloc)
    y_rows = _moe(g_expert, g_start, g_nblk, g_nzero, x_rows, w_gu, b_gu.reshape(N_EXPERTS, 1, 2 * D_EXPERT),
                  w_down, b_down.reshape(N_EXPERTS, 1, D), n_groups)
    return dest, y_rows, h1, route, mod3


def kernel(x, c, positions, w_ada, b_ada, g_mix, w_in, b_in, sinks, w_branch_a, w_branch_b, w_out, g_ffn,
           w_router, b_router, w_gu, b_gu, w_down, b_down, g_final):
    batch, seq, _ = x.shape
    depth = w_ada.shape[0]
    assert depth == 1, "single-layer trunk"
    c_pad = jnp.zeros((8, D), f32).at[:batch].set(c)
    l = 0
    dest, y_rows, h1, route, mod3 = _layer(
        x, c_pad, positions, w_ada[l], b_ada[l], g_mix[l], w_in[l], b_in[l], sinks[l], w_branch_a[l],
        w_branch_b[l], w_out[l], g_ffn[l], w_router[l], b_router[l], w_gu[l], b_gu[l], w_down[l], b_down[l])
    n = batch * seq
    out = _combine(dest.reshape(n // CB_TM, 1, CB_TM * TOP_K), y_rows, h1, route, mod3,
                   g_final.reshape(1, D), seq)
    return out.reshape(batch, seq, D)
```

```python
import functools

import jax
import jax.numpy as jnp
from jax import lax
from jax.experimental import pallas as pl
from jax.experimental.pallas import tpu as pltpu

f32 = jnp.float32
bf16 = jnp.bfloat16
i32 = jnp.int32

D = 2048
HD = 64
SWA_QW = 1024
SWA_KVW = 256
FOX_W = 1024
N_FGATE = 16
WINDOW = 128
N_EXPERTS = 32
TOP_K = 4
D_EXPERT = 2048
SWIGLU_LIMIT = 7.0
SWIGLU_ALPHA = 1.702
RMS_EPS = 1e-5
NEG_INF = -1e30
ROPE_THETA = 10000.0

QKV_W = SWA_QW + 2 * SWA_KVW + 3 * FOX_W
FGATE_OFF = QKV_W
GATE_OFF = QKV_W + N_FGATE

LANES = 128
P_SA = 0
P_SB = D
P_QA = 2 * D
P_KA = P_QA + SWA_QW
P_VA = P_KA + SWA_KVW
P_QB = P_VA + SWA_KVW
P_KB = P_QB + FOX_W
P_VB = P_KB + FOX_W
PROJ_W = P_VB + FOX_W

VMEM_LIMIT = 56 * 1024 * 1024


def _cparams(sem, vmem=VMEM_LIMIT):
    return pltpu.CompilerParams(dimension_semantics=sem, vmem_limit_bytes=vmem)


def _dot(a, b):
    return jnp.dot(a, b, preferred_element_type=f32)


ADA_TN = 1024


def _adaln_kernel(c_ref, w_ref, b_ref, o_ref):
    c = c_ref[...]
    ca = c * jax.nn.sigmoid(c)
    o_ref[...] = _dot(ca, w_ref[...]) + b_ref[...]


def _adaln(c_pad, w_ada, b_ada):
    rows = c_pad.shape[0]
    n = w_ada.shape[1]
    return pl.pallas_call(
        _adaln_kernel,
        grid=(n // ADA_TN,),
        in_specs=[
            pl.BlockSpec((rows, D), lambda j: (0, 0)),
            pl.BlockSpec((D, ADA_TN), lambda j: (0, j)),
            pl.BlockSpec((1, ADA_TN), lambda j: (0, j)),
        ],
        out_specs=pl.BlockSpec((rows, ADA_TN), lambda j: (0, j)),
        out_shape=jax.ShapeDtypeStruct((rows, n), f32),
        compiler_params=_cparams(("arbitrary",)),
        name="adaln",
    )(c_pad, w_ada, b_ada)


IN_TM = 1024
IN_TN = 512
IN_NJ = PROJ_W // IN_TN
J_QA0 = P_QA // IN_TN
J_KVA = P_KA // IN_TN
J_QB0 = P_QB // IN_TN
J_KB0 = P_KB // IN_TN


def _rope_block(a, cos, sin_signed, first_half):
    fwd = pltpu.roll(a, LANES - HD // 2, axis=1)
    bwd = pltpu.roll(a, HD // 2, axis=1)
    partner = jnp.where(first_half, fwd, bwd)
    return a * cos + partner * sin_signed


def _inproj_kernel(x_ref, mod_ref, g_ref, w_ref, b_ref, wf_ref, bf_ref, cos_ref, sin_ref,
                   proj_ref, logf_ref, xn_ref):
    j = pl.program_id(1)

    @pl.when(j == 0)
    def _():
        x = x_ref[...]
        ms = jnp.mean(x * x, axis=-1, keepdims=True)
        y = x * lax.rsqrt(ms + RMS_EPS) * g_ref[...]
        xn = y * (1.0 + mod_ref[:, D:2 * D]) + mod_ref[:, 0:D]
        xnb = xn.astype(bf16)
        xn_ref[...] = xnb
        fl = _dot(xnb, wf_ref[...]) + bf_ref[...]
        logf_ref[...] = jnp.minimum(fl, 0.0) - jnp.log(1.0 + jnp.exp(-jnp.abs(fl)))

    acc = _dot(xn_ref[...], w_ref[...]) + b_ref[...]

    lane = lax.broadcasted_iota(i32, (IN_TM, LANES), 1)
    first_half = (lane % HD) < (HD // 2)
    scale = HD ** -0.5

    def rope_cols(n_blocks, mult):
        cos = cos_ref[...]
        sin = sin_ref[...]
        for g in range(n_blocks):
            sl = slice(g * LANES, (g + 1) * LANES)
            r = _rope_block(acc[:, sl], cos, sin, first_half)
            proj_ref[:, sl] = (r * mult).astype(bf16)

    @pl.when(j < J_QA0)
    def _():
        proj_ref[...] = jax.nn.sigmoid(acc).astype(bf16)

    @pl.when((j >= J_QA0) & (j < J_KVA))
    def _():
        rope_cols(IN_TN // LANES, scale)

    @pl.when(j == J_KVA)
    def _():
        rope_cols(SWA_KVW // LANES, 1.0)
        proj_ref[:, SWA_KVW:] = acc[:, SWA_KVW:].astype(bf16)

    @pl.when((j >= J_QB0) & (j < J_KB0))
    def _():
        proj_ref[...] = (acc * scale).astype(bf16)

    @pl.when(j >= J_KB0)
    def _():
        proj_ref[...] = acc.astype(bf16)


def _inproj(x2, mod3, g_mix, w_cat, b_cat, w_f, b_f, cos_t, sin_t, seq):
    n = x2.shape[0]
    tiles_per_batch = seq // IN_TM
    return pl.pallas_call(
        _inproj_kernel,
        grid=(n // IN_TM, IN_NJ),
        in_specs=[
            pl.BlockSpec((IN_TM, D), lambda i, j: (i, 0)),
            pl.BlockSpec((None, 1, 6 * D), lambda i, j: (i // tiles_per_batch, 0, 0)),
            pl.BlockSpec((1, D), lambda i, j: (0, 0)),
            pl.BlockSpec((D, IN_TN), lambda i, j: (0, j)),
            pl.BlockSpec((1, IN_TN), lambda i, j: (0, j)),
            pl.BlockSpec((D, N_FGATE), lambda i, j: (0, 0)),
            pl.BlockSpec((1, N_FGATE), lambda i, j: (0, 0)),
            pl.BlockSpec((IN_TM, LANES), lambda i, j: (i, 0)),
            pl.BlockSpec((IN_TM, LANES), lambda i, j: (i, 0)),
        ],
        out_specs=[
            pl.BlockSpec((IN_TM, IN_TN), lambda i, j: (i, j)),
            pl.BlockSpec((IN_TM, N_FGATE), lambda i, j: (i, 0)),
        ],
        out_shape=[
            jax.ShapeDtypeStruct((n, PROJ_W), bf16),
            jax.ShapeDtypeStruct((n, N_FGATE), f32),
        ],
        scratch_shapes=[pltpu.VMEM((IN_TM, D), bf16)],
        compiler_params=_cparams(("arbitrary", "arbitrary")),
        name="inproj",
    )(x2, mod3, g_mix, w_cat, b_cat, w_f, b_f, cos_t, sin_t)


CUM_CH = 256


def _split3(x):
    x1 = x.astype(bf16)
    r1 = x - x1.astype(f32)
    x2 = r1.astype(bf16)
    r2 = r1 - x2.astype(f32)
    return x1, x2, r2.astype(bf16)


def _cumsum_kernel(x_ref, o_ref):
    seq = x_ref.shape[0]
    row = lax.broadcasted_iota(i32, (CUM_CH, CUM_CH), 0)
    col = lax.broadcasted_iota(i32, (CUM_CH, CUM_CH), 1)
    tri = jnp.where(col <= row, 1.0, 0.0).astype(bf16)
    carry = jnp.zeros((1, N_FGATE), f32)
    for c in range(seq // CUM_CH):
        sl = slice(c * CUM_CH, (c + 1) * CUM_CH)
        x1, x2, x3 = _split3(x_ref[sl, :])
        cs = _dot(tri, x1) + _dot(tri, x2) + _dot(tri, x3) + carry
        o_ref[sl, :] = cs
        carry = cs[CUM_CH - 1:CUM_CH, :]


def _cumsum(logf3):
    b, seq, h = logf3.shape
    return pl.pallas_call(
        _cumsum_kernel,
        grid=(b,),
        in_specs=[pl.BlockSpec((None, seq, h), lambda i: (i, 0, 0))],
        out_specs=pl.BlockSpec((None, seq, h), lambda i: (i, 0, 0)),
        out_shape=jax.ShapeDtypeStruct((b, seq, h), f32),
        compiler_params=_cparams(("arbitrary",)),
        name="fgate_cumsum",
    )(logf3)


SWA_QCOLS = 512


def _swa_kernel(sinks_ref, q_ref, k_ref, v_ref, o_ref, vt_ref):
    jp = pl.program_id(1)
    seq = q_ref.shape[0]
    nblk = seq // WINDOW

    for c in range(nblk):
        vblk = v_ref[c * WINDOW:(c + 1) * WINDOW, :].astype(f32)
        vt_ref[c] = vblk.T.astype(bf16)

    kj = lax.broadcasted_iota(i32, (2 * WINDOW, WINDOW), 0)
    qi = lax.broadcasted_iota(i32, (2 * WINDOW, WINDOW), 1)
    delta = qi + WINDOW - kj
    band = (delta >= 0) & (delta < WINDOW)
    zeros_half = jnp.zeros((HD, WINDOW), bf16)

    def body(n, _):
        prev = jnp.maximum(n - 1, 0)
        r0 = pl.multiple_of(n * WINDOW, WINDOW)
        p0 = pl.multiple_of(prev * WINDOW, WINDOW)
        kwin = jnp.concatenate([k_ref[pl.ds(p0, WINDOW), :], k_ref[pl.ds(r0, WINDOW), :]], axis=0)
        valid = band & ((n - 1) * WINDOW + kj >= 0)
        qblk = q_ref[pl.ds(r0, WINDOW), :]
        for t in range(SWA_QCOLS // LANES):
            g = t // 2
            qt = qblk[:, t * LANES:(t + 1) * LANES].astype(f32).T
            v_prev = vt_ref[prev, g * HD:(g + 1) * HD, :]
            v_cur = vt_ref[n, g * HD:(g + 1) * HD, :]
            outs = []
            for u in range(2):
                sink = sinks_ref[jp * 8 + t * 2 + u]
                feat = qt[u * HD:(u + 1) * HD, :].astype(bf16)
                wq = jnp.concatenate([feat, zeros_half] if g == 0 else [zeros_half, feat], axis=0)
                st = _dot(kwin, wq)
                st = jnp.where(valid, st, NEG_INF)
                m = jnp.maximum(jnp.max(st, axis=0, keepdims=True), sink)
                pt = jnp.exp(st - m)
                denom = jnp.sum(pt, axis=0, keepdims=True) + jnp.exp(sink - m)
                ptb = pt.astype(bf16)
                ot = _dot(v_prev, ptb[:WINDOW, :]) + _dot(v_cur, ptb[WINDOW:, :])
                outs.append(ot / denom)
            o_ref[pl.ds(r0, WINDOW), t * LANES:(t + 1) * LANES] = (
                jnp.concatenate(outs, axis=0).T.astype(bf16))
        return 0

    lax.fori_loop(0, nblk, body, 0)


def _swa(sinks, proj, batch, seq):
    n = proj.shape[0]
    qb0 = P_QA // SWA_QCOLS
    kb0 = P_KA // LANES
    vb0 = P_VA // LANES
    return pl.pallas_call(
        _swa_kernel,
        grid=(batch, SWA_QW // SWA_QCOLS),
        in_specs=[
            pl.BlockSpec(memory_space=pltpu.SMEM),
            pl.BlockSpec((seq, SWA_QCOLS), lambda b, j: (b, qb0 + j)),
            pl.BlockSpec((seq, LANES), lambda b, j: (b, kb0 + j)),
            pl.BlockSpec((seq, LANES), lambda b, j: (b, vb0 + j)),
        ],
        out_specs=pl.BlockSpec((seq, SWA_QCOLS), lambda b, j: (b, j)),
        out_shape=jax.ShapeDtypeStruct((n, SWA_QW), bf16),
        scratch_shapes=[pltpu.VMEM((seq // WINDOW, LANES, WINDOW), bf16)],
        compiler_params=_cparams(("arbitrary", "arbitrary")),
        name="swa_attention",
    )(sinks, proj, proj, proj)


FOX_T = 256
LOG2E = 1.4426950408889634


def _fox_kernel(q_ref, k_ref, v_ref, cum_ref, o_ref, vt_ref, kb_ref):
    p = pl.program_id(1)
    seq = q_ref.shape[0]
    nblk = seq // FOX_T

    for c in range(nblk):
        vblk = v_ref[c * FOX_T:(c + 1) * FOX_T, :].astype(f32)
        vt_ref[c] = vblk.T.astype(bf16)

    cum = cum_ref[...]
    hl = lax.broadcasted_iota(i32, cum.shape, 1)
    for u in range(2):
        col = jnp.sum(jnp.where(hl == 2 * p + u, cum, 0.0), axis=1, keepdims=True)
        kb_ref[u] = jnp.broadcast_to(col * LOG2E, (seq, LANES))

    frow = lax.broadcasted_iota(i32, (LANES, FOX_T), 0)
    kidx = lax.broadcasted_iota(i32, (FOX_T, FOX_T), 0)
    qidx = lax.broadcasted_iota(i32, (FOX_T, FOX_T), 1)
    causal = kidx <= qidx

    def scores(u, wq, k0):
        kblk = k_ref[pl.ds(k0, FOX_T), :]
        kb = kb_ref[u, pl.ds(k0, FOX_T), :]
        return _dot(kblk, wq) - jnp.concatenate([kb] * (FOX_T // LANES), axis=1)

    def update(u, st, jk, state):
        m, l, acc = state
        m_new = jnp.maximum(m, jnp.max(st, axis=0, keepdims=True))
        alpha = jnp.exp2(m - m_new)
        pt = jnp.exp2(st - m_new)
        l_new = alpha * l + jnp.sum(pt, axis=0, keepdims=True)
        vt = vt_ref[jk, u * HD:(u + 1) * HD, :]
        acc_new = alpha * acc + _dot(vt, pt.astype(bf16))
        return m_new, l_new, acc_new

    def qblock(iq, _):
        q0 = pl.multiple_of(iq * FOX_T, FOX_T)
        qt = q_ref[pl.ds(q0, FOX_T), :].astype(f32).T
        qt = qt * LOG2E
        wqs = [jnp.where((frow // HD) == u, qt, 0.0).astype(bf16) for u in range(2)]
        init = tuple((jnp.full((1, FOX_T), NEG_INF, f32), jnp.zeros((1, FOX_T), f32),
                      jnp.zeros((HD, FOX_T), f32)) for _ in range(2))

        def kblock(jk, carry):
            sts, states = carry
            k1 = pl.multiple_of((jk + 1) * FOX_T, FOX_T)
            nxt = tuple(scores(u, wqs[u], k1) for u in range(2))
            return nxt, tuple(update(u, sts[u], jk, states[u]) for u in range(2))

        first = tuple(scores(u, wqs[u], 0) for u in range(2))
        sts, states = lax.fori_loop(0, iq, kblock, (first, init))
        outs = []
        for u in range(2):
            _, l, acc = update(u, jnp.where(causal, sts[u], NEG_INF), iq, states[u])
            outs.append(acc / l)
        o_ref[pl.ds(q0, FOX_T), :] = jnp.concatenate(outs, axis=0).T.astype(bf16)
        return 0

    lax.fori_loop(0, nblk, qblock, 0)


def _fox(proj, cum, batch, seq):
    n = proj.shape[0]
    qb0 = P_QB // LANES
    kb0 = P_KB // LANES
    vb0 = P_VB // LANES
    return pl.pallas_call(
        _fox_kernel,
        grid=(batch, FOX_W // LANES),
        in_specs=[
            pl.BlockSpec((seq, LANES), lambda b, p: (b, qb0 + p)),
            pl.BlockSpec((seq, LANES), lambda b, p: (b, kb0 + p)),
            pl.BlockSpec((seq, LANES), lambda b, p: (b, vb0 + p)),
            pl.BlockSpec((None, seq, N_FGATE), lambda b, p: (b, 0, 0)),
        ],
        out_specs=pl.BlockSpec((seq, LANES), lambda b, p: (b, p)),
        out_shape=jax.ShapeDtypeStruct((n, FOX_W), bf16),
        scratch_shapes=[pltpu.VMEM((seq // FOX_T, LANES, FOX_T), bf16),
                        pltpu.VMEM((2, seq, LANES), f32)],
        compiler_params=_cparams(("arbitrary", "arbitrary")),
        name="fox_attention",
    )(proj, proj, proj, cum)


MG_TM = 256
MG_TN = 512
ROUTE_W = LANES
R_IDX, R_GATE, R_RANK = 0, TOP_K, 2 * TOP_K


def _merge_kernel(ya_ref, yb_ref, sa_ref, sb_ref, x_ref, mod_ref, wa_ref, wb_ref, wo_ref, gf_ref,
                  wrh_ref, wrl_ref, br_ref, h1_ref, xn2_ref, route_ref, merged_ref, cnt_ref):
    i = pl.program_id(0)

    @pl.when(i == 0)
    def _():
        cnt_ref[...] = jnp.zeros_like(cnt_ref)

    ya = ya_ref[...]
    yb = yb_ref[...]
    for c in range(D // MG_TN):
        sl = slice(c * MG_TN, (c + 1) * MG_TN)
        ta = _dot(ya, wa_ref[:, sl])
        tb = _dot(yb, wb_ref[:, sl])
        merged_ref[:, sl] = (sa_ref[:, sl].astype(f32) * ta + sb_ref[:, sl].astype(f32) * tb).astype(bf16)

    merged = merged_ref[...]
    ssq = jnp.zeros((MG_TM, 1), f32)
    for c in range(D // MG_TN):
        sl = slice(c * MG_TN, (c + 1) * MG_TN)
        o = _dot(merged, wo_ref[:, sl])
        h = x_ref[:, sl] + mod_ref[:, 2 * D + c * MG_TN:2 * D + (c + 1) * MG_TN] * o
        h1_ref[:, sl] = h
        ssq = ssq + jnp.sum(h * h, axis=1, keepdims=True)

    inv = lax.rsqrt(ssq * (1.0 / D) + RMS_EPS)
    xn2 = h1_ref[...] * inv * gf_ref[...] * (1.0 + mod_ref[:, 4 * D:5 * D]) + mod_ref[:, 3 * D:4 * D]
    xn2_ref[...] = xn2.reshape(MG_TM, D // LANES, LANES)
    hi = xn2.astype(bf16)
    lo = (xn2 - hi.astype(f32)).astype(bf16)
    wrh = wrh_ref[...]
    logits = _dot(hi, wrh) + _dot(lo, wrh) + _dot(hi, wrl_ref[...]) + br_ref[...]

    elane = lax.broadcasted_iota(i32, (MG_TM, N_EXPERTS), 1)
    work = logits
    onehots, vals, idxs = [], [], []
    for _ in range(TOP_K):
        mk = jnp.max(work, axis=1, keepdims=True)
        idx = jnp.min(jnp.where(work == mk, elane, N_EXPERTS), axis=1, keepdims=True)
        one = elane == idx
        onehots.append(one)
        vals.append(mk)
        idxs.append(idx)
        work = jnp.where(one, -jnp.inf, work)
    exps = [jnp.exp(v - vals[0]) for v in vals]
    denom = exps[0] + exps[1] + exps[2] + exps[3]

    sel = jnp.where(onehots[0] | onehots[1] | onehots[2] | onehots[3], 1.0, 0.0)
    trow = lax.broadcasted_iota(i32, (MG_TM, MG_TM), 0)
    tcol = lax.broadcasted_iota(i32, (MG_TM, MG_TM), 1)
    strict = jnp.where(tcol < trow, 1.0, 0.0).astype(bf16)
    rank = _dot(strict, sel.astype(bf16)) + cnt_ref[...]
    cnt_ref[...] = cnt_ref[...] + jnp.sum(sel, axis=0, keepdims=True)

    rlane = lax.broadcasted_iota(i32, (MG_TM, ROUTE_W), 1)
    rec = jnp.zeros((MG_TM, ROUTE_W), f32)
    for k in range(TOP_K):
        rk = jnp.sum(jnp.where(onehots[k], rank, 0.0), axis=1, keepdims=True)
        rec = jnp.where(rlane == R_IDX + k, idxs[k].astype(f32), rec)
        rec = jnp.where(rlane == R_GATE + k, exps[k] / denom, rec)
        rec = jnp.where(rlane == R_RANK + k, rk, rec)
    route_ref[...] = rec


def _merge(ya, yb, proj, x2, mod3, wa, wb, wo, g_ffn, wr_hi, wr_lo, b_r, seq):
    n = x2.shape[0]
    tiles_per_batch = seq // MG_TM
    const = lambda i: (0, 0)
    return pl.pallas_call(
        _merge_kernel,
        grid=(n // MG_TM,),
        in_specs=[
            pl.BlockSpec((MG_TM, SWA_QW), lambda i: (i, 0)),
            pl.BlockSpec((MG_TM, FOX_W), lambda i: (i, 0)),
            pl.BlockSpec((MG_TM, D), lambda i: (i, P_SA // D)),
            pl.BlockSpec((MG_TM, D), lambda i: (i, P_SB // D)),
            pl.BlockSpec((MG_TM, D), lambda i: (i, 0)),
            pl.BlockSpec((None, 1, 6 * D), lambda i: (i // tiles_per_batch, 0, 0)),
            pl.BlockSpec((SWA_QW, D), const),
            pl.BlockSpec((FOX_W, D), const),
            pl.BlockSpec((D, D), const),
            pl.BlockSpec((1, D), const),
            pl.BlockSpec((D, N_EXPERTS), const),
            pl.BlockSpec((D, N_EXPERTS), const),
            pl.BlockSpec((1, N_EXPERTS), const),
        ],
        out_specs=[
            pl.BlockSpec((MG_TM, D), lambda i: (i, 0)),
            pl.BlockSpec((MG_TM, D // LANES, LANES), lambda i: (i, 0, 0)),
            pl.BlockSpec((MG_TM, ROUTE_W), lambda i: (i, 0)),
        ],
        out_shape=[
            jax.ShapeDtypeStruct((n, D), f32),
            jax.ShapeDtypeStruct((n, D // LANES, LANES), f32),
            jax.ShapeDtypeStruct((n, ROUTE_W), f32),
        ],
        scratch_shapes=[pltpu.VMEM((MG_TM, D), bf16), pltpu.VMEM((1, N_EXPERTS), f32)],
        compiler_params=_cparams(("arbitrary",)),
        name="merge_router",
    )(ya, yb, proj, proj, x2, mod3, wa, wb, wo, g_ffn, wr_hi, wr_lo, b_r)


MOE_TM = 256
GATHER_CH = 256


SLAB = D // LANES


def _gather_kernel(nrows_ref, tok_ref, tok_next_ref, x_ref, o_ref, stage, sems):
    i = pl.program_id(0)
    n = pl.num_programs(0)
    slot = i % 2
    live = i * GATHER_CH < nrows_ref[0]
    live_next = (i + 1 < n) & ((i + 1) * GATHER_CH < nrows_ref[0])

    def copies(idx_ref, sl):
        return [pltpu.make_async_copy(x_ref.at[idx_ref[0, 0, r]], stage.at[sl, r], sems.at[sl])
                for r in range(GATHER_CH)]

    def start(idx_ref, sl):
        for r, c in enumerate(copies(idx_ref, sl)):
            c.start(priority=r % 2)

    @pl.when((i == 0) & live)
    def _():
        start(tok_ref, 0)

    for sl in range(2):
        @pl.when(live_next & (slot == 1 - sl))
        def _():
            start(tok_next_ref, sl)

    for sl in range(2):
        @pl.when(live & (slot == sl))
        def _():
            for c in copies(tok_ref, sl):
                c.wait()
            o_ref[...] = stage[sl].reshape(GATHER_CH, D).astype(bf16)

    @pl.when(jnp.logical_not(live))
    def _():
        o_ref[...] = jnp.zeros((GATHER_CH, D), bf16)


def _gather_rows(nrows, row_tok3, xn2_slab, p_alloc):
    n_chunks = p_alloc // GATHER_CH
    return pl.pallas_call(
        _gather_kernel,
        grid=(n_chunks,),
        in_specs=[
            pl.BlockSpec(memory_space=pltpu.SMEM),
            pl.BlockSpec((1, 1, GATHER_CH), lambda i: (i, 0, 0), memory_space=pltpu.SMEM),
            pl.BlockSpec((1, 1, GATHER_CH), lambda i: (jnp.minimum(i + 1, n_chunks - 1), 0, 0),
                         memory_space=pltpu.SMEM),
            pl.BlockSpec(memory_space=pl.ANY),
        ],
        out_specs=pl.BlockSpec((GATHER_CH, D), lambda i: (i, 0)),
        out_shape=jax.ShapeDtypeStruct((p_alloc, D), bf16),
        scratch_shapes=[pltpu.VMEM((2, GATHER_CH, SLAB, LANES), f32), pltpu.SemaphoreType.DMA((2,))],
        compiler_params=_cparams(("arbitrary",)),
        name="moe_gather",
    )(nrows, row_tok3, row_tok3, xn2_slab)


MOE_RB = 8
MOE_TF = 512
MOE_NF = D_EXPERT // MOE_TF
MOE_VMEM = 60 * 1024 * 1024


def _moe_kernel(ge_ref, gs_ref, gn_ref, gz_ref, x_hbm, wg_ref, wl_ref, bg_ref, bl_ref, wd_ref, bd_ref, y_hbm,
                xbuf, yacc, yst, sem_in, sem_out):
    s = pl.program_id(0)
    f = pl.program_id(1)
    nb = gn_ref[s]
    nz = gz_ref[s]
    blk0 = gs_ref[s]

    def x_copy(r):
        return pltpu.make_async_copy(x_hbm.at[pl.ds((blk0 + r) * MOE_TM, MOE_TM)], xbuf.at[r], sem_in)

    def y_copy(r, slot):
        return pltpu.make_async_copy(yst.at[slot], y_hbm.at[pl.ds((blk0 + r) * MOE_TM, MOE_TM)],
                                     sem_out.at[slot])

    @pl.when((f == 0) & (nz > 0))
    def _():
        yst[0] = jnp.zeros((MOE_TM, SLAB, LANES), f32)
        for r in range(MOE_RB):
            @pl.when(r < nz)
            def _():
                y_copy(r, 0).start()
        for r in range(MOE_RB):
            @pl.when(r < nz)
            def _():
                y_copy(r, 0).wait()

    @pl.when(f == 0)
    def _():
        for r in range(MOE_RB):
            @pl.when(r < nb)
            def _():
                x_copy(r).start()

        def init(r, _):
            yacc[r] = jnp.broadcast_to(bd_ref[...], (MOE_TM, D))
            return 0

        lax.fori_loop(0, nb, init, 0)
        for r in range(MOE_RB):
            @pl.when(r < nb)
            def _():
                x_copy(r).wait()

    def up(r):
        x = xbuf[r]
        glu = _dot(x, wg_ref[...]) + bg_ref[...]
        lin = _dot(x, wl_ref[...]) + bl_ref[...]
        glu = jnp.minimum(glu, SWIGLU_LIMIT)
        lin = jnp.clip(lin, -SWIGLU_LIMIT, SWIGLU_LIMIT)
        return (glu * jax.nn.sigmoid(SWIGLU_ALPHA * glu) * (lin + 1.0)).astype(bf16)

    def down(r, act):
        yacc[r] = yacc[r] + _dot(act, wd_ref[...])

    @pl.when(nb > 0)
    def _():
        def body(r, act_prev):
            down(r - 1, act_prev)
            return up(r)

        act_last = lax.fori_loop(1, nb, body, up(0))
        down(nb - 1, act_last)

    @pl.when((f == MOE_NF - 1) & (nb > 0))
    def _():
        def emit(r, _):
            slot = r % 2

            @pl.when(r >= 2)
            def _():
                y_copy(r - 2, slot).wait()

            yst[slot] = yacc[r].reshape(MOE_TM, SLAB, LANES)
            y_copy(r, slot).start()
            return 0

        lax.fori_loop(0, nb, emit, 0)

        @pl.when(nb >= 2)
        def _():
            y_copy(nb - 2, nb % 2).wait()

        y_copy(nb - 1, (nb - 1) % 2).wait()


def _moe(g_expert, g_start, g_nblk, g_nzero, x_rows, w_gu, b_gu3, w_down, b_down3, n_groups):
    p_alloc = x_rows.shape[0]

    def f_eff(s, f, gn):
        return jnp.where(gn[s] > 0, f, MOE_NF - 1)

    grid_spec = pltpu.PrefetchScalarGridSpec(
        num_scalar_prefetch=4,
        grid=(n_groups, MOE_NF),
        in_specs=[
            pl.BlockSpec(memory_space=pl.ANY),
            pl.BlockSpec((None, D, MOE_TF), lambda s, f, ge, gs, gn, gz: (ge[s], 0, f_eff(s, f, gn))),
            pl.BlockSpec((None, D, MOE_TF), lambda s, f, ge, gs, gn, gz: (ge[s], 0, MOE_NF + f_eff(s, f, gn))),
            pl.BlockSpec((None, 1, MOE_TF), lambda s, f, ge, gs, gn, gz: (ge[s], 0, f_eff(s, f, gn))),
            pl.BlockSpec((None, 1, MOE_TF), lambda s, f, ge, gs, gn, gz: (ge[s], 0, MOE_NF + f_eff(s, f, gn))),
            pl.BlockSpec((None, MOE_TF, D), lambda s, f, ge, gs, gn, gz: (ge[s], f_eff(s, f, gn), 0)),
            pl.BlockSpec((None, 1, D), lambda s, f, ge, gs, gn, gz: (ge[s], 0, 0)),
        ],
        out_specs=pl.BlockSpec(memory_space=pl.ANY),
        scratch_shapes=[
            pltpu.VMEM((MOE_RB, MOE_TM, D), bf16),
            pltpu.VMEM((MOE_RB, MOE_TM, D), f32),
            pltpu.VMEM((2, MOE_TM, SLAB, LANES), f32),
            pltpu.SemaphoreType.DMA(()),
            pltpu.SemaphoreType.DMA((2,)),
        ],
    )
    return pl.pallas_call(
        _moe_kernel,
        grid_spec=grid_spec,
        out_shape=jax.ShapeDtypeStruct((p_alloc, SLAB, LANES), f32),
        compiler_params=_cparams(("arbitrary", "arbitrary"), MOE_VMEM),
        name="moe_experts",
    )(g_expert, g_start, g_nblk, g_nzero, x_rows, w_gu, w_gu, b_gu3, b_gu3, w_down, b_down3)


CB_TM = 64


def _combine_kernel(dest_ref, dest_next_ref, y_hbm, h1_ref, route_ref, mod_ref, g_ref, o_ref, ybuf, sems):
    i = pl.program_id(0)
    n = pl.num_programs(0)
    slot = i % 2

    def copies(idx_ref, sl):
        return [pltpu.make_async_copy(y_hbm.at[idx_ref[0, 0, t * TOP_K + k]], ybuf.at[sl, k, t], sems.at[sl])
                for t in range(CB_TM) for k in range(TOP_K)]

    def start(idx_ref, sl):
        for j, c in enumerate(copies(idx_ref, sl)):
            c.start(priority=j % 2)

    @pl.when(i == 0)
    def _():
        start(dest_ref, 0)

    for sl in range(2):
        @pl.when((i + 1 < n) & (slot == 1 - sl))
        def _():
            start(dest_next_ref, sl)

    route = route_ref[...]
    gt2 = mod_ref[:, 5 * D:6 * D]
    for sl in range(2):
        @pl.when(slot == sl)
        def _():
            for c in copies(dest_ref, sl):
                c.wait()
            moe = jnp.zeros((CB_TM, D), f32)
            for k in range(TOP_K):
                moe = moe + route[:, R_GATE + k:R_GATE + k + 1] * ybuf[sl, k].reshape(CB_TM, D)
            h = h1_ref[...] + gt2 * moe
            ms = jnp.mean(h * h, axis=-1, keepdims=True)
            o_ref[...] = h * lax.rsqrt(ms + RMS_EPS) * g_ref[...]


def _combine(dest3, y_rows, h1, route, mod3, g_final, seq):
    n = h1.shape[0]
    tiles_per_batch = seq // CB_TM
    n_tiles = n // CB_TM
    return pl.pallas_call(
        _combine_kernel,
        grid=(n_tiles,),
        in_specs=[
            pl.BlockSpec((1, 1, CB_TM * TOP_K), lambda i: (i, 0, 0), memory_space=pltpu.SMEM),
            pl.BlockSpec((1, 1, CB_TM * TOP_K), lambda i: (jnp.minimum(i + 1, n_tiles - 1), 0, 0),
                         memory_space=pltpu.SMEM),
            pl.BlockSpec(memory_space=pl.ANY),
            pl.BlockSpec((CB_TM, D), lambda i: (i, 0)),
            pl.BlockSpec((CB_TM, ROUTE_W), lambda i: (i, 0)),
            pl.BlockSpec((None, 1, 6 * D), lambda i: (i // tiles_per_batch, 0, 0)),
            pl.BlockSpec((1, D), lambda i: (0, 0)),
        ],
        out_specs=pl.BlockSpec((CB_TM, D), lambda i: (i, 0)),
        out_shape=jax.ShapeDtypeStruct((n, D), f32),
        scratch_shapes=[pltpu.VMEM((2, TOP_K, CB_TM, SLAB, LANES), f32), pltpu.SemaphoreType.DMA((2,))],
        compiler_params=_cparams(("arbitrary",)),
        name="moe_combine",
    )(dest3, dest3, y_rows, h1, route, mod3, g_final)


def _layer(h, c_act_pad, positions, w_ada, b_ada, g_mix, w_in, b_in, sinks, w_branch_a, w_branch_b, w_out,
           g_ffn, w_router, b_router, w_gu, b_gu, w_down, b_down):
    batch, seq, _ = h.shape
    n = batch * seq
    x2 = h.reshape(n, D)

    mod = _adaln(c_act_pad, w_ada, b_ada.reshape(1, 6 * D))
    mod3 = mod[:batch].reshape(batch, 1, 6 * D)

    w_cat = jnp.concatenate([w_in[:, GATE_OFF:], w_in[:, :QKV_W]], axis=1).astype(bf16)
    b_cat = jnp.concatenate([b_in[GATE_OFF:], b_in[:QKV_W]]).reshape(1, PROJ_W)
    w_f = w_in[:, FGATE_OFF:GATE_OFF].astype(bf16)
    b_f = b_in[FGATE_OFF:GATE_OFF].reshape(1, N_FGATE)

    inv_freq = 1.0 / (ROPE_THETA ** (jnp.arange(0, HD, 2, dtype=f32) / HD))
    ang = positions.astype(f32).reshape(n, 1) * inv_freq[None, :]
    cos_t = jnp.tile(jnp.cos(ang), (1, LANES // (HD // 2)))
    sin_h = jnp.sin(ang)
    sin_t = jnp.tile(jnp.concatenate([-sin_h, sin_h], axis=1), (1, LANES // HD))

    proj, logf = _inproj(x2, mod3, g_mix.reshape(1, D), w_cat, b_cat, w_f, b_f, cos_t, sin_t, seq)
    cum = _cumsum(logf.reshape(batch, seq, N_FGATE))

    ya = _swa(sinks, proj, batch, seq)
    yb = _fox(proj, cum, batch, seq)

    wr_hi = w_router.astype(bf16)
    wr_lo = (w_router - wr_hi.astype(f32)).astype(bf16)
    h1, xn2, route = _merge(ya, yb, proj, x2, mod3, w_branch_a.astype(bf16), w_branch_b.astype(bf16),
                            w_out.astype(bf16), g_ffn.reshape(1, D), wr_hi, wr_lo,
                            b_router.reshape(1, N_EXPERTS), seq)

    a_total = n * TOP_K
    top_idx = route[:, R_IDX:R_IDX + TOP_K].astype(i32)
    rank = route[:, R_RANK:R_RANK + TOP_K].astype(i32)
    counts = jnp.sum((top_idx[..., None] == jnp.arange(N_EXPERTS, dtype=i32)).astype(i32), axis=(0, 1))
    nblk_e = (counts + MOE_TM - 1) // MOE_TM
    blk_end = jnp.cumsum(nblk_e)
    blk_start = blk_end - nblk_e
    dest = (blk_start[top_idx] * MOE_TM + rank).reshape(a_total)
    p_alloc = a_total + N_EXPERTS * MOE_TM
    tok = jnp.arange(a_total, dtype=i32) // TOP_K
    row_tok = jnp.zeros((p_alloc,), i32).at[dest].set(tok, unique_indices=True, mode="promise_in_bounds")

    n_groups = (a_total // MOE_TM + N_EXPERTS + N_EXPERTS * (MOE_RB - 1)) // MOE_RB
    ng_e = (nblk_e + MOE_RB - 1) // MOE_RB
    g_end = jnp.cumsum(ng_e)
    gid = jnp.arange(n_groups, dtype=i32)
    g_exp_raw = jnp.searchsorted(g_end, gid, side="right").astype(i32)
    active = gid < g_end[-1]
    last_e = jnp.max(jnp.where(counts > 0, jnp.arange(N_EXPERTS, dtype=i32), 0))
    g_expert = jnp.where(active, jnp.minimum(g_exp_raw, N_EXPERTS - 1), last_e).astype(i32)
    g_local = gid - (g_end - ng_e)[g_expert]
    g_nblk = jnp.where(active, jnp.minimum(nblk_e[g_expert] - g_local * MOE_RB, MOE_RB), 0).astype(i32)
    tail0 = blk_end[-1] + (gid - g_end[-1]) * MOE_RB
    g_nzero = jnp.where(active, 0, jnp.clip(p_alloc // MOE_TM - tail0, 0, MOE_RB)).astype(i32)
    g_start = jnp.where(active, blk_start[g_expert] + g_local * MOE_RB,
                        jnp.minimum(tail0, p_alloc // MOE_TM)).astype(i32)

    nrows = (blk_end[-1] * MOE_TM).reshape(1).astype(i32)
    x_rows = _gather_rows(nrows, row_tok.reshape(p_alloc // GATHER_CH, 1, GATHER_CH), xn2, p_alloc)
    y_rows = _moe(g_expert, g_start, g_nblk, g_nzero, x_rows, w_gu, b_gu.reshape(N_EXPERTS, 1, 2 * D_EXPERT),
                  w_down, b_down.reshape(N_EXPERTS, 1, D), n_groups)
    return dest, y_rows, h1, route, mod3


def kernel(x, c, positions, w_ada, b_ada, g_mix, w_in, b_in, sinks, w_branch_a, w_branch_b, w_out, g_ffn,
           w_router, b_router, w_gu, b_gu, w_down, b_down, g_final):
    batch, seq, _ = x.shape
    depth = w_ada.shape[0]
    assert depth == 1, "single-layer trunk"
    c_pad = jnp.zeros((8, D), f32).at[:batch].set(c)
    l = 0
    dest, y_rows, h1, route, mod3 = _layer(
        x, c_pad, positions, w_ada[l], b_ada[l], g_mix[l], w_in[l], b_in[l], sinks[l], w_branch_a[l],
        w_branch_b[l], w_out[l], g_ffn[l], w_router[l], b_router[l], w_gu[l], b_gu[l], w_down[l], b_down[l])
    n = batch * seq
    out = _combine(dest.reshape(n // CB_TM, 1, CB_TM * TOP_K), y_rows, h1, route, mod3,
                   g_final.reshape(1, D), seq)
    return out.reshape(batch, seq, D)
```

```python
import functools

import jax
import jax.numpy as jnp
from jax import lax
from jax.experimental import pallas as pl
from jax.experimental.pallas import tpu as pltpu

f32 = jnp.float32
bf16 = jnp.bfloat16
i32 = jnp.int32

D = 2048
HD = 64
SWA_QW = 1024
SWA_KVW = 256
FOX_W = 1024
N_FGATE = 16
WINDOW = 128
N_EXPERTS = 32
TOP_K = 4
D_EXPERT = 2048
SWIGLU_LIMIT = 7.0
SWIGLU_ALPHA = 1.702
RMS_EPS = 1e-5
NEG_INF = -1e30
ROPE_THETA = 10000.0

QKV_W = SWA_QW + 2 * SWA_KVW + 3 * FOX_W
FGATE_OFF = QKV_W
GATE_OFF = QKV_W + N_FGATE

LANES = 128
P_SA = 0
P_SB = D
P_QA = 2 * D
P_KA = P_QA + SWA_QW
P_VA = P_KA + SWA_KVW
P_QB = P_VA + SWA_KVW
P_KB = P_QB + FOX_W
P_VB = P_KB + FOX_W
PROJ_W = P_VB + FOX_W

VMEM_LIMIT = 56 * 1024 * 1024


def _cparams(sem, vmem=VMEM_LIMIT):
    return pltpu.CompilerParams(dimension_semantics=sem, vmem_limit_bytes=vmem)


def _dot(a, b):
    return jnp.dot(a, b, preferred_element_type=f32)


ADA_TN = 1024


def _adaln_kernel(c_ref, w_ref, b_ref, o_ref):
    c = c_ref[...]
    ca = c * jax.nn.sigmoid(c)
    o_ref[...] = _dot(ca, w_ref[...]) + b_ref[...]


def _adaln(c_pad, w_ada, b_ada):
    rows = c_pad.shape[0]
    n = w_ada.shape[1]
    return pl.pallas_call(
        _adaln_kernel,
        grid=(n // ADA_TN,),
        in_specs=[
            pl.BlockSpec((rows, D), lambda j: (0, 0)),
            pl.BlockSpec((D, ADA_TN), lambda j: (0, j)),
            pl.BlockSpec((1, ADA_TN), lambda j: (0, j)),
        ],
        out_specs=pl.BlockSpec((rows, ADA_TN), lambda j: (0, j)),
        out_shape=jax.ShapeDtypeStruct((rows, n), f32),
        compiler_params=_cparams(("arbitrary",)),
        name="adaln",
    )(c_pad, w_ada, b_ada)


IN_TM = 1024
IN_TN = 512
IN_NJ = PROJ_W // IN_TN
J_QA0 = P_QA // IN_TN
J_KVA = P_KA // IN_TN
J_QB0 = P_QB // IN_TN
J_KB0 = P_KB // IN_TN


def _rope_block(a, cos, sin_signed, first_half):
    fwd = pltpu.roll(a, LANES - HD // 2, axis=1)
    bwd = pltpu.roll(a, HD // 2, axis=1)
    partner = jnp.where(first_half, fwd, bwd)
    return a * cos + partner * sin_signed


def _inproj_kernel(x_ref, mod_ref, g_ref, w_ref, b_ref, wf_ref, bf_ref, cos_ref, sin_ref,
                   proj_ref, logf_ref, xn_ref):
    j = pl.program_id(1)

    @pl.when(j == 0)
    def _():
        x = x_ref[...]
        ms = jnp.mean(x * x, axis=-1, keepdims=True)
        y = x * lax.rsqrt(ms + RMS_EPS) * g_ref[...]
        xn = y * (1.0 + mod_ref[:, D:2 * D]) + mod_ref[:, 0:D]
        xnb = xn.astype(bf16)
        xn_ref[...] = xnb
        fl = _dot(xnb, wf_ref[...]) + bf_ref[...]
        logf_ref[...] = jnp.minimum(fl, 0.0) - jnp.log(1.0 + jnp.exp(-jnp.abs(fl)))

    acc = _dot(xn_ref[...], w_ref[...]) + b_ref[...]

    lane = lax.broadcasted_iota(i32, (IN_TM, LANES), 1)
    first_half = (lane % HD) < (HD // 2)
    scale = HD ** -0.5

    def rope_cols(n_blocks, mult):
        cos = cos_ref[...]
        sin = sin_ref[...]
        for g in range(n_blocks):
            sl = slice(g * LANES, (g + 1) * LANES)
            r = _rope_block(acc[:, sl], cos, sin, first_half)
            proj_ref[:, sl] = (r * mult).astype(bf16)

    @pl.when(j < J_QA0)
    def _():
        proj_ref[...] = jax.nn.sigmoid(acc).astype(bf16)

    @pl.when((j >= J_QA0) & (j < J_KVA))
    def _():
        rope_cols(IN_TN // LANES, scale)

    @pl.when(j == J_KVA)
    def _():
        rope_cols(SWA_KVW // LANES, 1.0)
        proj_ref[:, SWA_KVW:] = acc[:, SWA_KVW:].astype(bf16)

    @pl.when((j >= J_QB0) & (j < J_KB0))
    def _():
        proj_ref[...] = (acc * scale).astype(bf16)

    @pl.when(j >= J_KB0)
    def _():
        proj_ref[...] = acc.astype(bf16)


def _inproj(x2, mod3, g_mix, w_cat, b_cat, w_f, b_f, cos_t, sin_t, seq):
    n = x2.shape[0]
    tiles_per_batch = seq // IN_TM
    return pl.pallas_call(
        _inproj_kernel,
        grid=(n // IN_TM, IN_NJ),
        in_specs=[
            pl.BlockSpec((IN_TM, D), lambda i, j: (i, 0)),
            pl.BlockSpec((None, 1, 6 * D), lambda i, j: (i // tiles_per_batch, 0, 0)),
            pl.BlockSpec((1, D), lambda i, j: (0, 0)),
            pl.BlockSpec((D, IN_TN), lambda i, j: (0, j)),
            pl.BlockSpec((1, IN_TN), lambda i, j: (0, j)),
            pl.BlockSpec((D, N_FGATE), lambda i, j: (0, 0)),
            pl.BlockSpec((1, N_FGATE), lambda i, j: (0, 0)),
            pl.BlockSpec((IN_TM, LANES), lambda i, j: (i, 0)),
            pl.BlockSpec((IN_TM, LANES), lambda i, j: (i, 0)),
        ],
        out_specs=[
            pl.BlockSpec((IN_TM, IN_TN), lambda i, j: (i, j)),
            pl.BlockSpec((IN_TM, N_FGATE), lambda i, j: (i, 0)),
        ],
        out_shape=[
            jax.ShapeDtypeStruct((n, PROJ_W), bf16),
            jax.ShapeDtypeStruct((n, N_FGATE), f32),
        ],
        scratch_shapes=[pltpu.VMEM((IN_TM, D), bf16)],
        compiler_params=_cparams(("arbitrary", "arbitrary")),
        name="inproj",
    )(x2, mod3, g_mix, w_cat, b_cat, w_f, b_f, cos_t, sin_t)


CUM_CH = 256


def _split3(x):
    x1 = x.astype(bf16)
    r1 = x - x1.astype(f32)
    x2 = r1.astype(bf16)
    r2 = r1 - x2.astype(f32)
    return x1, x2, r2.astype(bf16)


def _cumsum_kernel(x_ref, o_ref):
    seq = x_ref.shape[0]
    row = lax.broadcasted_iota(i32, (CUM_CH, CUM_CH), 0)
    col = lax.broadcasted_iota(i32, (CUM_CH, CUM_CH), 1)
    tri = jnp.where(col <= row, 1.0, 0.0).astype(bf16)
    carry = jnp.zeros((1, N_FGATE), f32)
    for c in range(seq // CUM_CH):
        sl = slice(c * CUM_CH, (c + 1) * CUM_CH)
        x1, x2, x3 = _split3(x_ref[sl, :])
        cs = _dot(tri, x1) + _dot(tri, x2) + _dot(tri, x3) + carry
        o_ref[sl, :] = cs
        carry = cs[CUM_CH - 1:CUM_CH, :]


def _cumsum(logf3):
    b, seq, h = logf3.shape
    return pl.pallas_call(
        _cumsum_kernel,
        grid=(b,),
        in_specs=[pl.BlockSpec((None, seq, h), lambda i: (i, 0, 0))],
        out_specs=pl.BlockSpec((None, seq, h), lambda i: (i, 0, 0)),
        out_shape=jax.ShapeDtypeStruct((b, seq, h), f32),
        compiler_params=_cparams(("arbitrary",)),
        name="fgate_cumsum",
    )(logf3)


SWA_QCOLS = 512


def _swa_kernel(sinks_ref, q_ref, k_ref, v_ref, o_ref, vt_ref):
    jp = pl.program_id(1)
    seq = q_ref.shape[0]
    nblk = seq // WINDOW

    for c in range(nblk):
        vblk = v_ref[c * WINDOW:(c + 1) * WINDOW, :].astype(f32)
        vt_ref[c] = vblk.T.astype(bf16)

    kj = lax.broadcasted_iota(i32, (2 * WINDOW, WINDOW), 0)
    qi = lax.broadcasted_iota(i32, (2 * WINDOW, WINDOW), 1)
    delta = qi + WINDOW - kj
    band = (delta >= 0) & (delta < WINDOW)
    zeros_half = jnp.zeros((HD, WINDOW), bf16)

    def body(n, _):
        prev = jnp.maximum(n - 1, 0)
        r0 = pl.multiple_of(n * WINDOW, WINDOW)
        p0 = pl.multiple_of(prev * WINDOW, WINDOW)
        kwin = jnp.concatenate([k_ref[pl.ds(p0, WINDOW), :], k_ref[pl.ds(r0, WINDOW), :]], axis=0)
        valid = band & ((n - 1) * WINDOW + kj >= 0)
        qblk = q_ref[pl.ds(r0, WINDOW), :]
        for t in range(SWA_QCOLS // LANES):
            g = t // 2
            qt = qblk[:, t * LANES:(t + 1) * LANES].astype(f32).T
            v_prev = vt_ref[prev, g * HD:(g + 1) * HD, :]
            v_cur = vt_ref[n, g * HD:(g + 1) * HD, :]
            outs = []
            for u in range(2):
                sink = sinks_ref[jp * 8 + t * 2 + u]
                feat = qt[u * HD:(u + 1) * HD, :].astype(bf16)
                wq = jnp.concatenate([feat, zeros_half] if g == 0 else [zeros_half, feat], axis=0)
                st = _dot(kwin, wq)
                st = jnp.where(valid, st, NEG_INF)
                m = jnp.maximum(jnp.max(st, axis=0, keepdims=True), sink)
                pt = jnp.exp(st - m)
                denom = jnp.sum(pt, axis=0, keepdims=True) + jnp.exp(sink - m)
                ptb = pt.astype(bf16)
                ot = _dot(v_prev, ptb[:WINDOW, :]) + _dot(v_cur, ptb[WINDOW:, :])
                outs.append(ot / denom)
            o_ref[pl.ds(r0, WINDOW), t * LANES:(t + 1) * LANES] = (
                jnp.concatenate(outs, axis=0).T.astype(bf16))
        return 0

    lax.fori_loop(0, nblk, body, 0)


def _swa(sinks, proj, batch, seq):
    n = proj.shape[0]
    qb0 = P_QA // SWA_QCOLS
    kb0 = P_KA // LANES
    vb0 = P_VA // LANES
    return pl.pallas_call(
        _swa_kernel,
        grid=(batch, SWA_QW // SWA_QCOLS),
        in_specs=[
            pl.BlockSpec(memory_space=pltpu.SMEM),
            pl.BlockSpec((seq, SWA_QCOLS), lambda b, j: (b, qb0 + j)),
            pl.BlockSpec((seq, LANES), lambda b, j: (b, kb0 + j)),
            pl.BlockSpec((seq, LANES), lambda b, j: (b, vb0 + j)),
        ],
        out_specs=pl.BlockSpec((seq, SWA_QCOLS), lambda b, j: (b, j)),
        out_shape=jax.ShapeDtypeStruct((n, SWA_QW), bf16),
        scratch_shapes=[pltpu.VMEM((seq // WINDOW, LANES, WINDOW), bf16)],
        compiler_params=_cparams(("arbitrary", "arbitrary")),
        name="swa_attention",
    )(sinks, proj, proj, proj)


FOX_T = 256
LOG2E = 1.4426950408889634


def _fox_kernel(q_ref, k_ref, v_ref, cum_ref, o_ref, vt_ref, kb_ref):
    p = pl.program_id(1)
    seq = q_ref.shape[0]
    nblk = seq // FOX_T

    for c in range(nblk):
        vblk = v_ref[c * FOX_T:(c + 1) * FOX_T, :].astype(f32)
        vt_ref[c] = vblk.T.astype(bf16)

    cum = cum_ref[...]
    hl = lax.broadcasted_iota(i32, cum.shape, 1)
    for u in range(2):
        col = jnp.sum(jnp.where(hl == 2 * p + u, cum, 0.0), axis=1, keepdims=True)
        kb_ref[u] = jnp.broadcast_to(col * LOG2E, (seq, LANES))

    frow = lax.broadcasted_iota(i32, (LANES, FOX_T), 0)
    kidx = lax.broadcasted_iota(i32, (FOX_T, FOX_T), 0)
    qidx = lax.broadcasted_iota(i32, (FOX_T, FOX_T), 1)
    causal = kidx <= qidx

    def scores(u, wq, k0):
        kblk = k_ref[pl.ds(k0, FOX_T), :]
        kb = kb_ref[u, pl.ds(k0, FOX_T), :]
        return _dot(kblk, wq) - jnp.concatenate([kb] * (FOX_T // LANES), axis=1)

    def update(u, st, jk, state):
        m, l, acc = state
        m_new = jnp.maximum(m, jnp.max(st, axis=0, keepdims=True))
        alpha = jnp.exp2(m - m_new)
        pt = jnp.exp2(st - m_new)
        l_new = alpha * l + jnp.sum(pt, axis=0, keepdims=True)
        vt = vt_ref[jk, u * HD:(u + 1) * HD, :]
        acc_new = alpha * acc + _dot(vt, pt.astype(bf16))
        return m_new, l_new, acc_new

    def qblock(iq, _):
        q0 = pl.multiple_of(iq * FOX_T, FOX_T)
        qt = q_ref[pl.ds(q0, FOX_T), :].astype(f32).T
        qt = qt * LOG2E
        wqs = [jnp.where((frow // HD) == u, qt, 0.0).astype(bf16) for u in range(2)]
        init = tuple((jnp.full((1, FOX_T), NEG_INF, f32), jnp.zeros((1, FOX_T), f32),
                      jnp.zeros((HD, FOX_T), f32)) for _ in range(2))

        def kblock(jk, carry):
            sts, states = carry
            k1 = pl.multiple_of((jk + 1) * FOX_T, FOX_T)
            nxt = tuple(scores(u, wqs[u], k1) for u in range(2))
            return nxt, tuple(update(u, sts[u], jk, states[u]) for u in range(2))

        first = tuple(scores(u, wqs[u], 0) for u in range(2))
        sts, states = lax.fori_loop(0, iq, kblock, (first, init))
        outs = []
        for u in range(2):
            _, l, acc = update(u, jnp.where(causal, sts[u], NEG_INF), iq, states[u])
            outs.append(acc / l)
        o_ref[pl.ds(q0, FOX_T), :] = jnp.concatenate(outs, axis=0).T.astype(bf16)
        return 0

    lax.fori_loop(0, nblk, qblock, 0)


def _fox(proj, cum, batch, seq):
    n = proj.shape[0]
    qb0 = P_QB // LANES
    kb0 = P_KB // LANES
    vb0 = P_VB // LANES
    return pl.pallas_call(
        _fox_kernel,
        grid=(batch, FOX_W // LANES),
        in_specs=[
            pl.BlockSpec((seq, LANES), lambda b, p: (b, qb0 + p)),
            pl.BlockSpec((seq, LANES), lambda b, p: (b, kb0 + p)),
            pl.BlockSpec((seq, LANES), lambda b, p: (b, vb0 + p)),
            pl.BlockSpec((None, seq, N_FGATE), lambda b, p: (b, 0, 0)),
        ],
        out_specs=pl.BlockSpec((seq, LANES), lambda b, p: (b, p)),
        out_shape=jax.ShapeDtypeStruct((n, FOX_W), bf16),
        scratch_shapes=[pltpu.VMEM((seq // FOX_T, LANES, FOX_T), bf16),
                        pltpu.VMEM((2, seq, LANES), f32)],
        compiler_params=_cparams(("arbitrary", "arbitrary")),
        name="fox_attention",
    )(proj, proj, proj, cum)


MG_TM = 256
MG_TN = 512
ROUTE_W = LANES
R_IDX, R_GATE, R_RANK = 0, TOP_K, 2 * TOP_K


def _merge_kernel(ya_ref, yb_ref, sa_ref, sb_ref, x_ref, mod_ref, wa_ref, wb_ref, wo_ref, gf_ref,
                  wrh_ref, wrl_ref, br_ref, h1_ref, xn2_ref, route_ref, merged_ref, cnt_ref):
    i = pl.program_id(0)

    @pl.when(i == 0)
    def _():
        cnt_ref[...] = jnp.zeros_like(cnt_ref)

    ya = ya_ref[...]
    yb = yb_ref[...]
    for c in range(D // MG_TN):
        sl = slice(c * MG_TN, (c + 1) * MG_TN)
        ta = _dot(ya, wa_ref[:, sl])
        tb = _dot(yb, wb_ref[:, sl])
        merged_ref[:, sl] = (sa_ref[:, sl].astype(f32) * ta + sb_ref[:, sl].astype(f32) * tb).astype(bf16)

    merged = merged_ref[...]
    ssq = jnp.zeros((MG_TM, 1), f32)
    for c in range(D // MG_TN):
        sl = slice(c * MG_TN, (c + 1) * MG_TN)
        o = _dot(merged, wo_ref[:, sl])
        h = x_ref[:, sl] + mod_ref[:, 2 * D + c * MG_TN:2 * D + (c + 1) * MG_TN] * o
        h1_ref[:, sl] = h
        ssq = ssq + jnp.sum(h * h, axis=1, keepdims=True)

    inv = lax.rsqrt(ssq * (1.0 / D) + RMS_EPS)
    xn2 = h1_ref[...] * inv * gf_ref[...] * (1.0 + mod_ref[:, 4 * D:5 * D]) + mod_ref[:, 3 * D:4 * D]
    xn2_ref[...] = xn2.reshape(MG_TM, D // LANES, LANES)
    hi = xn2.astype(bf16)
    lo = (xn2 - hi.astype(f32)).astype(bf16)
    wrh = wrh_ref[...]
    logits = _dot(hi, wrh) + _dot(lo, wrh) + _dot(hi, wrl_ref[...]) + br_ref[...]

    elane = lax.broadcasted_iota(i32, (MG_TM, N_EXPERTS), 1)
    work = logits
    onehots, vals, idxs = [], [], []
    for _ in range(TOP_K):
        mk = jnp.max(work, axis=1, keepdims=True)
        idx = jnp.min(jnp.where(work == mk, elane, N_EXPERTS), axis=1, keepdims=True)
        one = elane == idx
        onehots.append(one)
        vals.append(mk)
        idxs.append(idx)
        work = jnp.where(one, -jnp.inf, work)
    exps = [jnp.exp(v - vals[0]) for v in vals]
    denom = exps[0] + exps[1] + exps[2] + exps[3]

    sel = jnp.where(onehots[0] | onehots[1] | onehots[2] | onehots[3], 1.0, 0.0)
    trow = lax.broadcasted_iota(i32, (MG_TM, MG_TM), 0)
    tcol = lax.broadcasted_iota(i32, (MG_TM, MG_TM), 1)
    strict = jnp.where(tcol < trow, 1.0, 0.0).astype(bf16)
    rank = _dot(strict, sel.astype(bf16)) + cnt_ref[...]
    cnt_ref[...] = cnt_ref[...] + jnp.sum(sel, axis=0, keepdims=True)

    rlane = lax.broadcasted_iota(i32, (MG_TM, ROUTE_W), 1)
    rec = jnp.zeros((MG_TM, ROUTE_W), f32)
    for k in range(TOP_K):
        rk = jnp.sum(jnp.where(onehots[k], rank, 0.0), axis=1, keepdims=True)
        rec = jnp.where(rlane == R_IDX + k, idxs[k].astype(f32), rec)
        rec = jnp.where(rlane == R_GATE + k, exps[k] / denom, rec)
        rec = jnp.where(rlane == R_RANK + k, rk, rec)
    route_ref[...] = rec


def _merge(ya, yb, proj, x2, mod3, wa, wb, wo, g_ffn, wr_hi, wr_lo, b_r, seq):
    n = x2.shape[0]
    tiles_per_batch = seq // MG_TM
    const = lambda i: (0, 0)
    return pl.pallas_call(
        _merge_kernel,
        grid=(n // MG_TM,),
        in_specs=[
            pl.BlockSpec((MG_TM, SWA_QW), lambda i: (i, 0)),
            pl.BlockSpec((MG_TM, FOX_W), lambda i: (i, 0)),
            pl.BlockSpec((MG_TM, D), lambda i: (i, P_SA // D)),
            pl.BlockSpec((MG_TM, D), lambda i: (i, P_SB // D)),
            pl.BlockSpec((MG_TM, D), lambda i: (i, 0)),
            pl.BlockSpec((None, 1, 6 * D), lambda i: (i // tiles_per_batch, 0, 0)),
            pl.BlockSpec((SWA_QW, D), const),
            pl.BlockSpec((FOX_W, D), const),
            pl.BlockSpec((D, D), const),
            pl.BlockSpec((1, D), const),
            pl.BlockSpec((D, N_EXPERTS), const),
            pl.BlockSpec((D, N_EXPERTS), const),
            pl.BlockSpec((1, N_EXPERTS), const),
        ],
        out_specs=[
            pl.BlockSpec((MG_TM, D), lambda i: (i, 0)),
            pl.BlockSpec((MG_TM, D // LANES, LANES), lambda i: (i, 0, 0)),
            pl.BlockSpec((MG_TM, ROUTE_W), lambda i: (i, 0)),
        ],
        out_shape=[
            jax.ShapeDtypeStruct((n, D), f32),
            jax.ShapeDtypeStruct((n, D // LANES, LANES), f32),
            jax.ShapeDtypeStruct((n, ROUTE_W), f32),
        ],
        scratch_shapes=[pltpu.VMEM((MG_TM, D), bf16), pltpu.VMEM((1, N_EXPERTS), f32)],
        compiler_params=_cparams(("arbitrary",)),
        name="merge_router",
    )(ya, yb, proj, proj, x2, mod3, wa, wb, wo, g_ffn, wr_hi, wr_lo, b_r)


MOE_TM = 256
GATHER_CH = 256


SLAB = D // LANES


def _gather_kernel(nrows_ref, tok_ref, tok_next_ref, x_ref, o_ref, stage, sems):
    i = pl.program_id(0)
    n = pl.num_programs(0)
    slot = i % 2
    live = i * GATHER_CH < nrows_ref[0]
    live_next = (i + 1 < n) & ((i + 1) * GATHER_CH < nrows_ref[0])

    def copies(idx_ref, sl):
        return [pltpu.make_async_copy(x_ref.at[idx_ref[0, 0, r]], stage.at[sl, r], sems.at[sl])
                for r in range(GATHER_CH)]

    def start(idx_ref, sl):
        for r, c in enumerate(copies(idx_ref, sl)):
            c.start(priority=r % 2)

    @pl.when((i == 0) & live)
    def _():
        start(tok_ref, 0)

    for sl in range(2):
        @pl.when(live_next & (slot == 1 - sl))
        def _():
            start(tok_next_ref, sl)

    for sl in range(2):
        @pl.when(live & (slot == sl))
        def _():
            pltpu.make_async_copy(x_ref.at[pl.ds(0, GATHER_CH)], stage.at[sl], sems.at[sl]).wait()
            o_ref[...] = stage[sl].reshape(GATHER_CH, D).astype(bf16)

    @pl.when(jnp.logical_not(live))
    def _():
        o_ref[...] = jnp.zeros((GATHER_CH, D), bf16)


def _gather_rows(nrows, row_tok3, xn2_slab, p_alloc):
    n_chunks = p_alloc // GATHER_CH
    return pl.pallas_call(
        _gather_kernel,
        grid=(n_chunks,),
        in_specs=[
            pl.BlockSpec(memory_space=pltpu.SMEM),
            pl.BlockSpec((1, 1, GATHER_CH), lambda i: (i, 0, 0), memory_space=pltpu.SMEM),
            pl.BlockSpec((1, 1, GATHER_CH), lambda i: (jnp.minimum(i + 1, n_chunks - 1), 0, 0),
                         memory_space=pltpu.SMEM),
            pl.BlockSpec(memory_space=pl.ANY),
        ],
        out_specs=pl.BlockSpec((GATHER_CH, D), lambda i: (i, 0)),
        out_shape=jax.ShapeDtypeStruct((p_alloc, D), bf16),
        scratch_shapes=[pltpu.VMEM((2, GATHER_CH, SLAB, LANES), f32), pltpu.SemaphoreType.DMA((2,))],
        compiler_params=_cparams(("arbitrary",)),
        name="moe_gather",
    )(nrows, row_tok3, row_tok3, xn2_slab)


MOE_RB = 8
MOE_TF = 512
MOE_NF = D_EXPERT // MOE_TF
MOE_VMEM = 60 * 1024 * 1024


def _moe_kernel(ge_ref, gs_ref, gn_ref, gz_ref, x_hbm, wg_ref, wl_ref, bg_ref, bl_ref, wd_ref, bd_ref, y_hbm,
                xbuf, yacc, yst, sem_in, sem_out):
    s = pl.program_id(0)
    f = pl.program_id(1)
    nb = gn_ref[s]
    nz = gz_ref[s]
    blk0 = gs_ref[s]

    def x_copy(r):
        return pltpu.make_async_copy(x_hbm.at[pl.ds((blk0 + r) * MOE_TM, MOE_TM)], xbuf.at[r], sem_in)

    def y_copy(r, slot):
        return pltpu.make_async_copy(yst.at[slot], y_hbm.at[pl.ds((blk0 + r) * MOE_TM, MOE_TM)],
                                     sem_out.at[slot])

    @pl.when((f == 0) & (nz > 0))
    def _():
        yst[0] = jnp.zeros((MOE_TM, SLAB, LANES), f32)
        for r in range(MOE_RB):
            @pl.when(r < nz)
            def _():
                y_copy(r, 0).start()
        for r in range(MOE_RB):
            @pl.when(r < nz)
            def _():
                y_copy(r, 0).wait()

    @pl.when(f == 0)
    def _():
        for r in range(MOE_RB):
            @pl.when(r < nb)
            def _():
                x_copy(r).start()

        def init(r, _):
            yacc[r] = jnp.broadcast_to(bd_ref[...], (MOE_TM, D))
            return 0

        lax.fori_loop(0, nb, init, 0)
        for r in range(MOE_RB):
            @pl.when(r < nb)
            def _():
                x_copy(r).wait()

    def up(r):
        x = xbuf[r]
        glu = _dot(x, wg_ref[...]) + bg_ref[...]
        lin = _dot(x, wl_ref[...]) + bl_ref[...]
        glu = jnp.minimum(glu, SWIGLU_LIMIT)
        lin = jnp.clip(lin, -SWIGLU_LIMIT, SWIGLU_LIMIT)
        return (glu * jax.nn.sigmoid(SWIGLU_ALPHA * glu) * (lin + 1.0)).astype(bf16)

    def down(r, act):
        yacc[r] = yacc[r] + _dot(act, wd_ref[...])

    @pl.when(nb > 0)
    def _():
        def body(r, act_prev):
            down(r - 1, act_prev)
            return up(r)

        act_last = lax.fori_loop(1, nb, body, up(0))
        down(nb - 1, act_last)

    @pl.when((f == MOE_NF - 1) & (nb > 0))
    def _():
        def emit(r, _):
            slot = r % 2

            @pl.when(r >= 2)
            def _():
                y_copy(r - 2, slot).wait()

            yst[slot] = yacc[r].reshape(MOE_TM, SLAB, LANES)
            y_copy(r, slot).start()
            return 0

        lax.fori_loop(0, nb, emit, 0)

        @pl.when(nb >= 2)
        def _():
            y_copy(nb - 2, nb % 2).wait()

        y_copy(nb - 1, (nb - 1) % 2).wait()


def _moe(g_expert, g_start, g_nblk, g_nzero, x_rows, w_gu, b_gu3, w_down, b_down3, n_groups):
    p_alloc = x_rows.shape[0]

    def f_eff(s, f, gn):
        return jnp.where(gn[s] > 0, f, MOE_NF - 1)

    grid_spec = pltpu.PrefetchScalarGridSpec(
        num_scalar_prefetch=4,
        grid=(n_groups, MOE_NF),
        in_specs=[
            pl.BlockSpec(memory_space=pl.ANY),
            pl.BlockSpec((None, D, MOE_TF), lambda s, f, ge, gs, gn, gz: (ge[s], 0, f_eff(s, f, gn))),
            pl.BlockSpec((None, D, MOE_TF), lambda s, f, ge, gs, gn, gz: (ge[s], 0, MOE_NF + f_eff(s, f, gn))),
            pl.BlockSpec((None, 1, MOE_TF), lambda s, f, ge, gs, gn, gz: (ge[s], 0, f_eff(s, f, gn))),
            pl.BlockSpec((None, 1, MOE_TF), lambda s, f, ge, gs, gn, gz: (ge[s], 0, MOE_NF + f_eff(s, f, gn))),
            pl.BlockSpec((None, MOE_TF, D), lambda s, f, ge, gs, gn, gz: (ge[s], f_eff(s, f, gn), 0)),
            pl.BlockSpec((None, 1, D), lambda s, f, ge, gs, gn, gz: (ge[s], 0, 0)),
        ],
        out_specs=pl.BlockSpec(memory_space=pl.ANY),
        scratch_shapes=[
            pltpu.VMEM((MOE_RB, MOE_TM, D), bf16),
            pltpu.VMEM((MOE_RB, MOE_TM, D), f32),
            pltpu.VMEM((2, MOE_TM, SLAB, LANES), f32),
            pltpu.SemaphoreType.DMA(()),
            pltpu.SemaphoreType.DMA((2,)),
        ],
    )
    return pl.pallas_call(
        _moe_kernel,
        grid_spec=grid_spec,
        out_shape=jax.ShapeDtypeStruct((p_alloc, SLAB, LANES), f32),
        compiler_params=_cparams(("arbitrary", "arbitrary"), MOE_VMEM),
        name="moe_experts",
    )(g_expert, g_start, g_nblk, g_nzero, x_rows, w_gu, w_gu, b_gu3, b_gu3, w_down, b_down3)


CB_TM = 64


def _combine_kernel(dest_ref, dest_next_ref, y_hbm, h1_ref, route_ref, mod_ref, g_ref, o_ref, ybuf, sems):
    i = pl.program_id(0)
    n = pl.num_programs(0)
    slot = i % 2

    def copies(idx_ref, sl):
        return [pltpu.make_async_copy(y_hbm.at[idx_ref[0, 0, t * TOP_K + k]], ybuf.at[sl, k, t], sems.at[sl])
                for t in range(CB_TM) for k in range(TOP_K)]

    def start(idx_ref, sl):
        for j, c in enumerate(copies(idx_ref, sl)):
            c.start(priority=j % 2)

    @pl.when(i == 0)
    def _():
        start(dest_ref, 0)

    for sl in range(2):
        @pl.when((i + 1 < n) & (slot == 1 - sl))
        def _():
            start(dest_next_ref, sl)

    route = route_ref[...]
    gt2 = mod_ref[:, 5 * D:6 * D]
    for sl in range(2):
        @pl.when(slot == sl)
        def _():
            for k in range(TOP_K):
                pltpu.make_async_copy(y_hbm.at[pl.ds(0, CB_TM)], ybuf.at[sl, k], sems.at[sl]).wait()
            moe = jnp.zeros((CB_TM, D), f32)
            for k in range(TOP_K):
                moe = moe + route[:, R_GATE + k:R_GATE + k + 1] * ybuf[sl, k].reshape(CB_TM, D)
            h = h1_ref[...] + gt2 * moe
            ms = jnp.mean(h * h, axis=-1, keepdims=True)
            o_ref[...] = h * lax.rsqrt(ms + RMS_EPS) * g_ref[...]


def _combine(dest3, y_rows, h1, route, mod3, g_final, seq):
    n = h1.shape[0]
    tiles_per_batch = seq // CB_TM
    n_tiles = n // CB_TM
    return pl.pallas_call(
        _combine_kernel,
        grid=(n_tiles,),
        in_specs=[
            pl.BlockSpec((1, 1, CB_TM * TOP_K), lambda i: (i, 0, 0), memory_space=pltpu.SMEM),
            pl.BlockSpec((1, 1, CB_TM * TOP_K), lambda i: (jnp.minimum(i + 1, n_tiles - 1), 0, 0),
                         memory_space=pltpu.SMEM),
            pl.BlockSpec(memory_space=pl.ANY),
            pl.BlockSpec((CB_TM, D), lambda i: (i, 0)),
            pl.BlockSpec((CB_TM, ROUTE_W), lambda i: (i, 0)),
            pl.BlockSpec((None, 1, 6 * D), lambda i: (i // tiles_per_batch, 0, 0)),
            pl.BlockSpec((1, D), lambda i: (0, 0)),
        ],
        out_specs=pl.BlockSpec((CB_TM, D), lambda i: (i, 0)),
        out_shape=jax.ShapeDtypeStruct((n, D), f32),
        scratch_shapes=[pltpu.VMEM((2, TOP_K, CB_TM, SLAB, LANES), f32), pltpu.SemaphoreType.DMA((2,))],
        compiler_params=_cparams(("arbitrary",)),
        name="moe_combine",
    )(dest3, dest3, y_rows, h1, route, mod3, g_final)


def _layer(h, c_act_pad, positions, w_ada, b_ada, g_mix, w_in, b_in, sinks, w_branch_a, w_branch_b, w_out,
           g_ffn, w_router, b_router, w_gu, b_gu, w_down, b_down):
    batch, seq, _ = h.shape
    n = batch * seq
    x2 = h.reshape(n, D)

    mod = _adaln(c_act_pad, w_ada, b_ada.reshape(1, 6 * D))
    mod3 = mod[:batch].reshape(batch, 1, 6 * D)

    w_cat = jnp.concatenate([w_in[:, GATE_OFF:], w_in[:, :QKV_W]], axis=1).astype(bf16)
    b_cat = jnp.concatenate([b_in[GATE_OFF:], b_in[:QKV_W]]).reshape(1, PROJ_W)
    w_f = w_in[:, FGATE_OFF:GATE_OFF].astype(bf16)
    b_f = b_in[FGATE_OFF:GATE_OFF].reshape(1, N_FGATE)

    inv_freq = 1.0 / (ROPE_THETA ** (jnp.arange(0, HD, 2, dtype=f32) / HD))
    ang = positions.astype(f32).reshape(n, 1) * inv_freq[None, :]
    cos_t = jnp.tile(jnp.cos(ang), (1, LANES // (HD // 2)))
    sin_h = jnp.sin(ang)
    sin_t = jnp.tile(jnp.concatenate([-sin_h, sin_h], axis=1), (1, LANES // HD))

    proj, logf = _inproj(x2, mod3, g_mix.reshape(1, D), w_cat, b_cat, w_f, b_f, cos_t, sin_t, seq)
    cum = _cumsum(logf.reshape(batch, seq, N_FGATE))

    ya = _swa(sinks, proj, batch, seq)
    yb = _fox(proj, cum, batch, seq)

    wr_hi = w_router.astype(bf16)
    wr_lo = (w_router - wr_hi.astype(f32)).astype(bf16)
    h1, xn2, route = _merge(ya, yb, proj, x2, mod3, w_branch_a.astype(bf16), w_branch_b.astype(bf16),
                            w_out.astype(bf16), g_ffn.reshape(1, D), wr_hi, wr_lo,
                            b_router.reshape(1, N_EXPERTS), seq)

    a_total = n * TOP_K
    top_idx = route[:, R_IDX:R_IDX + TOP_K].astype(i32)
    rank = route[:, R_RANK:R_RANK + TOP_K].astype(i32)
    counts = jnp.sum((top_idx[..., None] == jnp.arange(N_EXPERTS, dtype=i32)).astype(i32), axis=(0, 1))
    nblk_e = (counts + MOE_TM - 1) // MOE_TM
    blk_end = jnp.cumsum(nblk_e)
    blk_start = blk_end - nblk_e
    dest = (blk_start[top_idx] * MOE_TM + rank).reshape(a_total)
    p_alloc = a_total + N_EXPERTS * MOE_TM
    tok = jnp.arange(a_total, dtype=i32) // TOP_K
    row_tok = jnp.zeros((p_alloc,), i32).at[dest].set(tok, unique_indices=True, mode="promise_in_bounds")

    n_groups = (a_total // MOE_TM + N_EXPERTS + N_EXPERTS * (MOE_RB - 1)) // MOE_RB
    ng_e = (nblk_e + MOE_RB - 1) // MOE_RB
    g_end = jnp.cumsum(ng_e)
    gid = jnp.arange(n_groups, dtype=i32)
    g_exp_raw = jnp.searchsorted(g_end, gid, side="right").astype(i32)
    active = gid < g_end[-1]
    last_e = jnp.max(jnp.where(counts > 0, jnp.arange(N_EXPERTS, dtype=i32), 0))
    g_expert = jnp.where(active, jnp.minimum(g_exp_raw, N_EXPERTS - 1), last_e).astype(i32)
    g_local = gid - (g_end - ng_e)[g_expert]
    g_nblk = jnp.where(active, jnp.minimum(nblk_e[g_expert] - g_local * MOE_RB, MOE_RB), 0).astype(i32)
    tail0 = blk_end[-1] + (gid - g_end[-1]) * MOE_RB
    g_nzero = jnp.where(active, 0, jnp.clip(p_alloc // MOE_TM - tail0, 0, MOE_RB)).astype(i32)
    g_start = jnp.where(active, blk_start[g_expert] + g_local * MOE_RB,
                        jnp.minimum(tail0, p_alloc // MOE_TM)).astype(i32)

    nrows = (blk_end[-1] * MOE_TM).reshape(1).astype(i32)
    x_rows = _gather_rows(nrows, row_tok.reshape(p_alloc // GATHER_CH, 1, GATHER_CH), xn2, p_alloc)
    y_rows = _moe(g_expert, g_start, g_nblk, g_nzero, x_rows, w_gu, b_gu.reshape(N_EXPERTS, 1, 2 * D_EXPERT),
                  w_down, b_down.reshape(N_EXPERTS, 1, D), n_groups)
    return dest, y_rows, h1, route, mod3


def kernel(x, c, positions, w_ada, b_ada, g_mix, w_in, b_in, sinks, w_branch_a, w_branch_b, w_out, g_ffn,
           w_router, b_router, w_gu, b_gu, w_down, b_down, g_final):
    batch, seq, _ = x.shape
    depth = w_ada.shape[0]
    assert depth == 1, "single-layer trunk"
    c_pad = jnp.zeros((8, D), f32).at[:batch].set(c)
    l = 0
    dest, y_rows, h1, route, mod3 = _layer(
        x, c_pad, positions, w_ada[l], b_ada[l], g_mix[l], w_in[l], b_in[l], sinks[l], w_branch_a[l],
        w_branch_b[l], w_out[l], g_ffn[l], w_router[l], b_router[l], w_gu[l], b_gu[l], w_down[l], b_down[l])
    n = batch * seq
    out = _combine(dest.reshape(n // CB_TM, 1, CB_TM * TOP_K), y_rows, h1, route, mod3,
                   g_final.reshape(1, D), seq)
    return out.reshape(batch, seq, D)
```

```python
import functools

import jax
import jax.numpy as jnp
from jax import lax
from jax.experimental import pallas as pl
from jax.experimental.pallas import tpu as pltpu

f32 = jnp.float32
bf16 = jnp.bfloat16
i32 = jnp.int32

D = 2048
HD = 64
SWA_QW = 1024
SWA_KVW = 256
FOX_W = 1024
N_FGATE = 16
WINDOW = 128
N_EXPERTS = 32
TOP_K = 4
D_EXPERT = 2048
SWIGLU_LIMIT = 7.0
SWIGLU_ALPHA = 1.702
RMS_EPS = 1e-5
NEG_INF = -1e30
ROPE_THETA = 10000.0

QKV_W = SWA_QW + 2 * SWA_KVW + 3 * FOX_W
FGATE_OFF = QKV_W
GATE_OFF = QKV_W + N_FGATE

LANES = 128
P_SA = 0
P_SB = D
P_QA = 2 * D
P_KA = P_QA + SWA_QW
P_VA = P_KA + SWA_KVW
P_QB = P_VA + SWA_KVW
P_KB = P_QB + FOX_W
P_VB = P_KB + FOX_W
PROJ_W = P_VB + FOX_W

VMEM_LIMIT = 56 * 1024 * 1024


def _cparams(sem, vmem=VMEM_LIMIT):
    return pltpu.CompilerParams(dimension_semantics=sem, vmem_limit_bytes=vmem)


def _dot(a, b):
    return jnp.dot(a, b, preferred_element_type=f32)


ADA_TN = 1024


def _adaln_kernel(c_ref, w_ref, b_ref, o_ref):
    c = c_ref[...]
    ca = c * jax.nn.sigmoid(c)
    o_ref[...] = _dot(ca, w_ref[...]) + b_ref[...]


def _adaln(c_pad, w_ada, b_ada):
    rows = c_pad.shape[0]
    n = w_ada.shape[1]
    return pl.pallas_call(
        _adaln_kernel,
        grid=(n // ADA_TN,),
        in_specs=[
            pl.BlockSpec((rows, D), lambda j: (0, 0)),
            pl.BlockSpec((D, ADA_TN), lambda j: (0, j)),
            pl.BlockSpec((1, ADA_TN), lambda j: (0, j)),
        ],
        out_specs=pl.BlockSpec((rows, ADA_TN), lambda j: (0, j)),
        out_shape=jax.ShapeDtypeStruct((rows, n), f32),
        compiler_params=_cparams(("arbitrary",)),
        name="adaln",
    )(c_pad, w_ada, b_ada)


IN_TM = 1024
IN_TN = 512
IN_NJ = PROJ_W // IN_TN
J_QA0 = P_QA // IN_TN
J_KVA = P_KA // IN_TN
J_QB0 = P_QB // IN_TN
J_KB0 = P_KB // IN_TN


def _rope_block(a, cos, sin_signed, first_half):
    fwd = pltpu.roll(a, LANES - HD // 2, axis=1)
    bwd = pltpu.roll(a, HD // 2, axis=1)
    partner = jnp.where(first_half, fwd, bwd)
    return a * cos + partner * sin_signed


def _inproj_kernel(x_ref, mod_ref, g_ref, w_ref, b_ref, wf_ref, bf_ref, cos_ref, sin_ref,
                   proj_ref, logf_ref, xn_ref):
    j = pl.program_id(1)

    @pl.when(j == 0)
    def _():
        x = x_ref[...]
        ms = jnp.mean(x * x, axis=-1, keepdims=True)
        y = x * lax.rsqrt(ms + RMS_EPS) * g_ref[...]
        xn = y * (1.0 + mod_ref[:, D:2 * D]) + mod_ref[:, 0:D]
        xnb = xn.astype(bf16)
        xn_ref[...] = xnb
        fl = _dot(xnb, wf_ref[...]) + bf_ref[...]
        logf_ref[...] = jnp.minimum(fl, 0.0) - jnp.log(1.0 + jnp.exp(-jnp.abs(fl)))

    acc = _dot(xn_ref[...], w_ref[...]) + b_ref[...]

    lane = lax.broadcasted_iota(i32, (IN_TM, LANES), 1)
    first_half = (lane % HD) < (HD // 2)
    scale = HD ** -0.5

    def rope_cols(n_blocks, mult):
        cos = cos_ref[...]
        sin = sin_ref[...]
        for g in range(n_blocks):
            sl = slice(g * LANES, (g + 1) * LANES)
            r = _rope_block(acc[:, sl], cos, sin, first_half)
            proj_ref[:, sl] = (r * mult).astype(bf16)

    @pl.when(j < J_QA0)
    def _():
        proj_ref[...] = jax.nn.sigmoid(acc).astype(bf16)

    @pl.when((j >= J_QA0) & (j < J_KVA))
    def _():
        rope_cols(IN_TN // LANES, scale)

    @pl.when(j == J_KVA)
    def _():
        rope_cols(SWA_KVW // LANES, 1.0)
        proj_ref[:, SWA_KVW:] = acc[:, SWA_KVW:].astype(bf16)

    @pl.when((j >= J_QB0) & (j < J_KB0))
    def _():
        proj_ref[...] = (acc * scale).astype(bf16)

    @pl.when(j >= J_KB0)
    def _():
        proj_ref[...] = acc.astype(bf16)


def _inproj(x2, mod3, g_mix, w_cat, b_cat, w_f, b_f, cos_t, sin_t, seq):
    n = x2.shape[0]
    tiles_per_batch = seq // IN_TM
    return pl.pallas_call(
        _inproj_kernel,
        grid=(n // IN_TM, IN_NJ),
        in_specs=[
            pl.BlockSpec((IN_TM, D), lambda i, j: (i, 0)),
            pl.BlockSpec((None, 1, 6 * D), lambda i, j: (i // tiles_per_batch, 0, 0)),
            pl.BlockSpec((1, D), lambda i, j: (0, 0)),
            pl.BlockSpec((D, IN_TN), lambda i, j: (0, j)),
            pl.BlockSpec((1, IN_TN), lambda i, j: (0, j)),
            pl.BlockSpec((D, N_FGATE), lambda i, j: (0, 0)),
            pl.BlockSpec((1, N_FGATE), lambda i, j: (0, 0)),
            pl.BlockSpec((IN_TM, LANES), lambda i, j: (i, 0)),
            pl.BlockSpec((IN_TM, LANES), lambda i, j: (i, 0)),
        ],
        out_specs=[
            pl.BlockSpec((IN_TM, IN_TN), lambda i, j: (i, j)),
            pl.BlockSpec((IN_TM, N_FGATE), lambda i, j: (i, 0)),
        ],
        out_shape=[
            jax.ShapeDtypeStruct((n, PROJ_W), bf16),
            jax.ShapeDtypeStruct((n, N_FGATE), f32),
        ],
        scratch_shapes=[pltpu.VMEM((IN_TM, D), bf16)],
        compiler_params=_cparams(("arbitrary", "arbitrary")),
        name="inproj",
    )(x2, mod3, g_mix, w_cat, b_cat, w_f, b_f, cos_t, sin_t)


CUM_CH = 256


def _split3(x):
    x1 = x.astype(bf16)
    r1 = x - x1.astype(f32)
    x2 = r1.astype(bf16)
    r2 = r1 - x2.astype(f32)
    return x1, x2, r2.astype(bf16)


def _cumsum_kernel(x_ref, o_ref):
    seq = x_ref.shape[0]
    row = lax.broadcasted_iota(i32, (CUM_CH, CUM_CH), 0)
    col = lax.broadcasted_iota(i32, (CUM_CH, CUM_CH), 1)
    tri = jnp.where(col <= row, 1.0, 0.0).astype(bf16)
    carry = jnp.zeros((1, N_FGATE), f32)
    for c in range(seq // CUM_CH):
        sl = slice(c * CUM_CH, (c + 1) * CUM_CH)
        x1, x2, x3 = _split3(x_ref[sl, :])
        cs = _dot(tri, x1) + _dot(tri, x2) + _dot(tri, x3) + carry
        o_ref[sl, :] = cs
        carry = cs[CUM_CH - 1:CUM_CH, :]


def _cumsum(logf3):
    b, seq, h = logf3.shape
    return pl.pallas_call(
        _cumsum_kernel,
        grid=(b,),
        in_specs=[pl.BlockSpec((None, seq, h), lambda i: (i, 0, 0))],
        out_specs=pl.BlockSpec((None, seq, h), lambda i: (i, 0, 0)),
        out_shape=jax.ShapeDtypeStruct((b, seq, h), f32),
        compiler_params=_cparams(("arbitrary",)),
        name="fgate_cumsum",
    )(logf3)


SWA_QCOLS = 512


def _swa_kernel(sinks_ref, q_ref, k_ref, v_ref, o_ref, vt_ref):
    jp = pl.program_id(1)
    seq = q_ref.shape[0]
    nblk = seq // WINDOW

    for c in range(nblk):
        vblk = v_ref[c * WINDOW:(c + 1) * WINDOW, :].astype(f32)
        vt_ref[c] = vblk.T.astype(bf16)

    kj = lax.broadcasted_iota(i32, (2 * WINDOW, WINDOW), 0)
    qi = lax.broadcasted_iota(i32, (2 * WINDOW, WINDOW), 1)
    delta = qi + WINDOW - kj
    band = (delta >= 0) & (delta < WINDOW)
    zeros_half = jnp.zeros((HD, WINDOW), bf16)

    def body(n, _):
        prev = jnp.maximum(n - 1, 0)
        r0 = pl.multiple_of(n * WINDOW, WINDOW)
        p0 = pl.multiple_of(prev * WINDOW, WINDOW)
        kwin = jnp.concatenate([k_ref[pl.ds(p0, WINDOW), :], k_ref[pl.ds(r0, WINDOW), :]], axis=0)
        valid = band & ((n - 1) * WINDOW + kj >= 0)
        qblk = q_ref[pl.ds(r0, WINDOW), :]
        for t in range(SWA_QCOLS // LANES):
            g = t // 2
            qt = qblk[:, t * LANES:(t + 1) * LANES].astype(f32).T
            v_prev = vt_ref[prev, g * HD:(g + 1) * HD, :]
            v_cur = vt_ref[n, g * HD:(g + 1) * HD, :]
            outs = []
            for u in range(2):
                sink = sinks_ref[jp * 8 + t * 2 + u]
                feat = qt[u * HD:(u + 1) * HD, :].astype(bf16)
                wq = jnp.concatenate([feat, zeros_half] if g == 0 else [zeros_half, feat], axis=0)
                st = _dot(kwin, wq)
                st = jnp.where(valid, st, NEG_INF)
                m = jnp.maximum(jnp.max(st, axis=0, keepdims=True), sink)
                pt = jnp.exp(st - m)
                denom = jnp.sum(pt, axis=0, keepdims=True) + jnp.exp(sink - m)
                ptb = pt.astype(bf16)
                ot = _dot(v_prev, ptb[:WINDOW, :]) + _dot(v_cur, ptb[WINDOW:, :])
                outs.append(ot / denom)
            o_ref[pl.ds(r0, WINDOW), t * LANES:(t + 1) * LANES] = (
                jnp.concatenate(outs, axis=0).T.astype(bf16))
        return 0

    lax.fori_loop(0, nblk, body, 0)


def _swa(sinks, proj, batch, seq):
    n = proj.shape[0]
    qb0 = P_QA // SWA_QCOLS
    kb0 = P_KA // LANES
    vb0 = P_VA // LANES
    return pl.pallas_call(
        _swa_kernel,
        grid=(batch, SWA_QW // SWA_QCOLS),
        in_specs=[
            pl.BlockSpec(memory_space=pltpu.SMEM),
            pl.BlockSpec((seq, SWA_QCOLS), lambda b, j: (b, qb0 + j)),
            pl.BlockSpec((seq, LANES), lambda b, j: (b, kb0 + j)),
            pl.BlockSpec((seq, LANES), lambda b, j: (b, vb0 + j)),
        ],
        out_specs=pl.BlockSpec((seq, SWA_QCOLS), lambda b, j: (b, j)),
        out_shape=jax.ShapeDtypeStruct((n, SWA_QW), bf16),
        scratch_shapes=[pltpu.VMEM((seq // WINDOW, LANES, WINDOW), bf16)],
        compiler_params=_cparams(("arbitrary", "arbitrary")),
        name="swa_attention",
    )(sinks, proj, proj, proj)


FOX_T = 256
LOG2E = 1.4426950408889634


def _fox_kernel(q_ref, k_ref, v_ref, cum_ref, o_ref, vt_ref, kb_ref):
    p = pl.program_id(1)
    seq = q_ref.shape[0]
    nblk = seq // FOX_T

    for c in range(nblk):
        vblk = v_ref[c * FOX_T:(c + 1) * FOX_T, :].astype(f32)
        vt_ref[c] = vblk.T.astype(bf16)

    cum = cum_ref[...]
    hl = lax.broadcasted_iota(i32, cum.shape, 1)
    for u in range(2):
        col = jnp.sum(jnp.where(hl == 2 * p + u, cum, 0.0), axis=1, keepdims=True)
        kb_ref[u] = jnp.broadcast_to(col * LOG2E, (seq, LANES))

    frow = lax.broadcasted_iota(i32, (LANES, FOX_T), 0)
    kidx = lax.broadcasted_iota(i32, (FOX_T, FOX_T), 0)
    qidx = lax.broadcasted_iota(i32, (FOX_T, FOX_T), 1)
    causal = kidx <= qidx

    def scores(u, wq, k0):
        kblk = k_ref[pl.ds(k0, FOX_T), :]
        kb = kb_ref[u, pl.ds(k0, FOX_T), :]
        return _dot(kblk, wq) - jnp.concatenate([kb] * (FOX_T // LANES), axis=1)

    def update(u, st, jk, state):
        m, l, acc = state
        m_new = jnp.maximum(m, jnp.max(st, axis=0, keepdims=True))
        alpha = jnp.exp2(m - m_new)
        pt = jnp.exp2(st - m_new)
        l_new = alpha * l + jnp.sum(pt, axis=0, keepdims=True)
        vt = vt_ref[jk, u * HD:(u + 1) * HD, :]
        acc_new = alpha * acc + _dot(vt, pt.astype(bf16))
        return m_new, l_new, acc_new

    def qblock(iq, _):
        q0 = pl.multiple_of(iq * FOX_T, FOX_T)
        qt = q_ref[pl.ds(q0, FOX_T), :].astype(f32).T
        qt = qt * LOG2E
        wqs = [jnp.where((frow // HD) == u, qt, 0.0).astype(bf16) for u in range(2)]
        init = tuple((jnp.full((1, FOX_T), NEG_INF, f32), jnp.zeros((1, FOX_T), f32),
                      jnp.zeros((HD, FOX_T), f32)) for _ in range(2))

        def kblock(jk, carry):
            sts, states = carry
            k1 = pl.multiple_of((jk + 1) * FOX_T, FOX_T)
            nxt = tuple(scores(u, wqs[u], k1) for u in range(2))
            return nxt, tuple(update(u, sts[u], jk, states[u]) for u in range(2))

        first = tuple(scores(u, wqs[u], 0) for u in range(2))
        sts, states = lax.fori_loop(0, iq, kblock, (first, init))
        outs = []
        for u in range(2):
            _, l, acc = update(u, jnp.where(causal, sts[u], NEG_INF), iq, states[u])
            outs.append(acc / l)
        o_ref[pl.ds(q0, FOX_T), :] = jnp.concatenate(outs, axis=0).T.astype(bf16)
        return 0

    lax.fori_loop(0, nblk, qblock, 0)


def _fox(proj, cum, batch, seq):
    n = proj.shape[0]
    qb0 = P_QB // LANES
    kb0 = P_KB // LANES
    vb0 = P_VB // LANES
    return pl.pallas_call(
        _fox_kernel,
        grid=(batch, FOX_W // LANES),
        in_specs=[
            pl.BlockSpec((seq, LANES), lambda b, p: (b, qb0 + p)),
            pl.BlockSpec((seq, LANES), lambda b, p: (b, kb0 + p)),
            pl.BlockSpec((seq, LANES), lambda b, p: (b, vb0 + p)),
            pl.BlockSpec((None, seq, N_FGATE), lambda b, p: (b, 0, 0)),
        ],
        out_specs=pl.BlockSpec((seq, LANES), lambda b, p: (b, p)),
        out_shape=jax.ShapeDtypeStruct((n, FOX_W), bf16),
        scratch_shapes=[pltpu.VMEM((seq // FOX_T, LANES, FOX_T), bf16),
                        pltpu.VMEM((2, seq, LANES), f32)],
        compiler_params=_cparams(("arbitrary", "arbitrary")),
        name="fox_attention",
    )(proj, proj, proj, cum)


MG_TM = 256
MG_TN = 512
ROUTE_W = LANES
R_IDX, R_GATE, R_RANK = 0, TOP_K, 2 * TOP_K


def _merge_kernel(ya_ref, yb_ref, sa_ref, sb_ref, x_ref, mod_ref, wa_ref, wb_ref, wo_ref, gf_ref,
                  wrh_ref, wrl_ref, br_ref, h1_ref, xn2_ref, route_ref, merged_ref, cnt_ref):
    i = pl.program_id(0)

    @pl.when(i == 0)
    def _():
        cnt_ref[...] = jnp.zeros_like(cnt_ref)

    ya = ya_ref[...]
    yb = yb_ref[...]
    for c in range(D // MG_TN):
        sl = slice(c * MG_TN, (c + 1) * MG_TN)
        ta = _dot(ya, wa_ref[:, sl])
        tb = _dot(yb, wb_ref[:, sl])
        merged_ref[:, sl] = (sa_ref[:, sl].astype(f32) * ta + sb_ref[:, sl].astype(f32) * tb).astype(bf16)

    merged = merged_ref[...]
    ssq = jnp.zeros((MG_TM, 1), f32)
    for c in range(D // MG_TN):
        sl = slice(c * MG_TN, (c + 1) * MG_TN)
        o = _dot(merged, wo_ref[:, sl])
        h = x_ref[:, sl] + mod_ref[:, 2 * D + c * MG_TN:2 * D + (c + 1) * MG_TN] * o
        h1_ref[:, sl] = h
        ssq = ssq + jnp.sum(h * h, axis=1, keepdims=True)

    inv = lax.rsqrt(ssq * (1.0 / D) + RMS_EPS)
    xn2 = h1_ref[...] * inv * gf_ref[...] * (1.0 + mod_ref[:, 4 * D:5 * D]) + mod_ref[:, 3 * D:4 * D]
    xn2_ref[...] = xn2.reshape(MG_TM, D // LANES, LANES)
    hi = xn2.astype(bf16)
    lo = (xn2 - hi.astype(f32)).astype(bf16)
    wrh = wrh_ref[...]
    logits = _dot(hi, wrh) + _dot(lo, wrh) + _dot(hi, wrl_ref[...]) + br_ref[...]

    elane = lax.broadcasted_iota(i32, (MG_TM, N_EXPERTS), 1)
    work = logits
    onehots, vals, idxs = [], [], []
    for _ in range(TOP_K):
        mk = jnp.max(work, axis=1, keepdims=True)
        idx = jnp.min(jnp.where(work == mk, elane, N_EXPERTS), axis=1, keepdims=True)
        one = elane == idx
        onehots.append(one)
        vals.append(mk)
        idxs.append(idx)
        work = jnp.where(one, -jnp.inf, work)
    exps = [jnp.exp(v - vals[0]) for v in vals]
    denom = exps[0] + exps[1] + exps[2] + exps[3]

    sel = jnp.where(onehots[0] | onehots[1] | onehots[2] | onehots[3], 1.0, 0.0)
    trow = lax.broadcasted_iota(i32, (MG_TM, MG_TM), 0)
    tcol = lax.broadcasted_iota(i32, (MG_TM, MG_TM), 1)
    strict = jnp.where(tcol < trow, 1.0, 0.0).astype(bf16)
    rank = _dot(strict, sel.astype(bf16)) + cnt_ref[...]
    cnt_ref[...] = cnt_ref[...] + jnp.sum(sel, axis=0, keepdims=True)

    rlane = lax.broadcasted_iota(i32, (MG_TM, ROUTE_W), 1)
    rec = jnp.zeros((MG_TM, ROUTE_W), f32)
    for k in range(TOP_K):
        rk = jnp.sum(jnp.where(onehots[k], rank, 0.0), axis=1, keepdims=True)
        rec = jnp.where(rlane == R_IDX + k, idxs[k].astype(f32), rec)
        rec = jnp.where(rlane == R_GATE + k, exps[k] / denom, rec)
        rec = jnp.where(rlane == R_RANK + k, rk, rec)
    route_ref[...] = rec


def _merge(ya, yb, proj, x2, mod3, wa, wb, wo, g_ffn, wr_hi, wr_lo, b_r, seq):
    n = x2.shape[0]
    tiles_per_batch = seq // MG_TM
    const = lambda i: (0, 0)
    return pl.pallas_call(
        _merge_kernel,
        grid=(n // MG_TM,),
        in_specs=[
            pl.BlockSpec((MG_TM, SWA_QW), lambda i: (i, 0)),
            pl.BlockSpec((MG_TM, FOX_W), lambda i: (i, 0)),
            pl.BlockSpec((MG_TM, D), lambda i: (i, P_SA // D)),
            pl.BlockSpec((MG_TM, D), lambda i: (i, P_SB // D)),
            pl.BlockSpec((MG_TM, D), lambda i: (i, 0)),
            pl.BlockSpec((None, 1, 6 * D), lambda i: (i // tiles_per_batch, 0, 0)),
            pl.BlockSpec((SWA_QW, D), const),
            pl.BlockSpec((FOX_W, D), const),
            pl.BlockSpec((D, D), const),
            pl.BlockSpec((1, D), const),
            pl.BlockSpec((D, N_EXPERTS), const),
            pl.BlockSpec((D, N_EXPERTS), const),
            pl.BlockSpec((1, N_EXPERTS), const),
        ],
        out_specs=[
            pl.BlockSpec((MG_TM, D), lambda i: (i, 0)),
            pl.BlockSpec((MG_TM, D // LANES, LANES), lambda i: (i, 0, 0)),
            pl.BlockSpec((MG_TM, ROUTE_W), lambda i: (i, 0)),
        ],
        out_shape=[
            jax.ShapeDtypeStruct((n, D), f32),
            jax.ShapeDtypeStruct((n, D // LANES, LANES), f32),
            jax.ShapeDtypeStruct((n, ROUTE_W), f32),
        ],
        scratch_shapes=[pltpu.VMEM((MG_TM, D), bf16), pltpu.VMEM((1, N_EXPERTS), f32)],
        compiler_params=_cparams(("arbitrary",)),
        name="merge_router",
    )(ya, yb, proj, proj, x2, mod3, wa, wb, wo, g_ffn, wr_hi, wr_lo, b_r)


MOE_TM = 256
GATHER_CH = 256


SLAB = D // LANES


def _gather_kernel(nrows_ref, tok_ref, tok_next_ref, x_ref, o_ref, stage, sems):
    i = pl.program_id(0)
    n = pl.num_programs(0)
    slot = i % 2
    live = i * GATHER_CH < nrows_ref[0]
    live_next = (i + 1 < n) & ((i + 1) * GATHER_CH < nrows_ref[0])

    def copies(idx_ref, sl):
        return [pltpu.make_async_copy(x_ref.at[idx_ref[0, 0, r]], stage.at[sl, r], sems.at[sl])
                for r in range(GATHER_CH)]

    def start(idx_ref, sl):
        for r, c in enumerate(copies(idx_ref, sl)):
            c.start(priority=r % 2)

    @pl.when((i == 0) & live)
    def _():
        start(tok_ref, 0)

    for sl in range(2):
        @pl.when(live_next & (slot == 1 - sl))
        def _():
            start(tok_next_ref, sl)

    for sl in range(2):
        @pl.when(live & (slot == sl))
        def _():
            pltpu.make_async_copy(x_ref.at[pl.ds(0, GATHER_CH)], stage.at[sl], sems.at[sl]).wait()
            o_ref[...] = stage[sl].reshape(GATHER_CH, D).astype(bf16)

    @pl.when(jnp.logical_not(live))
    def _():
        o_ref[...] = jnp.zeros((GATHER_CH, D), bf16)


def _gather_rows(nrows, row_tok3, xn2_slab, p_alloc):
    n_chunks = p_alloc // GATHER_CH
    return pl.pallas_call(
        _gather_kernel,
        grid=(n_chunks,),
        in_specs=[
            pl.BlockSpec(memory_space=pltpu.SMEM),
            pl.BlockSpec((1, 1, GATHER_CH), lambda i: (i, 0, 0), memory_space=pltpu.SMEM),
            pl.BlockSpec((1, 1, GATHER_CH), lambda i: (jnp.minimum(i + 1, n_chunks - 1), 0, 0),
                         memory_space=pltpu.SMEM),
            pl.BlockSpec(memory_space=pl.ANY),
        ],
        out_specs=pl.BlockSpec((GATHER_CH, D), lambda i: (i, 0)),
        out_shape=jax.ShapeDtypeStruct((p_alloc, D), bf16),
        scratch_shapes=[pltpu.VMEM((2, GATHER_CH, SLAB, LANES), f32), pltpu.SemaphoreType.DMA((2,))],
        compiler_params=_cparams(("arbitrary",)),
        name="moe_gather",
    )(nrows, row_tok3, row_tok3, xn2_slab)


MOE_RB = 8
MOE_TF = 512
MOE_NF = D_EXPERT // MOE_TF
MOE_VMEM = 60 * 1024 * 1024


def _moe_kernel(ge_ref, gs_ref, gn_ref, gz_ref, x_hbm, wg_ref, wl_ref, bg_ref, bl_ref, wd_ref, bd_ref, y_hbm,
                xbuf, yacc, yst, sem_in, sem_out):
    s = pl.program_id(0)
    f = pl.program_id(1)
    nb = gn_ref[s]
    nz = gz_ref[s]
    blk0 = gs_ref[s]

    def x_copy(r):
        return pltpu.make_async_copy(x_hbm.at[pl.ds((blk0 + r) * MOE_TM, MOE_TM)], xbuf.at[r], sem_in)

    def y_copy(r, slot):
        return pltpu.make_async_copy(yst.at[slot], y_hbm.at[pl.ds((blk0 + r) * MOE_TM, MOE_TM)],
                                     sem_out.at[slot])

    @pl.when((f == 0) & (nz > 0))
    def _():
        yst[0] = jnp.zeros((MOE_TM, SLAB, LANES), f32)
        for r in range(MOE_RB):
            @pl.when(r < nz)
            def _():
                y_copy(r, 0).start()
        for r in range(MOE_RB):
            @pl.when(r < nz)
            def _():
                y_copy(r, 0).wait()

    @pl.when(f == 0)
    def _():
        for r in range(MOE_RB):
            @pl.when(r < nb)
            def _():
                x_copy(r).start()

        def init(r, _):
            yacc[r] = jnp.broadcast_to(bd_ref[...], (MOE_TM, D))
            return 0

        lax.fori_loop(0, nb, init, 0)
        for r in range(MOE_RB):
            @pl.when(r < nb)
            def _():
                x_copy(r).wait()

    def up(r):
        x = xbuf[r]
        glu = _dot(x, wg_ref[...]) + bg_ref[...]
        lin = _dot(x, wl_ref[...]) + bl_ref[...]
        glu = jnp.minimum(glu, SWIGLU_LIMIT)
        lin = jnp.clip(lin, -SWIGLU_LIMIT, SWIGLU_LIMIT)
        return (glu * jax.nn.sigmoid(SWIGLU_ALPHA * glu) * (lin + 1.0)).astype(bf16)

    def down(r, act):
        yacc[r] = yacc[r] + _dot(act, wd_ref[...])

    @pl.when((nb > 0) & (f < MOE_NF - 1))
    def _():
        def body(r, act_prev):
            down(r - 1, act_prev)
            return up(r)

        act_last = lax.fori_loop(1, nb, body, up(0))
        down(nb - 1, act_last)

    @pl.when((nb > 0) & (f == MOE_NF - 1))
    def _():
        def emit(r, act):
            slot = r % 2

            @pl.when(r >= 2)
            def _():
                y_copy(r - 2, slot).wait()

            yst[slot] = (yacc[r] + _dot(act, wd_ref[...])).reshape(MOE_TM, SLAB, LANES)
            y_copy(r, slot).start()

        def body(r, act_prev):
            emit(r - 1, act_prev)
            return up(r)

        act_last = lax.fori_loop(1, nb, body, up(0))
        emit(nb - 1, act_last)

        @pl.when(nb >= 2)
        def _():
            y_copy(nb - 2, nb % 2).wait()

        y_copy(nb - 1, (nb - 1) % 2).wait()


def _moe(g_expert, g_start, g_nblk, g_nzero, x_rows, w_gu, b_gu3, w_down, b_down3, n_groups):
    p_alloc = x_rows.shape[0]

    def f_eff(s, f, gn):
        return jnp.where(gn[s] > 0, f, MOE_NF - 1)

    grid_spec = pltpu.PrefetchScalarGridSpec(
        num_scalar_prefetch=4,
        grid=(n_groups, MOE_NF),
        in_specs=[
            pl.BlockSpec(memory_space=pl.ANY),
            pl.BlockSpec((None, D, MOE_TF), lambda s, f, ge, gs, gn, gz: (ge[s], 0, f_eff(s, f, gn))),
            pl.BlockSpec((None, D, MOE_TF), lambda s, f, ge, gs, gn, gz: (ge[s], 0, MOE_NF + f_eff(s, f, gn))),
            pl.BlockSpec((None, 1, MOE_TF), lambda s, f, ge, gs, gn, gz: (ge[s], 0, f_eff(s, f, gn))),
            pl.BlockSpec((None, 1, MOE_TF), lambda s, f, ge, gs, gn, gz: (ge[s], 0, MOE_NF + f_eff(s, f, gn))),
            pl.BlockSpec((None, MOE_TF, D), lambda s, f, ge, gs, gn, gz: (ge[s], f_eff(s, f, gn), 0)),
            pl.BlockSpec((None, 1, D), lambda s, f, ge, gs, gn, gz: (ge[s], 0, 0)),
        ],
        out_specs=pl.BlockSpec(memory_space=pl.ANY),
        scratch_shapes=[
            pltpu.VMEM((MOE_RB, MOE_TM, D), bf16),
            pltpu.VMEM((MOE_RB, MOE_TM, D), f32),
            pltpu.VMEM((2, MOE_TM, SLAB, LANES), f32),
            pltpu.SemaphoreType.DMA(()),
            pltpu.SemaphoreType.DMA((2,)),
        ],
    )
    return pl.pallas_call(
        _moe_kernel,
        grid_spec=grid_spec,
        out_shape=jax.ShapeDtypeStruct((p_alloc, SLAB, LANES), f32),
        compiler_params=_cparams(("arbitrary", "arbitrary"), MOE_VMEM),
        name="moe_experts",
    )(g_expert, g_start, g_nblk, g_nzero, x_rows, w_gu, w_gu, b_gu3, b_gu3, w_down, b_down3)


CB_TM = 64


def _combine_kernel(dest_ref, dest_next_ref, y_hbm, h1_ref, route_ref, mod_ref, g_ref, o_ref, ybuf, sems):
    i = pl.program_id(0)
    n = pl.num_programs(0)
    slot = i % 2

    def copies(idx_ref, sl):
        return [pltpu.make_async_copy(y_hbm.at[idx_ref[0, 0, t * TOP_K + k]], ybuf.at[sl, k, t], sems.at[sl])
                for t in range(CB_TM) for k in range(TOP_K)]

    def start(idx_ref, sl):
        for j, c in enumerate(copies(idx_ref, sl)):
            c.start(priority=j % 2)

    @pl.when(i == 0)
    def _():
        start(dest_ref, 0)

    for sl in range(2):
        @pl.when((i + 1 < n) & (slot == 1 - sl))
        def _():
            start(dest_next_ref, sl)

    route = route_ref[...]
    gt2 = mod_ref[:, 5 * D:6 * D]
    for sl in range(2):
        @pl.when(slot == sl)
        def _():
            for k in range(TOP_K):
                pltpu.make_async_copy(y_hbm.at[pl.ds(0, CB_TM)], ybuf.at[sl, k], sems.at[sl]).wait()
            moe = jnp.zeros((CB_TM, D), f32)
            for k in range(TOP_K):
                moe = moe + route[:, R_GATE + k:R_GATE + k + 1] * ybuf[sl, k].reshape(CB_TM, D)
            h = h1_ref[...] + gt2 * moe
            ms = jnp.mean(h * h, axis=-1, keepdims=True)
            o_ref[...] = h * lax.rsqrt(ms + RMS_EPS) * g_ref[...]


def _combine(dest3, y_rows, h1, route, mod3, g_final, seq):
    n = h1.shape[0]
    tiles_per_batch = seq // CB_TM
    n_tiles = n // CB_TM
    return pl.pallas_call(
        _combine_kernel,
        grid=(n_tiles,),
        in_specs=[
            pl.BlockSpec((1, 1, CB_TM * TOP_K), lambda i: (i, 0, 0), memory_space=pltpu.SMEM),
            pl.BlockSpec((1, 1, CB_TM * TOP_K), lambda i: (jnp.minimum(i + 1, n_tiles - 1), 0, 0),
                         memory_space=pltpu.SMEM),
            pl.BlockSpec(memory_space=pl.ANY),
            pl.BlockSpec((CB_TM, D), lambda i: (i, 0)),
            pl.BlockSpec((CB_TM, ROUTE_W), lambda i: (i, 0)),
            pl.BlockSpec((None, 1, 6 * D), lambda i: (i // tiles_per_batch, 0, 0)),
            pl.BlockSpec((1, D), lambda i: (0, 0)),
        ],
        out_specs=pl.BlockSpec((CB_TM, D), lambda i: (i, 0)),
        out_shape=jax.ShapeDtypeStruct((n, D), f32),
        scratch_shapes=[pltpu.VMEM((2, TOP_K, CB_TM, SLAB, LANES), f32), pltpu.SemaphoreType.DMA((2,))],
        compiler_params=_cparams(("arbitrary",)),
        name="moe_combine",
    )(dest3, dest3, y_rows, h1, route, mod3, g_final)


def _layer(h, c_act_pad, positions, w_ada, b_ada, g_mix, w_in, b_in, sinks, w_branch_a, w_branch_b, w_out,
           g_ffn, w_router, b_router, w_gu, b_gu, w_down, b_down):
    batch, seq, _ = h.shape
    n = batch * seq
    x2 = h.reshape(n, D)

    mod = _adaln(c_act_pad, w_ada, b_ada.reshape(1, 6 * D))
    mod3 = mod[:batch].reshape(batch, 1, 6 * D)

    w_cat = jnp.concatenate([w_in[:, GATE_OFF:], w_in[:, :QKV_W]], axis=1).astype(bf16)
    b_cat = jnp.concatenate([b_in[GATE_OFF:], b_in[:QKV_W]]).reshape(1, PROJ_W)
    w_f = w_in[:, FGATE_OFF:GATE_OFF].astype(bf16)
    b_f = b_in[FGATE_OFF:GATE_OFF].reshape(1, N_FGATE)

    inv_freq = 1.0 / (ROPE_THETA ** (jnp.arange(0, HD, 2, dtype=f32) / HD))
    ang = positions.astype(f32).reshape(n, 1) * inv_freq[None, :]
    cos_t = jnp.tile(jnp.cos(ang), (1, LANES // (HD // 2)))
    sin_h = jnp.sin(ang)
    sin_t = jnp.tile(jnp.concatenate([-sin_h, sin_h], axis=1), (1, LANES // HD))

    proj, logf = _inproj(x2, mod3, g_mix.reshape(1, D), w_cat, b_cat, w_f, b_f, cos_t, sin_t, seq)
    cum = _cumsum(logf.reshape(batch, seq, N_FGATE))

    ya = _swa(sinks, proj, batch, seq)
    yb = _fox(proj, cum, batch, seq)

    wr_hi = w_router.astype(bf16)
    wr_lo = (w_router - wr_hi.astype(f32)).astype(bf16)
    h1, xn2, route = _merge(ya, yb, proj, x2, mod3, w_branch_a.astype(bf16), w_branch_b.astype(bf16),
                            w_out.astype(bf16), g_ffn.reshape(1, D), wr_hi, wr_lo,
                            b_router.reshape(1, N_EXPERTS), seq)

    a_total = n * TOP_K
    top_idx = route[:, R_IDX:R_IDX + TOP_K].astype(i32)
    rank = route[:, R_RANK:R_RANK + TOP_K].astype(i32)
    counts = jnp.sum((top_idx[..., None] == jnp.arange(N_EXPERTS, dtype=i32)).astype(i32), axis=(0, 1))
    nblk_e = (counts + MOE_TM - 1) // MOE_TM
    blk_end = jnp.cumsum(nblk_e)
    blk_start = blk_end - nblk_e
    dest = (blk_start[top_idx] * MOE_TM + rank).reshape(a_total)
    p_alloc = a_total + N_EXPERTS * MOE_TM
    tok = jnp.arange(a_total, dtype=i32) // TOP_K
    row_tok = jnp.zeros((p_alloc,), i32).at[dest].set(tok, unique_indices=True, mode="promise_in_bounds")

    n_groups = (a_total // MOE_TM + N_EXPERTS + N_EXPERTS * (MOE_RB - 1)) // MOE_RB
    ng_e = (nblk_e + MOE_RB - 1) // MOE_RB
    g_end = jnp.cumsum(ng_e)
    gid = jnp.arange(n_groups, dtype=i32)
    g_exp_raw = jnp.searchsorted(g_end, gid, side="right").astype(i32)
    active = gid < g_end[-1]
    last_e = jnp.max(jnp.where(counts > 0, jnp.arange(N_EXPERTS, dtype=i32), 0))
    g_expert = jnp.where(active, jnp.minimum(g_exp_raw, N_EXPERTS - 1), last_e).astype(i32)
    g_local = gid - (g_end - ng_e)[g_expert]
    g_nblk = jnp.where(active, jnp.minimum(nblk_e[g_expert] - g_local * MOE_RB, MOE_RB), 0).astype(i32)
    tail0 = blk_end[-1] + (gid - g_end[-1]) * MOE_RB
    g_nzero = jnp.where(active, 0, jnp.clip(p_alloc // MOE_TM - tail0, 0, MOE_RB)).astype(i32)
    g_start = jnp.where(active, blk_start[g_expert] + g_local * MOE_RB,
                        jnp.minimum(tail0, p_alloc // MOE_TM)).astype(i32)

    nrows = (blk_end[-1] * MOE_TM).reshape(1).astype(i32)
    x_rows = _gather_rows(nrows, row_tok.reshape(p_alloc // GATHER_CH, 1, GATHER_CH), xn2, p_alloc)
    y_rows = _moe(g_expert, g_start, g_nblk, g_nzero, x_rows, w_gu, b_gu.reshape(N_EXPERTS, 1, 2 * D_EXPERT),
                  w_down, b_down.reshape(N_EXPERTS, 1, D), n_groups)
    return dest, y_rows, h1, route, mod3


def kernel(x, c, positions, w_ada, b_ada, g_mix, w_in, b_in, sinks, w_branch_a, w_branch_b, w_out, g_ffn,
           w_router, b_router, w_gu, b_gu, w_down, b_down, g_final):
    batch, seq, _ = x.shape
    depth = w_ada.shape[0]
    assert depth == 1, "single-layer trunk"
    c_pad = jnp.zeros((8, D), f32).at[:batch].set(c)
    l = 0
    dest, y_rows, h1, route, mod3 = _layer(
        x, c_pad, positions, w_ada[l], b_ada[l], g_mix[l], w_in[l], b_in[l], sinks[l], w_branch_a[l],
        w_branch_b[l], w_out[l], g_ffn[l], w_router[l], b_router[l], w_gu[l], b_gu[l], w_down[l], b_down[l])
    n = batch * seq
    out = _combine(dest.reshape(n // CB_TM, 1, CB_TM * TOP_K), y_rows, h1, route, mod3,
                   g_final.reshape(1, D), seq)
    return out.reshape(batch, seq, D)
```

```python
import jax
import jax.numpy as jnp
from jax import lax
from jax.experimental import pallas as pl
from jax.experimental.pallas import tpu as pltpu

f32 = jnp.float32
bf16 = jnp.bfloat16
i32 = jnp.int32

D = 2048
HD = 64
SWA_QW = 1024
SWA_KVW = 256
FOX_W = 1024
N_FGATE = 16
WINDOW = 128
N_EXPERTS = 32
TOP_K = 4
D_EXPERT = 2048
SWIGLU_LIMIT = 7.0
SWIGLU_ALPHA = 1.702
RMS_EPS = 1e-5
NEG_INF = -1e30
ROPE_THETA = 10000.0

QKV_W = SWA_QW + 2 * SWA_KVW + 3 * FOX_W
FGATE_OFF = QKV_W
GATE_OFF = QKV_W + N_FGATE

LANES = 128
SUBLANES = 8
P_SA = 0
P_SB = D
P_QA = 2 * D
P_KA = P_QA + SWA_QW
P_VA = P_KA + SWA_KVW
P_QB = P_VA + SWA_KVW
P_KB = P_QB + FOX_W
P_VB = P_KB + FOX_W
PROJ_W = P_VB + FOX_W

VMEM_LIMIT = 56 * 1024 * 1024


def _cparams(sem, vmem=VMEM_LIMIT):
    return pltpu.CompilerParams(dimension_semantics=sem, vmem_limit_bytes=vmem)


def _dot(a, b):
    return jnp.dot(a, b, preferred_element_type=f32)


ADA_TN = 1024


def _adaln_kernel(c_ref, w_ref, b_ref, o_ref):
    c = c_ref[...]
    ca = c * jax.nn.sigmoid(c)
    o_ref[...] = _dot(ca, w_ref[...]) + b_ref[...]


def _adaln(c_pad, w_ada, b_ada):
    rows = c_pad.shape[0]
    n = w_ada.shape[1]
    return pl.pallas_call(
        _adaln_kernel,
        grid=(n // ADA_TN,),
        in_specs=[
            pl.BlockSpec((rows, D), lambda j: (0, 0)),
            pl.BlockSpec((D, ADA_TN), lambda j: (0, j)),
            pl.BlockSpec((1, ADA_TN), lambda j: (0, j)),
        ],
        out_specs=pl.BlockSpec((rows, ADA_TN), lambda j: (0, j)),
        out_shape=jax.ShapeDtypeStruct((rows, n), f32),
        compiler_params=_cparams(("arbitrary",)),
        name="adaln",
    )(c_pad, w_ada, b_ada)


IN_TM = 1024
IN_TN = 512
IN_NJ = PROJ_W // IN_TN
J_QA0 = P_QA // IN_TN
J_KVA = P_KA // IN_TN
J_QB0 = P_QB // IN_TN
J_KB0 = P_KB // IN_TN


def _rope_block(a, cos, sin_signed, first_half):
    fwd = pltpu.roll(a, LANES - HD // 2, axis=1)
    bwd = pltpu.roll(a, HD // 2, axis=1)
    partner = jnp.where(first_half, fwd, bwd)
    return a * cos + partner * sin_signed


def _inproj_kernel(x_ref, mod_ref, g_ref, w_ref, b_ref, wf_ref, bf_ref, cos_ref, sin_ref,
                   proj_ref, logf_ref, xn_ref):
    j = pl.program_id(1)

    @pl.when(j == 0)
    def _():
        x = x_ref[...]
        ms = jnp.mean(x * x, axis=-1, keepdims=True)
        y = x * lax.rsqrt(ms + RMS_EPS) * g_ref[...]
        xn = y * (1.0 + mod_ref[:, D:2 * D]) + mod_ref[:, 0:D]
        xnb = xn.astype(bf16)
        xn_ref[...] = xnb
        fl = _dot(xnb, wf_ref[...]) + bf_ref[...]
        logf_ref[...] = jnp.minimum(fl, 0.0) - jnp.log(1.0 + jnp.exp(-jnp.abs(fl)))

    acc = _dot(xn_ref[...], w_ref[...]) + b_ref[...]

    lane = lax.broadcasted_iota(i32, (IN_TM, LANES), 1)
    first_half = (lane % HD) < (HD // 2)
    scale = HD ** -0.5

    def rope_cols(n_blocks, mult):
        cos = cos_ref[...]
        sin = sin_ref[...]
        for g in range(n_blocks):
            sl = slice(g * LANES, (g + 1) * LANES)
            r = _rope_block(acc[:, sl], cos, sin, first_half)
            proj_ref[:, sl] = (r * mult).astype(bf16)

    @pl.when(j < J_QA0)
    def _():
        proj_ref[...] = (0.5 * jnp.tanh(0.5 * acc) + 0.5).astype(bf16)

    @pl.when((j >= J_QA0) & (j < J_KVA))
    def _():
        rope_cols(IN_TN // LANES, scale)

    @pl.when(j == J_KVA)
    def _():
        rope_cols(SWA_KVW // LANES, 1.0)
        proj_ref[:, SWA_KVW:] = acc[:, SWA_KVW:].astype(bf16)

    @pl.when((j >= J_QB0) & (j < J_KB0))
    def _():
        proj_ref[...] = (acc * scale).astype(bf16)

    @pl.when(j >= J_KB0)
    def _():
        proj_ref[...] = acc.astype(bf16)


def _inproj(x2, mod3, g_mix, w_cat, b_cat, w_f, b_f, cos_t, sin_t, seq):
    n = x2.shape[0]
    tiles_per_batch = seq // IN_TM
    return pl.pallas_call(
        _inproj_kernel,
        grid=(n // IN_TM, IN_NJ),
        in_specs=[
            pl.BlockSpec((IN_TM, D), lambda i, j: (i, 0)),
            pl.BlockSpec((None, 1, 6 * D), lambda i, j: (i // tiles_per_batch, 0, 0)),
            pl.BlockSpec((1, D), lambda i, j: (0, 0)),
            pl.BlockSpec((D, IN_TN), lambda i, j: (0, j)),
            pl.BlockSpec((1, IN_TN), lambda i, j: (0, j)),
            pl.BlockSpec((D, N_FGATE), lambda i, j: (0, 0)),
            pl.BlockSpec((1, N_FGATE), lambda i, j: (0, 0)),
            pl.BlockSpec((IN_TM, LANES), lambda i, j: (i, 0)),
            pl.BlockSpec((IN_TM, LANES), lambda i, j: (i, 0)),
        ],
        out_specs=[
            pl.BlockSpec((IN_TM, IN_TN), lambda i, j: (i, j)),
            pl.BlockSpec((IN_TM, N_FGATE), lambda i, j: (i, 0)),
        ],
        out_shape=[
            jax.ShapeDtypeStruct((n, PROJ_W), bf16),
            jax.ShapeDtypeStruct((n, N_FGATE), f32),
        ],
        scratch_shapes=[pltpu.VMEM((IN_TM, D), bf16)],
        compiler_params=_cparams(("arbitrary", "arbitrary")),
        name="inproj",
    )(x2, mod3, g_mix, w_cat, b_cat, w_f, b_f, cos_t, sin_t)


CUM_CH = 256


def _split3(x):
    x1 = x.astype(bf16)
    r1 = x - x1.astype(f32)
    x2 = r1.astype(bf16)
    r2 = r1 - x2.astype(f32)
    return x1, x2, r2.astype(bf16)


def _cumsum_kernel(x_ref, o_ref):
    seq = x_ref.shape[0]
    row = lax.broadcasted_iota(i32, (CUM_CH, CUM_CH), 0)
    col = lax.broadcasted_iota(i32, (CUM_CH, CUM_CH), 1)
    tri = jnp.where(col <= row, 1.0, 0.0).astype(bf16)
    carry = jnp.zeros((1, N_FGATE), f32)
    for c in range(seq // CUM_CH):
        sl = slice(c * CUM_CH, (c + 1) * CUM_CH)
        x1, x2, x3 = _split3(x_ref[sl, :])
        cs = _dot(tri, x1) + _dot(tri, x2) + _dot(tri, x3) + carry
        o_ref[sl, :] = cs
        carry = cs[CUM_CH - 1:CUM_CH, :]


def _cumsum(logf3):
    b, seq, h = logf3.shape
    return pl.pallas_call(
        _cumsum_kernel,
        grid=(b,),
        in_specs=[pl.BlockSpec((None, seq, h), lambda i: (i, 0, 0))],
        out_specs=pl.BlockSpec((None, seq, h), lambda i: (i, 0, 0)),
        out_shape=jax.ShapeDtypeStruct((b, seq, h), f32),
        compiler_params=_cparams(("arbitrary",)),
        name="fgate_cumsum",
    )(logf3)


SWA_QCOLS = 512


def _swa_kernel(sinks_ref, q_ref, k_ref, v_ref, o_ref, vt_ref):
    jp = pl.program_id(1)
    seq = q_ref.shape[0]
    nblk = seq // WINDOW

    for c in range(nblk):
        vblk = v_ref[c * WINDOW:(c + 1) * WINDOW, :].astype(f32)
        vt_ref[c] = vblk.T.astype(bf16)

    kj = lax.broadcasted_iota(i32, (2 * WINDOW, WINDOW), 0)
    qi = lax.broadcasted_iota(i32, (2 * WINDOW, WINDOW), 1)
    delta = qi + WINDOW - kj
    band = (delta >= 0) & (delta < WINDOW)
    zeros_half = jnp.zeros((HD, WINDOW), bf16)

    def body(n, _):
        prev = jnp.maximum(n - 1, 0)
        r0 = pl.multiple_of(n * WINDOW, WINDOW)
        p0 = pl.multiple_of(prev * WINDOW, WINDOW)
        kwin = jnp.concatenate([k_ref[pl.ds(p0, WINDOW), :], k_ref[pl.ds(r0, WINDOW), :]], axis=0)
        valid = band & ((n - 1) * WINDOW + kj >= 0)
        qblk = q_ref[pl.ds(r0, WINDOW), :]
        for t in range(SWA_QCOLS // LANES):
            g = t // 2
            qt = qblk[:, t * LANES:(t + 1) * LANES].astype(f32).T
            v_prev = vt_ref[prev, g * HD:(g + 1) * HD, :]
            v_cur = vt_ref[n, g * HD:(g + 1) * HD, :]
            outs = []
            for u in range(2):
                sink = sinks_ref[jp * (SWA_QCOLS // HD) + t * (LANES // HD) + u]
                feat = qt[u * HD:(u + 1) * HD, :].astype(bf16)
                wq = jnp.concatenate([feat, zeros_half] if g == 0 else [zeros_half, feat], axis=0)
                st = _dot(kwin, wq)
                st = jnp.where(valid, st, NEG_INF)
                m = jnp.maximum(jnp.max(st, axis=0, keepdims=True), sink)
                pt = jnp.exp(st - m)
                denom = jnp.sum(pt, axis=0, keepdims=True) + jnp.exp(sink - m)
                ptb = pt.astype(bf16)
                ot = _dot(v_prev, ptb[:WINDOW, :]) + _dot(v_cur, ptb[WINDOW:, :])
                outs.append(ot / denom)
            o_ref[pl.ds(r0, WINDOW), t * LANES:(t + 1) * LANES] = (
                jnp.concatenate(outs, axis=0).T.astype(bf16))
        return 0

    lax.fori_loop(0, nblk, body, 0)


def _swa(sinks, proj, batch, seq):
    n = proj.shape[0]
    qb0 = P_QA // SWA_QCOLS
    kb0 = P_KA // LANES
    vb0 = P_VA // LANES
    return pl.pallas_call(
        _swa_kernel,
        grid=(batch, SWA_QW // SWA_QCOLS),
        in_specs=[
            pl.BlockSpec(memory_space=pltpu.SMEM),
            pl.BlockSpec((seq, SWA_QCOLS), lambda b, j: (b, qb0 + j)),
            pl.BlockSpec((seq, LANES), lambda b, j: (b, kb0 + j)),
            pl.BlockSpec((seq, LANES), lambda b, j: (b, vb0 + j)),
        ],
        out_specs=pl.BlockSpec((seq, SWA_QCOLS), lambda b, j: (b, j)),
        out_shape=jax.ShapeDtypeStruct((n, SWA_QW), bf16),
        scratch_shapes=[pltpu.VMEM((seq // WINDOW, LANES, WINDOW), bf16)],
        compiler_params=_cparams(("arbitrary", "arbitrary")),
        name="swa_attention",
    )(sinks, proj, proj, proj)


FOX_T = 256
LOG2E = 1.4426950408889634


def _fox_kernel(q_ref, k_ref, v_ref, cum_ref, o_ref, vt_ref, kb_ref):
    p = pl.program_id(1)
    seq = q_ref.shape[0]
    nblk = seq // FOX_T

    for c in range(nblk):
        vblk = v_ref[c * FOX_T:(c + 1) * FOX_T, :].astype(f32)
        vt_ref[c] = vblk.T.astype(bf16)

    cum = cum_ref[...]
    hl = lax.broadcasted_iota(i32, cum.shape, 1)
    for u in range(2):
        col = jnp.sum(jnp.where(hl == 2 * p + u, cum, 0.0), axis=1, keepdims=True)
        kb_ref[u] = jnp.broadcast_to(col * LOG2E, (seq, LANES))

    frow = lax.broadcasted_iota(i32, (LANES, FOX_T), 0)
    kidx = lax.broadcasted_iota(i32, (FOX_T, FOX_T), 0)
    qidx = lax.broadcasted_iota(i32, (FOX_T, FOX_T), 1)
    causal = kidx <= qidx

    def scores(u, wq, k0):
        kblk = k_ref[pl.ds(k0, FOX_T), :]
        kb = kb_ref[u, pl.ds(k0, FOX_T), :]
        return _dot(kblk, wq) - jnp.concatenate([kb] * (FOX_T // LANES), axis=1)

    def update(u, st, jk, state):
        m, l, acc = state
        m_new = jnp.maximum(m, jnp.max(st, axis=0, keepdims=True))
        alpha = jnp.exp2(m - m_new)
        pt = jnp.exp2(st - m_new)
        l_new = alpha * l + jnp.sum(pt, axis=0, keepdims=True)
        vt = vt_ref[jk, u * HD:(u + 1) * HD, :]
        acc_new = alpha * acc + _dot(vt, pt.astype(bf16))
        return m_new, l_new, acc_new

    def qblock(iq, _):
        q0 = pl.multiple_of(iq * FOX_T, FOX_T)
        qt = q_ref[pl.ds(q0, FOX_T), :].astype(f32).T
        qt = qt * LOG2E
        wqs = [jnp.where((frow // HD) == u, qt, 0.0).astype(bf16) for u in range(2)]
        init = tuple((jnp.full((1, FOX_T), NEG_INF, f32), jnp.zeros((1, FOX_T), f32),
                      jnp.zeros((HD, FOX_T), f32)) for _ in range(2))

        def kblock(jk, carry):
            sts, states = carry
            k1 = pl.multiple_of((jk + 1) * FOX_T, FOX_T)
            nxt = tuple(scores(u, wqs[u], k1) for u in range(2))
            return nxt, tuple(update(u, sts[u], jk, states[u]) for u in range(2))

        first = tuple(scores(u, wqs[u], 0) for u in range(2))
        sts, states = lax.fori_loop(0, iq, kblock, (first, init))
        outs = []
        for u in range(2):
            _, l, acc = update(u, jnp.where(causal, sts[u], NEG_INF), iq, states[u])
            outs.append(acc / l)
        o_ref[pl.ds(q0, FOX_T), :] = jnp.concatenate(outs, axis=0).T.astype(bf16)
        return 0

    lax.fori_loop(0, nblk, qblock, 0)


def _fox(proj, cum, batch, seq):
    n = proj.shape[0]
    qb0 = P_QB // LANES
    kb0 = P_KB // LANES
    vb0 = P_VB // LANES
    return pl.pallas_call(
        _fox_kernel,
        grid=(batch, FOX_W // LANES),
        in_specs=[
            pl.BlockSpec((seq, LANES), lambda b, p: (b, qb0 + p)),
            pl.BlockSpec((seq, LANES), lambda b, p: (b, kb0 + p)),
            pl.BlockSpec((seq, LANES), lambda b, p: (b, vb0 + p)),
            pl.BlockSpec((None, seq, N_FGATE), lambda b, p: (b, 0, 0)),
        ],
        out_specs=pl.BlockSpec((seq, LANES), lambda b, p: (b, p)),
        out_shape=jax.ShapeDtypeStruct((n, FOX_W), bf16),
        scratch_shapes=[pltpu.VMEM((seq // FOX_T, LANES, FOX_T), bf16),
                        pltpu.VMEM((2, seq, LANES), f32)],
        compiler_params=_cparams(("arbitrary", "arbitrary")),
        name="fox_attention",
    )(proj, proj, proj, cum)


MG_TM = 256
MG_TN = 512
ROUTE_W = LANES
R_IDX, R_GATE, R_RANK = 0, TOP_K, 2 * TOP_K


def _merge_kernel(ya_ref, yb_ref, sa_ref, sb_ref, x_ref, mod_ref, wa_ref, wb_ref, wo_ref, gf_ref,
                  wrh_ref, wrl_ref, br_ref, h1_ref, xn2_ref, route_ref, merged_ref, cnt_ref):
    i = pl.program_id(0)

    @pl.when(i == 0)
    def _():
        cnt_ref[...] = jnp.zeros_like(cnt_ref)

    ya = ya_ref[...]
    yb = yb_ref[...]
    for c in range(D // MG_TN):
        sl = slice(c * MG_TN, (c + 1) * MG_TN)
        ta = _dot(ya, wa_ref[:, sl])
        tb = _dot(yb, wb_ref[:, sl])
        merged_ref[:, sl] = (sa_ref[:, sl].astype(f32) * ta + sb_ref[:, sl].astype(f32) * tb).astype(bf16)

    merged = merged_ref[...]
    ssq = jnp.zeros((MG_TM, 1), f32)
    for c in range(D // MG_TN):
        sl = slice(c * MG_TN, (c + 1) * MG_TN)
        o = _dot(merged, wo_ref[:, sl])
        h = x_ref[:, sl] + mod_ref[:, 2 * D + c * MG_TN:2 * D + (c + 1) * MG_TN] * o
        h1_ref[:, sl] = h
        ssq = ssq + jnp.sum(h * h, axis=1, keepdims=True)

    inv = lax.rsqrt(ssq * (1.0 / D) + RMS_EPS)
    xn2 = h1_ref[...] * inv * gf_ref[...] * (1.0 + mod_ref[:, 4 * D:5 * D]) + mod_ref[:, 3 * D:4 * D]
    xn2_ref[...] = xn2.reshape(MG_TM, D // LANES, LANES)
    hi = xn2.astype(bf16)
    lo = (xn2 - hi.astype(f32)).astype(bf16)
    wrh = wrh_ref[...]
    logits = _dot(hi, wrh) + _dot(lo, wrh) + _dot(hi, wrl_ref[...]) + br_ref[...]

    elane = lax.broadcasted_iota(i32, (MG_TM, N_EXPERTS), 1)
    work = logits
    onehots, vals, idxs = [], [], []
    for _ in range(TOP_K):
        mk = jnp.max(work, axis=1, keepdims=True)
        idx = jnp.min(jnp.where(work == mk, elane, N_EXPERTS), axis=1, keepdims=True)
        one = elane == idx
        onehots.append(one)
        vals.append(mk)
        idxs.append(idx)
        work = jnp.where(one, -jnp.inf, work)
    exps = [jnp.exp(v - vals[0]) for v in vals]
    denom = exps[0] + exps[1] + exps[2] + exps[3]

    sel = jnp.where(onehots[0] | onehots[1] | onehots[2] | onehots[3], 1.0, 0.0)
    trow = lax.broadcasted_iota(i32, (MG_TM, MG_TM), 0)
    tcol = lax.broadcasted_iota(i32, (MG_TM, MG_TM), 1)
    strict = jnp.where(tcol < trow, 1.0, 0.0).astype(bf16)
    rank = _dot(strict, sel.astype(bf16)) + cnt_ref[...]
    cnt_ref[...] = cnt_ref[...] + jnp.sum(sel, axis=0, keepdims=True)

    rlane = lax.broadcasted_iota(i32, (MG_TM, ROUTE_W), 1)
    rec = jnp.zeros((MG_TM, ROUTE_W), f32)
    for k in range(TOP_K):
        rk = jnp.sum(jnp.where(onehots[k], rank, 0.0), axis=1, keepdims=True)
        rec = jnp.where(rlane == R_IDX + k, idxs[k].astype(f32), rec)
        rec = jnp.where(rlane == R_GATE + k, exps[k] / denom, rec)
        rec = jnp.where(rlane == R_RANK + k, rk, rec)
    route_ref[...] = rec


def _merge(ya, yb, proj, x2, mod3, wa, wb, wo, g_ffn, wr_hi, wr_lo, b_r, seq):
    n = x2.shape[0]
    tiles_per_batch = seq // MG_TM
    const = lambda i: (0, 0)
    return pl.pallas_call(
        _merge_kernel,
        grid=(n // MG_TM,),
        in_specs=[
            pl.BlockSpec((MG_TM, SWA_QW), lambda i: (i, 0)),
            pl.BlockSpec((MG_TM, FOX_W), lambda i: (i, 0)),
            pl.BlockSpec((MG_TM, D), lambda i: (i, P_SA // D)),
            pl.BlockSpec((MG_TM, D), lambda i: (i, P_SB // D)),
            pl.BlockSpec((MG_TM, D), lambda i: (i, 0)),
            pl.BlockSpec((None, 1, 6 * D), lambda i: (i // tiles_per_batch, 0, 0)),
            pl.BlockSpec((SWA_QW, D), const),
            pl.BlockSpec((FOX_W, D), const),
            pl.BlockSpec((D, D), const),
            pl.BlockSpec((1, D), const),
            pl.BlockSpec((D, N_EXPERTS), const),
            pl.BlockSpec((D, N_EXPERTS), const),
            pl.BlockSpec((1, N_EXPERTS), const),
        ],
        out_specs=[
            pl.BlockSpec((MG_TM, D), lambda i: (i, 0)),
            pl.BlockSpec((MG_TM, D // LANES, LANES), lambda i: (i, 0, 0)),
            pl.BlockSpec((MG_TM, ROUTE_W), lambda i: (i, 0)),
        ],
        out_shape=[
            jax.ShapeDtypeStruct((n, D), f32),
            jax.ShapeDtypeStruct((n, D // LANES, LANES), f32),
            jax.ShapeDtypeStruct((n, ROUTE_W), f32),
        ],
        scratch_shapes=[pltpu.VMEM((MG_TM, D), bf16), pltpu.VMEM((1, N_EXPERTS), f32)],
        compiler_params=_cparams(("arbitrary",)),
        name="merge_router",
    )(ya, yb, proj, proj, x2, mod3, wa, wb, wo, g_ffn, wr_hi, wr_lo, b_r)


MOE_TM = 256
GATHER_CH = 256


SLAB = D // LANES


def _gather_kernel(nrows_ref, tok_ref, tok_next_ref, x_ref, o_ref, stage, sems):
    i = pl.program_id(0)
    n = pl.num_programs(0)
    slot = i % 2
    live = i * GATHER_CH < nrows_ref[0]
    live_next = (i + 1 < n) & ((i + 1) * GATHER_CH < nrows_ref[0])

    def copies(idx_ref, sl):
        return [pltpu.make_async_copy(x_ref.at[idx_ref[0, 0, r]], stage.at[sl, r], sems.at[sl])
                for r in range(GATHER_CH)]

    def start(idx_ref, sl):
        for r, c in enumerate(copies(idx_ref, sl)):
            c.start(priority=r % 2)

    @pl.when((i == 0) & live)
    def _():
        start(tok_ref, 0)

    for sl in range(2):
        @pl.when(live_next & (slot == 1 - sl))
        def _():
            start(tok_next_ref, sl)

    for sl in range(2):
        @pl.when(live & (slot == sl))
        def _():
            pltpu.make_async_copy(x_ref.at[pl.ds(0, GATHER_CH)], stage.at[sl], sems.at[sl]).wait()
            o_ref[...] = stage[sl].reshape(GATHER_CH, D).astype(bf16)

    @pl.when(jnp.logical_not(live))
    def _():
        o_ref[...] = jnp.zeros((GATHER_CH, D), bf16)


def _gather_rows(nrows, row_tok3, xn2_slab, p_alloc):
    n_chunks = p_alloc // GATHER_CH
    return pl.pallas_call(
        _gather_kernel,
        grid=(n_chunks,),
        in_specs=[
            pl.BlockSpec(memory_space=pltpu.SMEM),
            pl.BlockSpec((1, 1, GATHER_CH), lambda i: (i, 0, 0), memory_space=pltpu.SMEM),
            pl.BlockSpec((1, 1, GATHER_CH), lambda i: (jnp.minimum(i + 1, n_chunks - 1), 0, 0),
                         memory_space=pltpu.SMEM),
            pl.BlockSpec(memory_space=pl.ANY),
        ],
        out_specs=pl.BlockSpec((GATHER_CH, D), lambda i: (i, 0)),
        out_shape=jax.ShapeDtypeStruct((p_alloc, D), bf16),
        scratch_shapes=[pltpu.VMEM((2, GATHER_CH, SLAB, LANES), f32), pltpu.SemaphoreType.DMA((2,))],
        compiler_params=_cparams(("arbitrary",)),
        name="moe_gather",
    )(nrows, row_tok3, row_tok3, xn2_slab)


MOE_RB = 8
MOE_TF = 512
MOE_NF = D_EXPERT // MOE_TF
MOE_VMEM = 60 * 1024 * 1024


def _moe_kernel(ge_ref, gs_ref, gn_ref, gz_ref, x_hbm, wg_ref, wl_ref, bg_ref, bl_ref, wd_ref, bd_ref, y_hbm,
                xbuf, yacc, yst, sem_in, sem_out):
    s = pl.program_id(0)
    f = pl.program_id(1)
    nb = gn_ref[s]
    nz = gz_ref[s]
    blk0 = gs_ref[s]

    def x_copy(r):
        return pltpu.make_async_copy(x_hbm.at[pl.ds((blk0 + r) * MOE_TM, MOE_TM)], xbuf.at[r], sem_in)

    def y_copy(r, slot):
        return pltpu.make_async_copy(yst.at[slot], y_hbm.at[pl.ds((blk0 + r) * MOE_TM, MOE_TM)],
                                     sem_out.at[slot])

    @pl.when((f == 0) & (nz > 0))
    def _():
        yst[0] = jnp.zeros((MOE_TM, SLAB, LANES), f32)
        for r in range(MOE_RB):
            @pl.when(r < nz)
            def _():
                y_copy(r, 0).start()
        for r in range(MOE_RB):
            @pl.when(r < nz)
            def _():
                y_copy(r, 0).wait()

    @pl.when(f == 0)
    def _():
        for r in range(MOE_RB):
            @pl.when(r < nb)
            def _():
                x_copy(r).start()

        def init(r, _):
            yacc[r] = jnp.broadcast_to(bd_ref[...], (MOE_TM, D))
            return 0

        lax.fori_loop(0, nb, init, 0)
        for r in range(MOE_RB):
            @pl.when(r < nb)
            def _():
                x_copy(r).wait()

    def up(r):
        x = xbuf[r]
        glu = _dot(x, wg_ref[...]) + bg_ref[...]
        lin = _dot(x, wl_ref[...]) + bl_ref[...]
        glu = jnp.minimum(glu, SWIGLU_LIMIT)
        lin = jnp.clip(lin, -SWIGLU_LIMIT, SWIGLU_LIMIT)
        return (glu * jax.nn.sigmoid(SWIGLU_ALPHA * glu) * (lin + 1.0)).astype(bf16)

    def down(r, act):
        yacc[r] = yacc[r] + _dot(act, wd_ref[...])

    @pl.when((nb > 0) & (f < MOE_NF - 1))
    def _():
        def body(r, act_prev):
            down(r - 1, act_prev)
            return up(r)

        act_last = lax.fori_loop(1, nb, body, up(0))
        down(nb - 1, act_last)

    @pl.when((nb > 0) & (f == MOE_NF - 1))
    def _():
        def emit(r, act):
            slot = r % 2

            @pl.when(r >= 2)
            def _():
                y_copy(r - 2, slot).wait()

            yst[slot] = (yacc[r] + _dot(act, wd_ref[...])).reshape(MOE_TM, SLAB, LANES)
            y_copy(r, slot).start()

        def body(r, act_prev):
            emit(r - 1, act_prev)
            return up(r)

        act_last = lax.fori_loop(1, nb, body, up(0))
        emit(nb - 1, act_last)

        @pl.when(nb >= 2)
        def _():
            y_copy(nb - 2, nb % 2).wait()

        y_copy(nb - 1, (nb - 1) % 2).wait()


def _moe(g_expert, g_start, g_nblk, g_nzero, x_rows, w_gu, b_gu3, w_down, b_down3, n_groups):
    p_alloc = x_rows.shape[0]

    def f_eff(s, f, gn):
        return jnp.where(gn[s] > 0, f, MOE_NF - 1)

    grid_spec = pltpu.PrefetchScalarGridSpec(
        num_scalar_prefetch=4,
        grid=(n_groups, MOE_NF),
        in_specs=[
            pl.BlockSpec(memory_space=pl.ANY),
            pl.BlockSpec((None, D, MOE_TF), lambda s, f, ge, gs, gn, gz: (ge[s], 0, f_eff(s, f, gn))),
            pl.BlockSpec((None, D, MOE_TF), lambda s, f, ge, gs, gn, gz: (ge[s], 0, MOE_NF + f_eff(s, f, gn))),
            pl.BlockSpec((None, 1, MOE_TF), lambda s, f, ge, gs, gn, gz: (ge[s], 0, f_eff(s, f, gn))),
            pl.BlockSpec((None, 1, MOE_TF), lambda s, f, ge, gs, gn, gz: (ge[s], 0, MOE_NF + f_eff(s, f, gn))),
            pl.BlockSpec((None, MOE_TF, D), lambda s, f, ge, gs, gn, gz: (ge[s], f_eff(s, f, gn), 0)),
            pl.BlockSpec((None, 1, D), lambda s, f, ge, gs, gn, gz: (ge[s], 0, 0)),
        ],
        out_specs=pl.BlockSpec(memory_space=pl.ANY),
        scratch_shapes=[
            pltpu.VMEM((MOE_RB, MOE_TM, D), bf16),
            pltpu.VMEM((MOE_RB, MOE_TM, D), f32),
            pltpu.VMEM((2, MOE_TM, SLAB, LANES), f32),
            pltpu.SemaphoreType.DMA(()),
            pltpu.SemaphoreType.DMA((2,)),
        ],
    )
    return pl.pallas_call(
        _moe_kernel,
        grid_spec=grid_spec,
        out_shape=jax.ShapeDtypeStruct((p_alloc, SLAB, LANES), f32),
        compiler_params=_cparams(("arbitrary", "arbitrary"), MOE_VMEM),
        name="moe_experts",
    )(g_expert, g_start, g_nblk, g_nzero, x_rows, w_gu, w_gu, b_gu3, b_gu3, w_down, b_down3)


CB_TM = 64


def _combine_kernel(dest_ref, dest_next_ref, y_hbm, h1_ref, route_ref, mod_ref, g_ref, o_ref, ybuf, sems):
    i = pl.program_id(0)
    n = pl.num_programs(0)
    slot = i % 2

    def copies(idx_ref, sl):
        return [pltpu.make_async_copy(y_hbm.at[idx_ref[0, 0, t * TOP_K + k]], ybuf.at[sl, k, t], sems.at[sl])
                for t in range(CB_TM) for k in range(TOP_K)]

    def start(idx_ref, sl):
        for j, c in enumerate(copies(idx_ref, sl)):
            c.start(priority=j % 2)

    @pl.when(i == 0)
    def _():
        start(dest_ref, 0)

    for sl in range(2):
        @pl.when((i + 1 < n) & (slot == 1 - sl))
        def _():
            start(dest_next_ref, sl)

    route = route_ref[...]
    gt2 = mod_ref[:, 5 * D:6 * D]
    for sl in range(2):
        @pl.when(slot == sl)
        def _():
            for k in range(TOP_K):
                pltpu.make_async_copy(y_hbm.at[pl.ds(0, CB_TM)], ybuf.at[sl, k], sems.at[sl]).wait()
            moe = jnp.zeros((CB_TM, D), f32)
            for k in range(TOP_K):
                moe = moe + route[:, R_GATE + k:R_GATE + k + 1] * ybuf[sl, k].reshape(CB_TM, D)
            h = h1_ref[...] + gt2 * moe
            ms = jnp.mean(h * h, axis=-1, keepdims=True)
            o_ref[...] = h * lax.rsqrt(ms + RMS_EPS) * g_ref[...]


def _combine(dest3, y_rows, h1, route, mod3, g_final, seq):
    n = h1.shape[0]
    tiles_per_batch = seq // CB_TM
    n_tiles = n // CB_TM
    return pl.pallas_call(
        _combine_kernel,
        grid=(n_tiles,),
        in_specs=[
            pl.BlockSpec((1, 1, CB_TM * TOP_K), lambda i: (i, 0, 0), memory_space=pltpu.SMEM),
            pl.BlockSpec((1, 1, CB_TM * TOP_K), lambda i: (jnp.minimum(i + 1, n_tiles - 1), 0, 0),
                         memory_space=pltpu.SMEM),
            pl.BlockSpec(memory_space=pl.ANY),
            pl.BlockSpec((CB_TM, D), lambda i: (i, 0)),
            pl.BlockSpec((CB_TM, ROUTE_W), lambda i: (i, 0)),
            pl.BlockSpec((None, 1, 6 * D), lambda i: (i // tiles_per_batch, 0, 0)),
            pl.BlockSpec((1, D), lambda i: (0, 0)),
        ],
        out_specs=pl.BlockSpec((CB_TM, D), lambda i: (i, 0)),
        out_shape=jax.ShapeDtypeStruct((n, D), f32),
        scratch_shapes=[pltpu.VMEM((2, TOP_K, CB_TM, SLAB, LANES), f32), pltpu.SemaphoreType.DMA((2,))],
        compiler_params=_cparams(("arbitrary",)),
        name="moe_combine",
    )(dest3, dest3, y_rows, h1, route, mod3, g_final)


def _layer(h, c_act_pad, positions, w_ada, b_ada, g_mix, w_in, b_in, sinks, w_branch_a, w_branch_b, w_out,
           g_ffn, w_router, b_router, w_gu, b_gu, w_down, b_down):
    batch, seq, _ = h.shape
    n = batch * seq
    x2 = h.reshape(n, D)

    mod = _adaln(c_act_pad, w_ada, b_ada.reshape(1, 6 * D))
    mod3 = mod[:batch].reshape(batch, 1, 6 * D)

    w_cat = jnp.concatenate([w_in[:, GATE_OFF:], w_in[:, :QKV_W]], axis=1).astype(bf16)
    b_cat = jnp.concatenate([b_in[GATE_OFF:], b_in[:QKV_W]]).reshape(1, PROJ_W)
    w_f = w_in[:, FGATE_OFF:GATE_OFF].astype(bf16)
    b_f = b_in[FGATE_OFF:GATE_OFF].reshape(1, N_FGATE)

    inv_freq = 1.0 / (ROPE_THETA ** (jnp.arange(0, HD, 2, dtype=f32) / HD))
    ang = positions.astype(f32).reshape(n, 1) * inv_freq[None, :]
    cos_t = jnp.tile(jnp.cos(ang), (1, LANES // (HD // 2)))
    sin_h = jnp.sin(ang)
    sin_t = jnp.tile(jnp.concatenate([-sin_h, sin_h], axis=1), (1, LANES // HD))

    proj, logf = _inproj(x2, mod3, g_mix.reshape(1, D), w_cat, b_cat, w_f, b_f, cos_t, sin_t, seq)
    cum = _cumsum(logf.reshape(batch, seq, N_FGATE))

    ya = _swa(sinks, proj, batch, seq)
    yb = _fox(proj, cum, batch, seq)

    wr_hi = w_router.astype(bf16)
    wr_lo = (w_router - wr_hi.astype(f32)).astype(bf16)
    h1, xn2, route = _merge(ya, yb, proj, x2, mod3, w_branch_a.astype(bf16), w_branch_b.astype(bf16),
                            w_out.astype(bf16), g_ffn.reshape(1, D), wr_hi, wr_lo,
                            b_router.reshape(1, N_EXPERTS), seq)

    a_total = n * TOP_K
    top_idx = route[:, R_IDX:R_IDX + TOP_K].astype(i32)
    rank = route[:, R_RANK:R_RANK + TOP_K].astype(i32)
    counts = jnp.sum((top_idx[..., None] == jnp.arange(N_EXPERTS, dtype=i32)).astype(i32), axis=(0, 1))
    nblk_e = (counts + MOE_TM - 1) // MOE_TM
    blk_end = jnp.cumsum(nblk_e)
    blk_start = blk_end - nblk_e
    dest = (blk_start[top_idx] * MOE_TM + rank).reshape(a_total)
    p_alloc = a_total + N_EXPERTS * MOE_TM
    tok = jnp.arange(a_total, dtype=i32) // TOP_K
    row_tok = jnp.zeros((p_alloc,), i32).at[dest].set(tok, unique_indices=True, mode="promise_in_bounds")

    n_groups = (a_total // MOE_TM + N_EXPERTS + N_EXPERTS * (MOE_RB - 1)) // MOE_RB
    ng_e = (nblk_e + MOE_RB - 1) // MOE_RB
    g_end = jnp.cumsum(ng_e)
    gid = jnp.arange(n_groups, dtype=i32)
    g_exp_raw = jnp.searchsorted(g_end, gid, side="right").astype(i32)
    active = gid < g_end[-1]
    last_e = jnp.max(jnp.where(counts > 0, jnp.arange(N_EXPERTS, dtype=i32), 0))
    g_expert = jnp.where(active, jnp.minimum(g_exp_raw, N_EXPERTS - 1), last_e).astype(i32)
    g_local = gid - (g_end - ng_e)[g_expert]
    g_nblk = jnp.where(active, jnp.minimum(nblk_e[g_expert] - g_local * MOE_RB, MOE_RB), 0).astype(i32)
    tail0 = blk_end[-1] + (gid - g_end[-1]) * MOE_RB
    g_nzero = jnp.where(active, 0, jnp.clip(p_alloc // MOE_TM - tail0, 0, MOE_RB)).astype(i32)
    g_start = jnp.where(active, blk_start[g_expert] + g_local * MOE_RB,
                        jnp.minimum(tail0, p_alloc // MOE_TM)).astype(i32)

    nrows = (blk_end[-1] * MOE_TM).reshape(1).astype(i32)
    x_rows = _gather_rows(nrows, row_tok.reshape(p_alloc // GATHER_CH, 1, GATHER_CH), xn2, p_alloc)
    y_rows = _moe(g_expert, g_start, g_nblk, g_nzero, x_rows, w_gu, b_gu.reshape(N_EXPERTS, 1, 2 * D_EXPERT),
                  w_down, b_down.reshape(N_EXPERTS, 1, D), n_groups)
    return dest, y_rows, h1, route, mod3


def kernel(x, c, positions, w_ada, b_ada, g_mix, w_in, b_in, sinks, w_branch_a, w_branch_b, w_out, g_ffn,
           w_router, b_router, w_gu, b_gu, w_down, b_down, g_final):
    batch, seq, _ = x.shape
    depth = w_ada.shape[0]
    assert depth == 1, "single-layer trunk"
    c_pad = jnp.zeros((SUBLANES, D), f32).at[:batch].set(c)
    l = 0
    dest, y_rows, h1, route, mod3 = _layer(
        x, c_pad, positions, w_ada[l], b_ada[l], g_mix[l], w_in[l], b_in[l], sinks[l], w_branch_a[l],
        w_branch_b[l], w_out[l], g_ffn[l], w_router[l], b_router[l], w_gu[l], b_gu[l], w_down[l], b_down[l])
    n = batch * seq
    out = _combine(dest.reshape(n // CB_TM, 1, CB_TM * TOP_K), y_rows, h1, route, mod3,
                   g_final.reshape(1, D), seq)
    return out.reshape(batch, seq, D)
```

```python
import jax
import jax.numpy as jnp
from jax import lax
from jax.experimental import pallas as pl
from jax.experimental.pallas import tpu as pltpu

f32 = jnp.float32
bf16 = jnp.bfloat16
i32 = jnp.int32

D = 2048
HD = 64
SWA_QW = 1024
SWA_KVW = 256
FOX_W = 1024
N_FGATE = 16
WINDOW = 128
N_EXPERTS = 32
TOP_K = 4
D_EXPERT = 2048
SWIGLU_LIMIT = 7.0
SWIGLU_ALPHA = 1.702
RMS_EPS = 1e-5
NEG_INF = -1e30
ROPE_THETA = 10000.0

QKV_W = SWA_QW + 2 * SWA_KVW + 3 * FOX_W
FGATE_OFF = QKV_W
GATE_OFF = QKV_W + N_FGATE

LANES = 128
SUBLANES = 8
P_SA = 0
P_SB = D
P_QA = 2 * D
P_KA = P_QA + SWA_QW
P_VA = P_KA + SWA_KVW
P_QB = P_VA + SWA_KVW
P_KB = P_QB + FOX_W
P_VB = P_KB + FOX_W
PROJ_W = P_VB + FOX_W

VMEM_LIMIT = 56 * 1024 * 1024


def _cparams(sem, vmem=VMEM_LIMIT):
    return pltpu.CompilerParams(dimension_semantics=sem, vmem_limit_bytes=vmem)


def _dot(a, b):
    return jnp.dot(a, b, preferred_element_type=f32)


ADA_TN = 1024


def _adaln_kernel(c_ref, w_ref, b_ref, o_ref):
    c = c_ref[...]
    ca = c * jax.nn.sigmoid(c)
    o_ref[...] = _dot(ca, w_ref[...]) + b_ref[...]


def _adaln(c_pad, w_ada, b_ada):
    rows = c_pad.shape[0]
    n = w_ada.shape[1]
    return pl.pallas_call(
        _adaln_kernel,
        grid=(n // ADA_TN,),
        in_specs=[
            pl.BlockSpec((rows, D), lambda j: (0, 0)),
            pl.BlockSpec((D, ADA_TN), lambda j: (0, j)),
            pl.BlockSpec((1, ADA_TN), lambda j: (0, j)),
        ],
        out_specs=pl.BlockSpec((rows, ADA_TN), lambda j: (0, j)),
        out_shape=jax.ShapeDtypeStruct((rows, n), f32),
        compiler_params=_cparams(("arbitrary",)),
        name="adaln",
    )(c_pad, w_ada, b_ada)


IN_TM = 1024
IN_TN = 512
IN_NJ = PROJ_W // IN_TN
J_QA0 = P_QA // IN_TN
J_KVA = P_KA // IN_TN
J_QB0 = P_QB // IN_TN
J_KB0 = P_KB // IN_TN


def _rope_block(a, cos, sin_signed, first_half):
    fwd = pltpu.roll(a, LANES - HD // 2, axis=1)
    bwd = pltpu.roll(a, HD // 2, axis=1)
    partner = jnp.where(first_half, fwd, bwd)
    return a * cos + partner * sin_signed


def _inproj_kernel(x_ref, mod_ref, g_ref, w_ref, b_ref, wf_ref, bf_ref, cos_ref, sin_ref,
                   proj_ref, logf_ref, xn_ref):
    j = pl.program_id(1)

    @pl.when(j == 0)
    def _():
        x = x_ref[...]
        ms = jnp.mean(x * x, axis=-1, keepdims=True)
        y = x * lax.rsqrt(ms + RMS_EPS) * g_ref[...]
        xn = y * (1.0 + mod_ref[:, D:2 * D]) + mod_ref[:, 0:D]
        xnb = xn.astype(bf16)
        xn_ref[...] = xnb
        fl = _dot(xnb, wf_ref[...]) + bf_ref[...]
        logf_ref[...] = jnp.minimum(fl, 0.0) - jnp.log(1.0 + jnp.exp(-jnp.abs(fl)))

    acc = _dot(xn_ref[...], w_ref[...]) + b_ref[...]

    lane = lax.broadcasted_iota(i32, (IN_TM, LANES), 1)
    first_half = (lane % HD) < (HD // 2)
    scale = HD ** -0.5

    def rope_cols(n_blocks, mult):
        cos = cos_ref[...]
        sin = sin_ref[...]
        for g in range(n_blocks):
            sl = slice(g * LANES, (g + 1) * LANES)
            r = _rope_block(acc[:, sl], cos, sin, first_half)
            proj_ref[:, sl] = (r * mult).astype(bf16)

    @pl.when(j < J_QA0)
    def _():
        proj_ref[...] = (0.5 * jnp.tanh(0.5 * acc) + 0.5).astype(bf16)

    @pl.when((j >= J_QA0) & (j < J_KVA))
    def _():
        rope_cols(IN_TN // LANES, scale)

    @pl.when(j == J_KVA)
    def _():
        rope_cols(SWA_KVW // LANES, 1.0)
        proj_ref[:, SWA_KVW:] = acc[:, SWA_KVW:].astype(bf16)

    @pl.when((j >= J_QB0) & (j < J_KB0))
    def _():
        proj_ref[...] = (acc * scale).astype(bf16)

    @pl.when(j >= J_KB0)
    def _():
        proj_ref[...] = acc.astype(bf16)


def _inproj(x2, mod3, g_mix, w_cat, b_cat, w_f, b_f, cos_t, sin_t, seq):
    n = x2.shape[0]
    tiles_per_batch = seq // IN_TM
    return pl.pallas_call(
        _inproj_kernel,
        grid=(n // IN_TM, IN_NJ),
        in_specs=[
            pl.BlockSpec((IN_TM, D), lambda i, j: (i, 0)),
            pl.BlockSpec((None, 1, 6 * D), lambda i, j: (i // tiles_per_batch, 0, 0)),
            pl.BlockSpec((1, D), lambda i, j: (0, 0)),
            pl.BlockSpec((D, IN_TN), lambda i, j: (0, j)),
            pl.BlockSpec((1, IN_TN), lambda i, j: (0, j)),
            pl.BlockSpec((D, N_FGATE), lambda i, j: (0, 0)),
            pl.BlockSpec((1, N_FGATE), lambda i, j: (0, 0)),
            pl.BlockSpec((IN_TM, LANES), lambda i, j: (i, 0)),
            pl.BlockSpec((IN_TM, LANES), lambda i, j: (i, 0)),
        ],
        out_specs=[
            pl.BlockSpec((IN_TM, IN_TN), lambda i, j: (i, j)),
            pl.BlockSpec((IN_TM, N_FGATE), lambda i, j: (i, 0)),
        ],
        out_shape=[
            jax.ShapeDtypeStruct((n, PROJ_W), bf16),
            jax.ShapeDtypeStruct((n, N_FGATE), f32),
        ],
        scratch_shapes=[pltpu.VMEM((IN_TM, D), bf16)],
        compiler_params=_cparams(("arbitrary", "arbitrary")),
        name="inproj",
    )(x2, mod3, g_mix, w_cat, b_cat, w_f, b_f, cos_t, sin_t)


CUM_CH = 256


def _split3(x):
    x1 = x.astype(bf16)
    r1 = x - x1.astype(f32)
    x2 = r1.astype(bf16)
    r2 = r1 - x2.astype(f32)
    return x1, x2, r2.astype(bf16)


def _cumsum_kernel(x_ref, o_ref):
    seq = x_ref.shape[0]
    row = lax.broadcasted_iota(i32, (CUM_CH, CUM_CH), 0)
    col = lax.broadcasted_iota(i32, (CUM_CH, CUM_CH), 1)
    tri = jnp.where(col <= row, 1.0, 0.0).astype(bf16)
    carry = jnp.zeros((1, N_FGATE), f32)
    for c in range(seq // CUM_CH):
        sl = slice(c * CUM_CH, (c + 1) * CUM_CH)
        x1, x2, x3 = _split3(x_ref[sl, :])
        cs = _dot(tri, x1) + _dot(tri, x2) + _dot(tri, x3) + carry
        o_ref[sl, :] = cs
        carry = cs[CUM_CH - 1:CUM_CH, :]


def _cumsum(logf3):
    b, seq, h = logf3.shape
    return pl.pallas_call(
        _cumsum_kernel,
        grid=(b,),
        in_specs=[pl.BlockSpec((None, seq, h), lambda i: (i, 0, 0))],
        out_specs=pl.BlockSpec((None, seq, h), lambda i: (i, 0, 0)),
        out_shape=jax.ShapeDtypeStruct((b, seq, h), f32),
        compiler_params=_cparams(("arbitrary",)),
        name="fgate_cumsum",
    )(logf3)


SWA_QCOLS = 512


def _swa_kernel(sinks_ref, q_ref, k_ref, v_ref, o_ref, vt_ref):
    jp = pl.program_id(1)
    seq = q_ref.shape[0]
    nblk = seq // WINDOW

    for c in range(nblk):
        vblk = v_ref[c * WINDOW:(c + 1) * WINDOW, :].astype(f32)
        vt_ref[c] = vblk.T.astype(bf16)

    kj = lax.broadcasted_iota(i32, (2 * WINDOW, WINDOW), 0)
    qi = lax.broadcasted_iota(i32, (2 * WINDOW, WINDOW), 1)
    delta = qi + WINDOW - kj
    band = (delta >= 0) & (delta < WINDOW)
    zeros_half = jnp.zeros((HD, WINDOW), bf16)

    def body(n, _):
        prev = jnp.maximum(n - 1, 0)
        r0 = pl.multiple_of(n * WINDOW, WINDOW)
        p0 = pl.multiple_of(prev * WINDOW, WINDOW)
        kwin = jnp.concatenate([k_ref[pl.ds(p0, WINDOW), :], k_ref[pl.ds(r0, WINDOW), :]], axis=0)
        valid = band & ((n - 1) * WINDOW + kj >= 0)
        qblk = q_ref[pl.ds(r0, WINDOW), :]
        for t in range(SWA_QCOLS // LANES):
            g = t // 2
            qt = qblk[:, t * LANES:(t + 1) * LANES].astype(f32).T
            v_prev = vt_ref[prev, g * HD:(g + 1) * HD, :]
            v_cur = vt_ref[n, g * HD:(g + 1) * HD, :]
            outs = []
            for u in range(2):
                sink = sinks_ref[jp * (SWA_QCOLS // HD) + t * (LANES // HD) + u]
                feat = qt[u * HD:(u + 1) * HD, :].astype(bf16)
                wq = jnp.concatenate([feat, zeros_half] if g == 0 else [zeros_half, feat], axis=0)
                st = _dot(kwin, wq)
                st = jnp.where(valid, st, NEG_INF)
                m = jnp.maximum(jnp.max(st, axis=0, keepdims=True), sink)
                pt = jnp.exp(st - m)
                denom = jnp.sum(pt, axis=0, keepdims=True) + jnp.exp(sink - m)
                ptb = pt.astype(bf16)
                ot = _dot(v_prev, ptb[:WINDOW, :]) + _dot(v_cur, ptb[WINDOW:, :])
                outs.append(ot / denom)
            o_ref[pl.ds(r0, WINDOW), t * LANES:(t + 1) * LANES] = (
                jnp.concatenate(outs, axis=0).T.astype(bf16))
        return 0

    lax.fori_loop(0, nblk, body, 0)


def _swa(sinks, proj, batch, seq):
    n = proj.shape[0]
    qb0 = P_QA // SWA_QCOLS
    kb0 = P_KA // LANES
    vb0 = P_VA // LANES
    return pl.pallas_call(
        _swa_kernel,
        grid=(batch, SWA_QW // SWA_QCOLS),
        in_specs=[
            pl.BlockSpec(memory_space=pltpu.SMEM),
            pl.BlockSpec((seq, SWA_QCOLS), lambda b, j: (b, qb0 + j)),
            pl.BlockSpec((seq, LANES), lambda b, j: (b, kb0 + j)),
            pl.BlockSpec((seq, LANES), lambda b, j: (b, vb0 + j)),
        ],
        out_specs=pl.BlockSpec((seq, SWA_QCOLS), lambda b, j: (b, j)),
        out_shape=jax.ShapeDtypeStruct((n, SWA_QW), bf16),
        scratch_shapes=[pltpu.VMEM((seq // WINDOW, LANES, WINDOW), bf16)],
        compiler_params=_cparams(("arbitrary", "arbitrary")),
        name="swa_attention",
    )(sinks, proj, proj, proj)


FOX_T = 256
LOG2E = 1.4426950408889634


def _fox_kernel(q_ref, k_ref, v_ref, cum_ref, o_ref, vt_ref, kb_ref):
    p = pl.program_id(1)
    seq = q_ref.shape[0]
    nblk = seq // FOX_T

    for c in range(nblk):
        vblk = v_ref[c * FOX_T:(c + 1) * FOX_T, :].astype(f32)
        vt_ref[c] = vblk.T.astype(bf16)

    cum = cum_ref[...]
    hl = lax.broadcasted_iota(i32, cum.shape, 1)
    for u in range(2):
        col = jnp.sum(jnp.where(hl == 2 * p + u, cum, 0.0), axis=1, keepdims=True)
        kb_ref[u] = jnp.broadcast_to(col * LOG2E, (seq, LANES))

    frow = lax.broadcasted_iota(i32, (LANES, FOX_T), 0)
    kidx = lax.broadcasted_iota(i32, (FOX_T, FOX_T), 0)
    qidx = lax.broadcasted_iota(i32, (FOX_T, FOX_T), 1)
    causal = kidx <= qidx

    def scores(u, wq, k0):
        kblk = k_ref[pl.ds(k0, FOX_T), :]
        kb = kb_ref[u, pl.ds(k0, FOX_T), :]
        return _dot(kblk, wq) - jnp.concatenate([kb] * (FOX_T // LANES), axis=1)

    def update(u, st, jk, state):
        m, l, acc = state
        m_new = jnp.maximum(m, jnp.max(st, axis=0, keepdims=True))
        alpha = jnp.exp2(m - m_new)
        pt = jnp.exp2(st - m_new)
        l_new = alpha * l + jnp.sum(pt, axis=0, keepdims=True)
        vt = vt_ref[jk, u * HD:(u + 1) * HD, :]
        acc_new = alpha * acc + _dot(vt, pt.astype(bf16))
        return m_new, l_new, acc_new

    def qblock(iq, _):
        q0 = pl.multiple_of(iq * FOX_T, FOX_T)
        qt = q_ref[pl.ds(q0, FOX_T), :].astype(f32).T
        qt = qt * LOG2E
        wqs = [jnp.where((frow // HD) == u, qt, 0.0).astype(bf16) for u in range(2)]
        init = tuple((jnp.full((1, FOX_T), NEG_INF, f32), jnp.zeros((1, FOX_T), f32),
                      jnp.zeros((HD, FOX_T), f32)) for _ in range(2))

        def kblock(jk, carry):
            sts, states = carry
            k1 = pl.multiple_of((jk + 1) * FOX_T, FOX_T)
            nxt = tuple(scores(u, wqs[u], k1) for u in range(2))
            return nxt, tuple(update(u, sts[u], jk, states[u]) for u in range(2))

        first = tuple(scores(u, wqs[u], 0) for u in range(2))
        sts, states = lax.fori_loop(0, iq, kblock, (first, init))
        outs = []
        for u in range(2):
            _, l, acc = update(u, jnp.where(causal, sts[u], NEG_INF), iq, states[u])
            outs.append(acc / l)
        o_ref[pl.ds(q0, FOX_T), :] = jnp.concatenate(outs, axis=0).T.astype(bf16)
        return 0

    lax.fori_loop(0, nblk, qblock, 0)


def _fox(proj, cum, batch, seq):
    n = proj.shape[0]
    qb0 = P_QB // LANES
    kb0 = P_KB // LANES
    vb0 = P_VB // LANES
    return pl.pallas_call(
        _fox_kernel,
        grid=(batch, FOX_W // LANES),
        in_specs=[
            pl.BlockSpec((seq, LANES), lambda b, p: (b, qb0 + p)),
            pl.BlockSpec((seq, LANES), lambda b, p: (b, kb0 + p)),
            pl.BlockSpec((seq, LANES), lambda b, p: (b, vb0 + p)),
            pl.BlockSpec((None, seq, N_FGATE), lambda b, p: (b, 0, 0)),
        ],
        out_specs=pl.BlockSpec((seq, LANES), lambda b, p: (b, p)),
        out_shape=jax.ShapeDtypeStruct((n, FOX_W), bf16),
        scratch_shapes=[pltpu.VMEM((seq // FOX_T, LANES, FOX_T), bf16),
                        pltpu.VMEM((2, seq, LANES), f32)],
        compiler_params=_cparams(("arbitrary", "arbitrary")),
        name="fox_attention",
    )(proj, proj, proj, cum)


MG_TM = 256
MG_TN = 512
ROUTE_W = LANES
R_IDX, R_GATE, R_RANK = 0, TOP_K, 2 * TOP_K


def _merge_kernel(ya_ref, yb_ref, sa_ref, sb_ref, x_ref, mod_ref, wa_ref, wb_ref, wo_ref, gf_ref,
                  wrh_ref, wrl_ref, br_ref, h1_ref, xn2_ref, route_ref, merged_ref, cnt_ref):
    i = pl.program_id(0)

    @pl.when(i == 0)
    def _():
        cnt_ref[...] = jnp.zeros_like(cnt_ref)

    ya = ya_ref[...]
    yb = yb_ref[...]
    for c in range(D // MG_TN):
        sl = slice(c * MG_TN, (c + 1) * MG_TN)
        ta = _dot(ya, wa_ref[:, sl])
        tb = _dot(yb, wb_ref[:, sl])
        merged_ref[:, sl] = (sa_ref[:, sl].astype(f32) * ta + sb_ref[:, sl].astype(f32) * tb).astype(bf16)

    merged = merged_ref[...]
    ssq = jnp.zeros((MG_TM, 1), f32)
    for c in range(D // MG_TN):
        sl = slice(c * MG_TN, (c + 1) * MG_TN)
        o = _dot(merged, wo_ref[:, sl])
        h = x_ref[:, sl] + mod_ref[:, 2 * D + c * MG_TN:2 * D + (c + 1) * MG_TN] * o
        h1_ref[:, sl] = h
        ssq = ssq + jnp.sum(h * h, axis=1, keepdims=True)

    inv = lax.rsqrt(ssq * (1.0 / D) + RMS_EPS)
    xn2 = h1_ref[...] * inv * gf_ref[...] * (1.0 + mod_ref[:, 4 * D:5 * D]) + mod_ref[:, 3 * D:4 * D]
    xn2_ref[...] = xn2.reshape(MG_TM, D // LANES, LANES)
    hi = xn2.astype(bf16)
    lo = (xn2 - hi.astype(f32)).astype(bf16)
    wrh = wrh_ref[...]
    logits = _dot(hi, wrh) + _dot(lo, wrh) + _dot(hi, wrl_ref[...]) + br_ref[...]

    elane = lax.broadcasted_iota(i32, (MG_TM, N_EXPERTS), 1)
    work = logits
    onehots, vals, idxs = [], [], []
    for _ in range(TOP_K):
        mk = jnp.max(work, axis=1, keepdims=True)
        idx = jnp.min(jnp.where(work == mk, elane, N_EXPERTS), axis=1, keepdims=True)
        one = elane == idx
        onehots.append(one)
        vals.append(mk)
        idxs.append(idx)
        work = jnp.where(one, -jnp.inf, work)
    exps = [jnp.exp(v - vals[0]) for v in vals]
    denom = exps[0] + exps[1] + exps[2] + exps[3]

    sel = jnp.where(onehots[0] | onehots[1] | onehots[2] | onehots[3], 1.0, 0.0)
    trow = lax.broadcasted_iota(i32, (MG_TM, MG_TM), 0)
    tcol = lax.broadcasted_iota(i32, (MG_TM, MG_TM), 1)
    strict = jnp.where(tcol < trow, 1.0, 0.0).astype(bf16)
    rank = _dot(strict, sel.astype(bf16)) + cnt_ref[...]
    cnt_ref[...] = cnt_ref[...] + jnp.sum(sel, axis=0, keepdims=True)

    rlane = lax.broadcasted_iota(i32, (MG_TM, ROUTE_W), 1)
    rec = jnp.zeros((MG_TM, ROUTE_W), f32)
    for k in range(TOP_K):
        rk = jnp.sum(jnp.where(onehots[k], rank, 0.0), axis=1, keepdims=True)
        rec = jnp.where(rlane == R_IDX + k, idxs[k].astype(f32), rec)
        rec = jnp.where(rlane == R_GATE + k, exps[k] / denom, rec)
        rec = jnp.where(rlane == R_RANK + k, rk, rec)
    route_ref[...] = rec


def _merge(ya, yb, proj, x2, mod3, wa, wb, wo, g_ffn, wr_hi, wr_lo, b_r, seq):
    n = x2.shape[0]
    tiles_per_batch = seq // MG_TM
    const = lambda i: (0, 0)
    return pl.pallas_call(
        _merge_kernel,
        grid=(n // MG_TM,),
        in_specs=[
            pl.BlockSpec((MG_TM, SWA_QW), lambda i: (i, 0)),
            pl.BlockSpec((MG_TM, FOX_W), lambda i: (i, 0)),
            pl.BlockSpec((MG_TM, D), lambda i: (i, P_SA // D)),
            pl.BlockSpec((MG_TM, D), lambda i: (i, P_SB // D)),
            pl.BlockSpec((MG_TM, D), lambda i: (i, 0)),
            pl.BlockSpec((None, 1, 6 * D), lambda i: (i // tiles_per_batch, 0, 0)),
            pl.BlockSpec((SWA_QW, D), const),
            pl.BlockSpec((FOX_W, D), const),
            pl.BlockSpec((D, D), const),
            pl.BlockSpec((1, D), const),
            pl.BlockSpec((D, N_EXPERTS), const),
            pl.BlockSpec((D, N_EXPERTS), const),
            pl.BlockSpec((1, N_EXPERTS), const),
        ],
        out_specs=[
            pl.BlockSpec((MG_TM, D), lambda i: (i, 0)),
            pl.BlockSpec((MG_TM, D // LANES, LANES), lambda i: (i, 0, 0)),
            pl.BlockSpec((MG_TM, ROUTE_W), lambda i: (i, 0)),
        ],
        out_shape=[
            jax.ShapeDtypeStruct((n, D), f32),
            jax.ShapeDtypeStruct((n, D // LANES, LANES), f32),
            jax.ShapeDtypeStruct((n, ROUTE_W), f32),
        ],
        scratch_shapes=[pltpu.VMEM((MG_TM, D), bf16), pltpu.VMEM((1, N_EXPERTS), f32)],
        compiler_params=_cparams(("arbitrary",)),
        name="merge_router",
    )(ya, yb, proj, proj, x2, mod3, wa, wb, wo, g_ffn, wr_hi, wr_lo, b_r)


MOE_TM = 256
GATHER_CH = 512


SLAB = D // LANES


def _gather_kernel(nrows_ref, tok_ref, tok_next_ref, x_ref, o_ref, stage, sems):
    i = pl.program_id(0)
    n = pl.num_programs(0)
    slot = i % 2
    live = i * GATHER_CH < nrows_ref[0]
    live_next = (i + 1 < n) & ((i + 1) * GATHER_CH < nrows_ref[0])

    def copies(idx_ref, sl):
        return [pltpu.make_async_copy(x_ref.at[idx_ref[0, 0, r]], stage.at[sl, r], sems.at[sl])
                for r in range(GATHER_CH)]

    def start(idx_ref, sl):
        for r, c in enumerate(copies(idx_ref, sl)):
            c.start(priority=r % 2)

    @pl.when((i == 0) & live)
    def _():
        start(tok_ref, 0)

    for sl in range(2):
        @pl.when(live_next & (slot == 1 - sl))
        def _():
            start(tok_next_ref, sl)

    for sl in range(2):
        @pl.when(live & (slot == sl))
        def _():
            pltpu.make_async_copy(x_ref.at[pl.ds(0, GATHER_CH)], stage.at[sl], sems.at[sl]).wait()
            o_ref[...] = stage[sl].reshape(GATHER_CH, D).astype(bf16)

    @pl.when(jnp.logical_not(live))
    def _():
        o_ref[...] = jnp.zeros((GATHER_CH, D), bf16)


def _gather_rows(nrows, row_tok3, xn2_slab, p_alloc):
    n_chunks = p_alloc // GATHER_CH
    return pl.pallas_call(
        _gather_kernel,
        grid=(n_chunks,),
        in_specs=[
            pl.BlockSpec(memory_space=pltpu.SMEM),
            pl.BlockSpec((1, 1, GATHER_CH), lambda i: (i, 0, 0), memory_space=pltpu.SMEM),
            pl.BlockSpec((1, 1, GATHER_CH), lambda i: (jnp.minimum(i + 1, n_chunks - 1), 0, 0),
                         memory_space=pltpu.SMEM),
            pl.BlockSpec(memory_space=pl.ANY),
        ],
        out_specs=pl.BlockSpec((GATHER_CH, D), lambda i: (i, 0)),
        out_shape=jax.ShapeDtypeStruct((p_alloc, D), bf16),
        scratch_shapes=[pltpu.VMEM((2, GATHER_CH, SLAB, LANES), f32), pltpu.SemaphoreType.DMA((2,))],
        compiler_params=_cparams(("arbitrary",)),
        name="moe_gather",
    )(nrows, row_tok3, row_tok3, xn2_slab)


MOE_RB = 8
MOE_TF = 512
MOE_NF = D_EXPERT // MOE_TF
MOE_VMEM = 60 * 1024 * 1024


def _moe_kernel(ge_ref, gs_ref, gn_ref, gz_ref, x_hbm, wg_ref, wl_ref, bg_ref, bl_ref, wd_ref, bd_ref, y_hbm,
                xbuf, yacc, yst, sem_in, sem_out):
    s = pl.program_id(0)
    f = pl.program_id(1)
    nb = gn_ref[s]
    nz = gz_ref[s]
    blk0 = gs_ref[s]

    def x_copy(r):
        return pltpu.make_async_copy(x_hbm.at[pl.ds((blk0 + r) * MOE_TM, MOE_TM)], xbuf.at[r], sem_in)

    def y_copy(r, slot):
        return pltpu.make_async_copy(yst.at[slot], y_hbm.at[pl.ds((blk0 + r) * MOE_TM, MOE_TM)],
                                     sem_out.at[slot])

    @pl.when((f == 0) & (nz > 0))
    def _():
        yst[0] = jnp.zeros((MOE_TM, SLAB, LANES), f32)
        for r in range(MOE_RB):
            @pl.when(r < nz)
            def _():
                y_copy(r, 0).start()
        for r in range(MOE_RB):
            @pl.when(r < nz)
            def _():
                y_copy(r, 0).wait()

    @pl.when(f == 0)
    def _():
        for r in range(MOE_RB):
            @pl.when(r < nb)
            def _():
                x_copy(r).start()

        def init(r, _):
            yacc[r] = jnp.broadcast_to(bd_ref[...], (MOE_TM, D))
            return 0

        lax.fori_loop(0, nb, init, 0)
        for r in range(MOE_RB):
            @pl.when(r < nb)
            def _():
                x_copy(r).wait()

    def up(r):
        x = xbuf[r]
        glu = _dot(x, wg_ref[...]) + bg_ref[...]
        lin = _dot(x, wl_ref[...]) + bl_ref[...]
        glu = jnp.minimum(glu, SWIGLU_LIMIT)
        lin = jnp.clip(lin, -SWIGLU_LIMIT, SWIGLU_LIMIT)
        return (glu * jax.nn.sigmoid(SWIGLU_ALPHA * glu) * (lin + 1.0)).astype(bf16)

    def down(r, act):
        yacc[r] = yacc[r] + _dot(act, wd_ref[...])

    @pl.when((nb > 0) & (f < MOE_NF - 1))
    def _():
        def body(r, act_prev):
            down(r - 1, act_prev)
            return up(r)

        act_last = lax.fori_loop(1, nb, body, up(0))
        down(nb - 1, act_last)

    @pl.when((nb > 0) & (f == MOE_NF - 1))
    def _():
        def emit(r, act):
            slot = r % 2

            @pl.when(r >= 2)
            def _():
                y_copy(r - 2, slot).wait()

            yst[slot] = (yacc[r] + _dot(act, wd_ref[...])).reshape(MOE_TM, SLAB, LANES)
            y_copy(r, slot).start()

        def body(r, act_prev):
            emit(r - 1, act_prev)
            return up(r)

        act_last = lax.fori_loop(1, nb, body, up(0))
        emit(nb - 1, act_last)

        @pl.when(nb >= 2)
        def _():
            y_copy(nb - 2, nb % 2).wait()

        y_copy(nb - 1, (nb - 1) % 2).wait()


def _moe(g_expert, g_start, g_nblk, g_nzero, x_rows, w_gu, b_gu3, w_down, b_down3, n_groups):
    p_alloc = x_rows.shape[0]

    def f_eff(s, f, gn):
        return jnp.where(gn[s] > 0, f, MOE_NF - 1)

    grid_spec = pltpu.PrefetchScalarGridSpec(
        num_scalar_prefetch=4,
        grid=(n_groups, MOE_NF),
        in_specs=[
            pl.BlockSpec(memory_space=pl.ANY),
            pl.BlockSpec((None, D, MOE_TF), lambda s, f, ge, gs, gn, gz: (ge[s], 0, f_eff(s, f, gn))),
            pl.BlockSpec((None, D, MOE_TF), lambda s, f, ge, gs, gn, gz: (ge[s], 0, MOE_NF + f_eff(s, f, gn))),
            pl.BlockSpec((None, 1, MOE_TF), lambda s, f, ge, gs, gn, gz: (ge[s], 0, f_eff(s, f, gn))),
            pl.BlockSpec((None, 1, MOE_TF), lambda s, f, ge, gs, gn, gz: (ge[s], 0, MOE_NF + f_eff(s, f, gn))),
            pl.BlockSpec((None, MOE_TF, D), lambda s, f, ge, gs, gn, gz: (ge[s], f_eff(s, f, gn), 0)),
            pl.BlockSpec((None, 1, D), lambda s, f, ge, gs, gn, gz: (ge[s], 0, 0)),
        ],
        out_specs=pl.BlockSpec(memory_space=pl.ANY),
        scratch_shapes=[
            pltpu.VMEM((MOE_RB, MOE_TM, D), bf16),
            pltpu.VMEM((MOE_RB, MOE_TM, D), f32),
            pltpu.VMEM((2, MOE_TM, SLAB, LANES), f32),
            pltpu.SemaphoreType.DMA(()),
            pltpu.SemaphoreType.DMA((2,)),
        ],
    )
    return pl.pallas_call(
        _moe_kernel,
        grid_spec=grid_spec,
        out_shape=jax.ShapeDtypeStruct((p_alloc, SLAB, LANES), f32),
        compiler_params=_cparams(("arbitrary", "arbitrary"), MOE_VMEM),
        name="moe_experts",
    )(g_expert, g_start, g_nblk, g_nzero, x_rows, w_gu, w_gu, b_gu3, b_gu3, w_down, b_down3)


CB_TM = 128


def _combine_kernel(dest_ref, dest_next_ref, y_hbm, h1_ref, route_ref, mod_ref, g_ref, o_ref, ybuf, sems):
    i = pl.program_id(0)
    n = pl.num_programs(0)
    slot = i % 2

    def copies(idx_ref, sl):
        return [pltpu.make_async_copy(y_hbm.at[idx_ref[0, 0, t * TOP_K + k]], ybuf.at[sl, k, t], sems.at[sl])
                for t in range(CB_TM) for k in range(TOP_K)]

    def start(idx_ref, sl):
        for j, c in enumerate(copies(idx_ref, sl)):
            c.start(priority=j % 2)

    @pl.when(i == 0)
    def _():
        start(dest_ref, 0)

    for sl in range(2):
        @pl.when((i + 1 < n) & (slot == 1 - sl))
        def _():
            start(dest_next_ref, sl)

    route = route_ref[...]
    gt2 = mod_ref[:, 5 * D:6 * D]
    for sl in range(2):
        @pl.when(slot == sl)
        def _():
            for k in range(TOP_K):
                pltpu.make_async_copy(y_hbm.at[pl.ds(0, CB_TM)], ybuf.at[sl, k], sems.at[sl]).wait()
            moe = jnp.zeros((CB_TM, D), f32)
            for k in range(TOP_K):
                moe = moe + route[:, R_GATE + k:R_GATE + k + 1] * ybuf[sl, k].reshape(CB_TM, D)
            h = h1_ref[...] + gt2 * moe
            ms = jnp.mean(h * h, axis=-1, keepdims=True)
            o_ref[...] = h * lax.rsqrt(ms + RMS_EPS) * g_ref[...]


def _combine(dest3, y_rows, h1, route, mod3, g_final, seq):
    n = h1.shape[0]
    tiles_per_batch = seq // CB_TM
    n_tiles = n // CB_TM
    return pl.pallas_call(
        _combine_kernel,
        grid=(n_tiles,),
        in_specs=[
            pl.BlockSpec((1, 1, CB_TM * TOP_K), lambda i: (i, 0, 0), memory_space=pltpu.SMEM),
            pl.BlockSpec((1, 1, CB_TM * TOP_K), lambda i: (jnp.minimum(i + 1, n_tiles - 1), 0, 0),
                         memory_space=pltpu.SMEM),
            pl.BlockSpec(memory_space=pl.ANY),
            pl.BlockSpec((CB_TM, D), lambda i: (i, 0)),
            pl.BlockSpec((CB_TM, ROUTE_W), lambda i: (i, 0)),
            pl.BlockSpec((None, 1, 6 * D), lambda i: (i // tiles_per_batch, 0, 0)),
            pl.BlockSpec((1, D), lambda i: (0, 0)),
        ],
        out_specs=pl.BlockSpec((CB_TM, D), lambda i: (i, 0)),
        out_shape=jax.ShapeDtypeStruct((n, D), f32),
        scratch_shapes=[pltpu.VMEM((2, TOP_K, CB_TM, SLAB, LANES), f32), pltpu.SemaphoreType.DMA((2,))],
        compiler_params=_cparams(("arbitrary",)),
        name="moe_combine",
    )(dest3, dest3, y_rows, h1, route, mod3, g_final)


def _layer(h, c_act_pad, positions, w_ada, b_ada, g_mix, w_in, b_in, sinks, w_branch_a, w_branch_b, w_out,
           g_ffn, w_router, b_router, w_gu, b_gu, w_down, b_down):
    batch, seq, _ = h.shape
    n = batch * seq
    x2 = h.reshape(n, D)

    mod = _adaln(c_act_pad, w_ada, b_ada.reshape(1, 6 * D))
    mod3 = mod[:batch].reshape(batch, 1, 6 * D)

    w_cat = jnp.concatenate([w_in[:, GATE_OFF:], w_in[:, :QKV_W]], axis=1).astype(bf16)
    b_cat = jnp.concatenate([b_in[GATE_OFF:], b_in[:QKV_W]]).reshape(1, PROJ_W)
    w_f = w_in[:, FGATE_OFF:GATE_OFF].astype(bf16)
    b_f = b_in[FGATE_OFF:GATE_OFF].reshape(1, N_FGATE)

    inv_freq = 1.0 / (ROPE_THETA ** (jnp.arange(0, HD, 2, dtype=f32) / HD))
    ang = positions.astype(f32).reshape(n, 1) * inv_freq[None, :]
    cos_t = jnp.tile(jnp.cos(ang), (1, LANES // (HD // 2)))
    sin_h = jnp.sin(ang)
    sin_t = jnp.tile(jnp.concatenate([-sin_h, sin_h], axis=1), (1, LANES // HD))

    proj, logf = _inproj(x2, mod3, g_mix.reshape(1, D), w_cat, b_cat, w_f, b_f, cos_t, sin_t, seq)
    cum = _cumsum(logf.reshape(batch, seq, N_FGATE))

    ya = _swa(sinks, proj, batch, seq)
    yb = _fox(proj, cum, batch, seq)

    wr_hi = w_router.astype(bf16)
    wr_lo = (w_router - wr_hi.astype(f32)).astype(bf16)
    h1, xn2, route = _merge(ya, yb, proj, x2, mod3, w_branch_a.astype(bf16), w_branch_b.astype(bf16),
                            w_out.astype(bf16), g_ffn.reshape(1, D), wr_hi, wr_lo,
                            b_router.reshape(1, N_EXPERTS), seq)

    a_total = n * TOP_K
    top_idx = route[:, R_IDX:R_IDX + TOP_K].astype(i32)
    rank = route[:, R_RANK:R_RANK + TOP_K].astype(i32)
    counts = jnp.sum((top_idx[..., None] == jnp.arange(N_EXPERTS, dtype=i32)).astype(i32), axis=(0, 1))
    nblk_e = (counts + MOE_TM - 1) // MOE_TM
    blk_end = jnp.cumsum(nblk_e)
    blk_start = blk_end - nblk_e
    dest = (blk_start[top_idx] * MOE_TM + rank).reshape(a_total)
    p_alloc = a_total + N_EXPERTS * MOE_TM
    tok = jnp.arange(a_total, dtype=i32) // TOP_K
    row_tok = jnp.zeros((p_alloc,), i32).at[dest].set(tok, unique_indices=True, mode="promise_in_bounds")

    n_groups = (a_total // MOE_TM + N_EXPERTS + N_EXPERTS * (MOE_RB - 1)) // MOE_RB
    ng_e = (nblk_e + MOE_RB - 1) // MOE_RB
    g_end = jnp.cumsum(ng_e)
    gid = jnp.arange(n_groups, dtype=i32)
    g_exp_raw = jnp.searchsorted(g_end, gid, side="right").astype(i32)
    active = gid < g_end[-1]
    last_e = jnp.max(jnp.where(counts > 0, jnp.arange(N_EXPERTS, dtype=i32), 0))
    g_expert = jnp.where(active, jnp.minimum(g_exp_raw, N_EXPERTS - 1), last_e).astype(i32)
    g_local = gid - (g_end - ng_e)[g_expert]
    g_nblk = jnp.where(active, jnp.minimum(nblk_e[g_expert] - g_local * MOE_RB, MOE_RB), 0).astype(i32)
    tail0 = blk_end[-1] + (gid - g_end[-1]) * MOE_RB
    g_nzero = jnp.where(active, 0, jnp.clip(p_alloc // MOE_TM - tail0, 0, MOE_RB)).astype(i32)
    g_start = jnp.where(active, blk_start[g_expert] + g_local * MOE_RB,
                        jnp.minimum(tail0, p_alloc // MOE_TM)).astype(i32)

    nrows = (blk_end[-1] * MOE_TM).reshape(1).astype(i32)
    x_rows = _gather_rows(nrows, row_tok.reshape(p_alloc // GATHER_CH, 1, GATHER_CH), xn2, p_alloc)
    y_rows = _moe(g_expert, g_start, g_nblk, g_nzero, x_rows, w_gu, b_gu.reshape(N_EXPERTS, 1, 2 * D_EXPERT),
                  w_down, b_down.reshape(N_EXPERTS, 1, D), n_groups)
    return dest, y_rows, h1, route, mod3


def kernel(x, c, positions, w_ada, b_ada, g_mix, w_in, b_in, sinks, w_branch_a, w_branch_b, w_out, g_ffn,
           w_router, b_router, w_gu, b_gu, w_down, b_down, g_final):
    batch, seq, _ = x.shape
    depth = w_ada.shape[0]
    assert depth == 1, "single-layer trunk"
    c_pad = jnp.zeros((SUBLANES, D), f32).at[:batch].set(c)
    l = 0
    dest, y_rows, h1, route, mod3 = _layer(
        x, c_pad, positions, w_ada[l], b_ada[l], g_mix[l], w_in[l], b_in[l], sinks[l], w_branch_a[l],
        w_branch_b[l], w_out[l], g_ffn[l], w_router[l], b_router[l], w_gu[l], b_gu[l], w_down[l], b_down[l])
    n = batch * seq
    out = _combine(dest.reshape(n // CB_TM, 1, CB_TM * TOP_K), y_rows, h1, route, mod3,
                   g_final.reshape(1, D), seq)
    return out.reshape(batch, seq, D)
```

```python
import jax
import jax.numpy as jnp
from jax import lax
from jax.experimental import pallas as pl
from jax.experimental.pallas import tpu as pltpu

f32 = jnp.float32
bf16 = jnp.bfloat16
i32 = jnp.int32

D = 2048
HD = 64
SWA_QW = 1024
SWA_KVW = 256
FOX_W = 1024
N_FGATE = 16
WINDOW = 128
N_EXPERTS = 32
TOP_K = 4
D_EXPERT = 2048
SWIGLU_LIMIT = 7.0
SWIGLU_ALPHA = 1.702
RMS_EPS = 1e-5
NEG_INF = -1e30
ROPE_THETA = 10000.0

QKV_W = SWA_QW + 2 * SWA_KVW + 3 * FOX_W
FGATE_OFF = QKV_W
GATE_OFF = QKV_W + N_FGATE

LANES = 128
SUBLANES = 8
P_SA = 0
P_SB = D
P_QA = 2 * D
P_KA = P_QA + SWA_QW
P_VA = P_KA + SWA_KVW
P_QB = P_VA + SWA_KVW
P_KB = P_QB + FOX_W
P_VB = P_KB + FOX_W
PROJ_W = P_VB + FOX_W

VMEM_LIMIT = 56 * 1024 * 1024


def _cparams(sem, vmem=VMEM_LIMIT):
    return pltpu.CompilerParams(dimension_semantics=sem, vmem_limit_bytes=vmem)


def _dot(a, b):
    return jnp.dot(a, b, preferred_element_type=f32)


ADA_TN = 1024


def _adaln_kernel(c_ref, w_ref, b_ref, o_ref):
    c = c_ref[...]
    ca = c * jax.nn.sigmoid(c)
    o_ref[...] = _dot(ca, w_ref[...]) + b_ref[...]


def _adaln(c_pad, w_ada, b_ada):
    rows = c_pad.shape[0]
    n = w_ada.shape[1]
    return pl.pallas_call(
        _adaln_kernel,
        grid=(n // ADA_TN,),
        in_specs=[
            pl.BlockSpec((rows, D), lambda j: (0, 0)),
            pl.BlockSpec((D, ADA_TN), lambda j: (0, j)),
            pl.BlockSpec((1, ADA_TN), lambda j: (0, j)),
        ],
        out_specs=pl.BlockSpec((rows, ADA_TN), lambda j: (0, j)),
        out_shape=jax.ShapeDtypeStruct((rows, n), f32),
        compiler_params=_cparams(("arbitrary",)),
        name="adaln",
    )(c_pad, w_ada, b_ada)


IN_TM = 1024
IN_TN = 512
IN_NJ = PROJ_W // IN_TN
J_QA0 = P_QA // IN_TN
J_KVA = P_KA // IN_TN
J_QB0 = P_QB // IN_TN
J_KB0 = P_KB // IN_TN


def _rope_block(a, cos, sin_signed, first_half):
    fwd = pltpu.roll(a, LANES - HD // 2, axis=1)
    bwd = pltpu.roll(a, HD // 2, axis=1)
    partner = jnp.where(first_half, fwd, bwd)
    return a * cos + partner * sin_signed


def _inproj_kernel(x_ref, mod_ref, g_ref, w_ref, b_ref, wf_ref, bf_ref, cos_ref, sin_ref,
                   proj_ref, logf_ref, xn_ref):
    j = pl.program_id(1)

    @pl.when(j == 0)
    def _():
        x = x_ref[...]
        ms = jnp.mean(x * x, axis=-1, keepdims=True)
        y = x * lax.rsqrt(ms + RMS_EPS) * g_ref[...]
        xn = y * (1.0 + mod_ref[:, D:2 * D]) + mod_ref[:, 0:D]
        xnb = xn.astype(bf16)
        xn_ref[...] = xnb
        fl = _dot(xnb, wf_ref[...]) + bf_ref[...]
        logf_ref[...] = jnp.minimum(fl, 0.0) - jnp.log(1.0 + jnp.exp(-jnp.abs(fl)))

    acc = _dot(xn_ref[...], w_ref[...]) + b_ref[...]

    lane = lax.broadcasted_iota(i32, (IN_TM, LANES), 1)
    first_half = (lane % HD) < (HD // 2)
    scale = HD ** -0.5

    def rope_cols(n_blocks, mult):
        cos = cos_ref[...]
        sin = sin_ref[...]
        for g in range(n_blocks):
            sl = slice(g * LANES, (g + 1) * LANES)
            r = _rope_block(acc[:, sl], cos, sin, first_half)
            proj_ref[:, sl] = (r * mult).astype(bf16)

    @pl.when(j < J_QA0)
    def _():
        proj_ref[...] = (0.5 * jnp.tanh(0.5 * acc) + 0.5).astype(bf16)

    @pl.when((j >= J_QA0) & (j < J_KVA))
    def _():
        rope_cols(IN_TN // LANES, scale)

    @pl.when(j == J_KVA)
    def _():
        rope_cols(SWA_KVW // LANES, 1.0)
        proj_ref[:, SWA_KVW:] = acc[:, SWA_KVW:].astype(bf16)

    @pl.when((j >= J_QB0) & (j < J_KB0))
    def _():
        proj_ref[...] = (acc * scale).astype(bf16)

    @pl.when(j >= J_KB0)
    def _():
        proj_ref[...] = acc.astype(bf16)


def _inproj(x2, mod3, g_mix, w_cat, b_cat, w_f, b_f, cos_t, sin_t, seq):
    n = x2.shape[0]
    tiles_per_batch = seq // IN_TM
    return pl.pallas_call(
        _inproj_kernel,
        grid=(n // IN_TM, IN_NJ),
        in_specs=[
            pl.BlockSpec((IN_TM, D), lambda i, j: (i, 0)),
            pl.BlockSpec((None, 1, 6 * D), lambda i, j: (i // tiles_per_batch, 0, 0)),
            pl.BlockSpec((1, D), lambda i, j: (0, 0)),
            pl.BlockSpec((D, IN_TN), lambda i, j: (0, j)),
            pl.BlockSpec((1, IN_TN), lambda i, j: (0, j)),
            pl.BlockSpec((D, N_FGATE), lambda i, j: (0, 0)),
            pl.BlockSpec((1, N_FGATE), lambda i, j: (0, 0)),
            pl.BlockSpec((IN_TM, LANES), lambda i, j: (i, 0)),
            pl.BlockSpec((IN_TM, LANES), lambda i, j: (i, 0)),
        ],
        out_specs=[
            pl.BlockSpec((IN_TM, IN_TN), lambda i, j: (i, j)),
            pl.BlockSpec((IN_TM, N_FGATE), lambda i, j: (i, 0)),
        ],
        out_shape=[
            jax.ShapeDtypeStruct((n, PROJ_W), bf16),
            jax.ShapeDtypeStruct((n, N_FGATE), f32),
        ],
        scratch_shapes=[pltpu.VMEM((IN_TM, D), bf16)],
        compiler_params=_cparams(("arbitrary", "arbitrary")),
        name="inproj",
    )(x2, mod3, g_mix, w_cat, b_cat, w_f, b_f, cos_t, sin_t)


CUM_CH = 256


def _split3(x):
    x1 = x.astype(bf16)
    r1 = x - x1.astype(f32)
    x2 = r1.astype(bf16)
    r2 = r1 - x2.astype(f32)
    return x1, x2, r2.astype(bf16)


def _cumsum_kernel(x_ref, o_ref):
    seq = x_ref.shape[0]
    row = lax.broadcasted_iota(i32, (CUM_CH, CUM_CH), 0)
    col = lax.broadcasted_iota(i32, (CUM_CH, CUM_CH), 1)
    tri = jnp.where(col <= row, 1.0, 0.0).astype(bf16)
    carry = jnp.zeros((1, N_FGATE), f32)
    for c in range(seq // CUM_CH):
        sl = slice(c * CUM_CH, (c + 1) * CUM_CH)
        x1, x2, x3 = _split3(x_ref[sl, :])
        cs = _dot(tri, x1) + _dot(tri, x2) + _dot(tri, x3) + carry
        o_ref[sl, :] = cs
        carry = cs[CUM_CH - 1:CUM_CH, :]


def _cumsum(logf3):
    b, seq, h = logf3.shape
    return pl.pallas_call(
        _cumsum_kernel,
        grid=(b,),
        in_specs=[pl.BlockSpec((None, seq, h), lambda i: (i, 0, 0))],
        out_specs=pl.BlockSpec((None, seq, h), lambda i: (i, 0, 0)),
        out_shape=jax.ShapeDtypeStruct((b, seq, h), f32),
        compiler_params=_cparams(("arbitrary",)),
        name="fgate_cumsum",
    )(logf3)


SWA_QCOLS = 512


def _swa_kernel(sinks_ref, q_ref, k_ref, v_ref, o_ref, vt_ref):
    jp = pl.program_id(1)
    seq = q_ref.shape[0]
    nblk = seq // WINDOW

    for c in range(nblk):
        vblk = v_ref[c * WINDOW:(c + 1) * WINDOW, :].astype(f32)
        vt_ref[c] = vblk.T.astype(bf16)

    kj = lax.broadcasted_iota(i32, (2 * WINDOW, WINDOW), 0)
    qi = lax.broadcasted_iota(i32, (2 * WINDOW, WINDOW), 1)
    delta = qi + WINDOW - kj
    band = (delta >= 0) & (delta < WINDOW)
    zeros_half = jnp.zeros((HD, WINDOW), bf16)

    def body(n, _):
        prev = jnp.maximum(n - 1, 0)
        r0 = pl.multiple_of(n * WINDOW, WINDOW)
        p0 = pl.multiple_of(prev * WINDOW, WINDOW)
        kwin = jnp.concatenate([k_ref[pl.ds(p0, WINDOW), :], k_ref[pl.ds(r0, WINDOW), :]], axis=0)
        valid = band & ((n - 1) * WINDOW + kj >= 0)
        qblk = q_ref[pl.ds(r0, WINDOW), :]
        for t in range(SWA_QCOLS // LANES):
            g = t // 2
            qt = qblk[:, t * LANES:(t + 1) * LANES].astype(f32).T
            v_prev = vt_ref[prev, g * HD:(g + 1) * HD, :]
            v_cur = vt_ref[n, g * HD:(g + 1) * HD, :]
            outs = []
            for u in range(2):
                sink = sinks_ref[jp * (SWA_QCOLS // HD) + t * (LANES // HD) + u]
                feat = qt[u * HD:(u + 1) * HD, :].astype(bf16)
                wq = jnp.concatenate([feat, zeros_half] if g == 0 else [zeros_half, feat], axis=0)
                st = _dot(kwin, wq)
                st = jnp.where(valid, st, NEG_INF)
                m = jnp.maximum(jnp.max(st, axis=0, keepdims=True), sink)
                pt = jnp.exp(st - m)
                denom = jnp.sum(pt, axis=0, keepdims=True) + jnp.exp(sink - m)
                ptb = pt.astype(bf16)
                ot = _dot(v_prev, ptb[:WINDOW, :]) + _dot(v_cur, ptb[WINDOW:, :])
                outs.append(ot / denom)
            o_ref[pl.ds(r0, WINDOW), t * LANES:(t + 1) * LANES] = (
                jnp.concatenate(outs, axis=0).T.astype(bf16))
        return 0

    lax.fori_loop(0, nblk, body, 0)


def _swa(sinks, proj, batch, seq):
    n = proj.shape[0]
    qb0 = P_QA // SWA_QCOLS
    kb0 = P_KA // LANES
    vb0 = P_VA // LANES
    return pl.pallas_call(
        _swa_kernel,
        grid=(batch, SWA_QW // SWA_QCOLS),
        in_specs=[
            pl.BlockSpec(memory_space=pltpu.SMEM),
            pl.BlockSpec((seq, SWA_QCOLS), lambda b, j: (b, qb0 + j)),
            pl.BlockSpec((seq, LANES), lambda b, j: (b, kb0 + j)),
            pl.BlockSpec((seq, LANES), lambda b, j: (b, vb0 + j)),
        ],
        out_specs=pl.BlockSpec((seq, SWA_QCOLS), lambda b, j: (b, j)),
        out_shape=jax.ShapeDtypeStruct((n, SWA_QW), bf16),
        scratch_shapes=[pltpu.VMEM((seq // WINDOW, LANES, WINDOW), bf16)],
        compiler_params=_cparams(("arbitrary", "arbitrary")),
        name="swa_attention",
    )(sinks, proj, proj, proj)


FOX_T = 256
LOG2E = 1.4426950408889634


def _fox_kernel(q_ref, k_ref, v_ref, cum_ref, o_ref, vt_ref, kb_ref):
    p = pl.program_id(1)
    seq = q_ref.shape[0]
    nblk = seq // FOX_T

    for c in range(nblk):
        vblk = v_ref[c * FOX_T:(c + 1) * FOX_T, :].astype(f32)
        vt_ref[c] = vblk.T.astype(bf16)

    cum = cum_ref[...]
    hl = lax.broadcasted_iota(i32, cum.shape, 1)
    for u in range(2):
        col = jnp.sum(jnp.where(hl == 2 * p + u, cum, 0.0), axis=1, keepdims=True)
        kb_ref[u] = jnp.broadcast_to(col * LOG2E, (seq, LANES))

    frow = lax.broadcasted_iota(i32, (LANES, FOX_T), 0)
    kidx = lax.broadcasted_iota(i32, (FOX_T, FOX_T), 0)
    qidx = lax.broadcasted_iota(i32, (FOX_T, FOX_T), 1)
    causal = kidx <= qidx

    def scores(u, wq, k0):
        kblk = k_ref[pl.ds(k0, FOX_T), :]
        kb = kb_ref[u, pl.ds(k0, FOX_T), :]
        return _dot(kblk, wq) - jnp.concatenate([kb] * (FOX_T // LANES), axis=1)

    def update(u, st, jk, state):
        m, l, acc = state
        m_new = jnp.maximum(m, jnp.max(st, axis=0, keepdims=True))
        alpha = jnp.exp2(m - m_new)
        pt = jnp.exp2(st - m_new)
        l_new = alpha * l + jnp.sum(pt, axis=0, keepdims=True)
        vt = vt_ref[jk, u * HD:(u + 1) * HD, :]
        acc_new = alpha * acc + _dot(vt, pt.astype(bf16))
        return m_new, l_new, acc_new

    def qblock(iq, _):
        q0 = pl.multiple_of(iq * FOX_T, FOX_T)
        qt = q_ref[pl.ds(q0, FOX_T), :].astype(f32).T
        qt = qt * LOG2E
        wqs = [jnp.where((frow // HD) == u, qt, 0.0).astype(bf16) for u in range(2)]
        init = tuple((jnp.full((1, FOX_T), NEG_INF, f32), jnp.zeros((1, FOX_T), f32),
                      jnp.zeros((HD, FOX_T), f32)) for _ in range(2))

        def kblock(jk, carry):
            sts, states = carry
            k1 = pl.multiple_of((jk + 1) * FOX_T, FOX_T)
            nxt = tuple(scores(u, wqs[u], k1) for u in range(2))
            return nxt, tuple(update(u, sts[u], jk, states[u]) for u in range(2))

        first = tuple(scores(u, wqs[u], 0) for u in range(2))
        sts, states = lax.fori_loop(0, iq, kblock, (first, init))
        outs = []
        for u in range(2):
            _, l, acc = update(u, jnp.where(causal, sts[u], NEG_INF), iq, states[u])
            outs.append(acc / l)
        o_ref[pl.ds(q0, FOX_T), :] = jnp.concatenate(outs, axis=0).T.astype(bf16)
        return 0

    lax.fori_loop(0, nblk, qblock, 0)


def _fox(proj, cum, batch, seq):
    n = proj.shape[0]
    qb0 = P_QB // LANES
    kb0 = P_KB // LANES
    vb0 = P_VB // LANES
    return pl.pallas_call(
        _fox_kernel,
        grid=(batch, FOX_W // LANES),
        in_specs=[
            pl.BlockSpec((seq, LANES), lambda b, p: (b, qb0 + p)),
            pl.BlockSpec((seq, LANES), lambda b, p: (b, kb0 + p)),
            pl.BlockSpec((seq, LANES), lambda b, p: (b, vb0 + p)),
            pl.BlockSpec((None, seq, N_FGATE), lambda b, p: (b, 0, 0)),
        ],
        out_specs=pl.BlockSpec((seq, LANES), lambda b, p: (b, p)),
        out_shape=jax.ShapeDtypeStruct((n, FOX_W), bf16),
        scratch_shapes=[pltpu.VMEM((seq // FOX_T, LANES, FOX_T), bf16),
                        pltpu.VMEM((2, seq, LANES), f32)],
        compiler_params=_cparams(("arbitrary", "arbitrary")),
        name="fox_attention",
    )(proj, proj, proj, cum)


MG_TM = 256
MG_TN = 512
ROUTE_W = LANES
R_IDX, R_GATE, R_RANK = 0, TOP_K, 2 * TOP_K


def _merge_kernel(ya_ref, yb_ref, sa_ref, sb_ref, x_ref, mod_ref, wa_ref, wb_ref, wo_ref, gf_ref,
                  wrh_ref, wrl_ref, br_ref, h1_ref, xn2_ref, route_ref, merged_ref, cnt_ref):
    i = pl.program_id(0)

    @pl.when(i == 0)
    def _():
        cnt_ref[...] = jnp.zeros_like(cnt_ref)

    ya = ya_ref[...]
    yb = yb_ref[...]
    for c in range(D // MG_TN):
        sl = slice(c * MG_TN, (c + 1) * MG_TN)
        ta = _dot(ya, wa_ref[:, sl])
        tb = _dot(yb, wb_ref[:, sl])
        merged_ref[:, sl] = (sa_ref[:, sl].astype(f32) * ta + sb_ref[:, sl].astype(f32) * tb).astype(bf16)

    merged = merged_ref[...]
    ssq = jnp.zeros((MG_TM, 1), f32)
    for c in range(D // MG_TN):
        sl = slice(c * MG_TN, (c + 1) * MG_TN)
        o = _dot(merged, wo_ref[:, sl])
        h = x_ref[:, sl] + mod_ref[:, 2 * D + c * MG_TN:2 * D + (c + 1) * MG_TN] * o
        h1_ref[:, sl] = h
        ssq = ssq + jnp.sum(h * h, axis=1, keepdims=True)

    inv = lax.rsqrt(ssq * (1.0 / D) + RMS_EPS)
    xn2 = h1_ref[...] * inv * gf_ref[...] * (1.0 + mod_ref[:, 4 * D:5 * D]) + mod_ref[:, 3 * D:4 * D]
    xn2_ref[...] = xn2.reshape(MG_TM, D // LANES, LANES)
    hi = xn2.astype(bf16)
    lo = (xn2 - hi.astype(f32)).astype(bf16)
    wrh = wrh_ref[...]
    logits = _dot(hi, wrh) + _dot(lo, wrh) + _dot(hi, wrl_ref[...]) + br_ref[...]

    elane = lax.broadcasted_iota(i32, (MG_TM, N_EXPERTS), 1)
    work = logits
    onehots, vals, idxs = [], [], []
    for _ in range(TOP_K):
        mk = jnp.max(work, axis=1, keepdims=True)
        idx = jnp.min(jnp.where(work == mk, elane, N_EXPERTS), axis=1, keepdims=True)
        one = elane == idx
        onehots.append(one)
        vals.append(mk)
        idxs.append(idx)
        work = jnp.where(one, -jnp.inf, work)
    exps = [jnp.exp(v - vals[0]) for v in vals]
    denom = exps[0] + exps[1] + exps[2] + exps[3]

    sel = jnp.where(onehots[0] | onehots[1] | onehots[2] | onehots[3], 1.0, 0.0)
    trow = lax.broadcasted_iota(i32, (MG_TM, MG_TM), 0)
    tcol = lax.broadcasted_iota(i32, (MG_TM, MG_TM), 1)
    strict = jnp.where(tcol < trow, 1.0, 0.0).astype(bf16)
    rank = _dot(strict, sel.astype(bf16)) + cnt_ref[...]
    cnt_ref[...] = cnt_ref[...] + jnp.sum(sel, axis=0, keepdims=True)

    rlane = lax.broadcasted_iota(i32, (MG_TM, ROUTE_W), 1)
    rec = jnp.zeros((MG_TM, ROUTE_W), f32)
    for k in range(TOP_K):
        rk = jnp.sum(jnp.where(onehots[k], rank, 0.0), axis=1, keepdims=True)
        rec = jnp.where(rlane == R_IDX + k, idxs[k].astype(f32), rec)
        rec = jnp.where(rlane == R_GATE + k, exps[k] / denom, rec)
        rec = jnp.where(rlane == R_RANK + k, rk, rec)
    route_ref[...] = rec


def _merge(ya, yb, proj, x2, mod3, wa, wb, wo, g_ffn, wr_hi, wr_lo, b_r, seq):
    n = x2.shape[0]
    tiles_per_batch = seq // MG_TM
    const = lambda i: (0, 0)
    return pl.pallas_call(
        _merge_kernel,
        grid=(n // MG_TM,),
        in_specs=[
            pl.BlockSpec((MG_TM, SWA_QW), lambda i: (i, 0)),
            pl.BlockSpec((MG_TM, FOX_W), lambda i: (i, 0)),
            pl.BlockSpec((MG_TM, D), lambda i: (i, P_SA // D)),
            pl.BlockSpec((MG_TM, D), lambda i: (i, P_SB // D)),
            pl.BlockSpec((MG_TM, D), lambda i: (i, 0)),
            pl.BlockSpec((None, 1, 6 * D), lambda i: (i // tiles_per_batch, 0, 0)),
            pl.BlockSpec((SWA_QW, D), const),
            pl.BlockSpec((FOX_W, D), const),
            pl.BlockSpec((D, D), const),
            pl.BlockSpec((1, D), const),
            pl.BlockSpec((D, N_EXPERTS), const),
            pl.BlockSpec((D, N_EXPERTS), const),
            pl.BlockSpec((1, N_EXPERTS), const),
        ],
        out_specs=[
            pl.BlockSpec((MG_TM, D), lambda i: (i, 0)),
            pl.BlockSpec((MG_TM, D // LANES, LANES), lambda i: (i, 0, 0)),
            pl.BlockSpec((MG_TM, ROUTE_W), lambda i: (i, 0)),
        ],
        out_shape=[
            jax.ShapeDtypeStruct((n, D), f32),
            jax.ShapeDtypeStruct((n, D // LANES, LANES), f32),
            jax.ShapeDtypeStruct((n, ROUTE_W), f32),
        ],
        scratch_shapes=[pltpu.VMEM((MG_TM, D), bf16), pltpu.VMEM((1, N_EXPERTS), f32)],
        compiler_params=_cparams(("arbitrary",)),
        name="merge_router",
    )(ya, yb, proj, proj, x2, mod3, wa, wb, wo, g_ffn, wr_hi, wr_lo, b_r)


MOE_TM = 256
GATHER_CH = 1024


SLAB = D // LANES


def _gather_kernel(nrows_ref, tok_ref, tok_next_ref, x_ref, o_ref, stage, sems):
    i = pl.program_id(0)
    n = pl.num_programs(0)
    slot = i % 2
    live = i * GATHER_CH < nrows_ref[0]
    live_next = (i + 1 < n) & ((i + 1) * GATHER_CH < nrows_ref[0])

    def copies(idx_ref, sl):
        return [pltpu.make_async_copy(x_ref.at[idx_ref[0, 0, r]], stage.at[sl, r], sems.at[sl])
                for r in range(GATHER_CH)]

    def start(idx_ref, sl):
        for r, c in enumerate(copies(idx_ref, sl)):
            c.start(priority=r % 2)

    @pl.when((i == 0) & live)
    def _():
        start(tok_ref, 0)

    for sl in range(2):
        @pl.when(live_next & (slot == 1 - sl))
        def _():
            start(tok_next_ref, sl)

    for sl in range(2):
        @pl.when(live & (slot == sl))
        def _():
            pltpu.make_async_copy(x_ref.at[pl.ds(0, GATHER_CH)], stage.at[sl], sems.at[sl]).wait()
            o_ref[...] = stage[sl].reshape(GATHER_CH, D).astype(bf16)

    @pl.when(jnp.logical_not(live))
    def _():
        o_ref[...] = jnp.zeros((GATHER_CH, D), bf16)


def _gather_rows(nrows, row_tok3, xn2_slab, p_alloc):
    n_chunks = p_alloc // GATHER_CH
    return pl.pallas_call(
        _gather_kernel,
        grid=(n_chunks,),
        in_specs=[
            pl.BlockSpec(memory_space=pltpu.SMEM),
            pl.BlockSpec((1, 1, GATHER_CH), lambda i: (i, 0, 0), memory_space=pltpu.SMEM),
            pl.BlockSpec((1, 1, GATHER_CH), lambda i: (jnp.minimum(i + 1, n_chunks - 1), 0, 0),
                         memory_space=pltpu.SMEM),
            pl.BlockSpec(memory_space=pl.ANY),
        ],
        out_specs=pl.BlockSpec((GATHER_CH, D), lambda i: (i, 0)),
        out_shape=jax.ShapeDtypeStruct((p_alloc, D), bf16),
        scratch_shapes=[pltpu.VMEM((2, GATHER_CH, SLAB, LANES), f32), pltpu.SemaphoreType.DMA((2,))],
        compiler_params=_cparams(("arbitrary",)),
        name="moe_gather",
    )(nrows, row_tok3, row_tok3, xn2_slab)


MOE_RB = 8
MOE_TF = 512
MOE_NF = D_EXPERT // MOE_TF
MOE_VMEM = 60 * 1024 * 1024


def _moe_kernel(ge_ref, gs_ref, gn_ref, gz_ref, x_hbm, wg_ref, wl_ref, bg_ref, bl_ref, wd_ref, bd_ref, y_hbm,
                xbuf, yacc, yst, sem_in, sem_out):
    s = pl.program_id(0)
    f = pl.program_id(1)
    nb = gn_ref[s]
    nz = gz_ref[s]
    blk0 = gs_ref[s]

    def x_copy(r):
        return pltpu.make_async_copy(x_hbm.at[pl.ds((blk0 + r) * MOE_TM, MOE_TM)], xbuf.at[r], sem_in)

    def y_copy(r, slot):
        return pltpu.make_async_copy(yst.at[slot], y_hbm.at[pl.ds((blk0 + r) * MOE_TM, MOE_TM)],
                                     sem_out.at[slot])

    @pl.when((f == 0) & (nz > 0))
    def _():
        yst[0] = jnp.zeros((MOE_TM, SLAB, LANES), f32)
        for r in range(MOE_RB):
            @pl.when(r < nz)
            def _():
                y_copy(r, 0).start()
        for r in range(MOE_RB):
            @pl.when(r < nz)
            def _():
                y_copy(r, 0).wait()

    @pl.when(f == 0)
    def _():
        for r in range(MOE_RB):
            @pl.when(r < nb)
            def _():
                x_copy(r).start()

        def init(r, _):
            yacc[r] = jnp.broadcast_to(bd_ref[...], (MOE_TM, D))
            return 0

        lax.fori_loop(0, nb, init, 0)
        for r in range(MOE_RB):
            @pl.when(r < nb)
            def _():
                x_copy(r).wait()

    def up(r):
        x = xbuf[r]
        glu = _dot(x, wg_ref[...]) + bg_ref[...]
        lin = _dot(x, wl_ref[...]) + bl_ref[...]
        glu = jnp.minimum(glu, SWIGLU_LIMIT)
        lin = jnp.clip(lin, -SWIGLU_LIMIT, SWIGLU_LIMIT)
        return (glu * jax.nn.sigmoid(SWIGLU_ALPHA * glu) * (lin + 1.0)).astype(bf16)

    def down(r, act):
        yacc[r] = yacc[r] + _dot(act, wd_ref[...])

    @pl.when((nb > 0) & (f < MOE_NF - 1))
    def _():
        def body(r, act_prev):
            down(r - 1, act_prev)
            return up(r)

        act_last = lax.fori_loop(1, nb, body, up(0))
        down(nb - 1, act_last)

    @pl.when((nb > 0) & (f == MOE_NF - 1))
    def _():
        def emit(r, act):
            slot = r % 2

            @pl.when(r >= 2)
            def _():
                y_copy(r - 2, slot).wait()

            yst[slot] = (yacc[r] + _dot(act, wd_ref[...])).reshape(MOE_TM, SLAB, LANES)
            y_copy(r, slot).start()

        def body(r, act_prev):
            emit(r - 1, act_prev)
            return up(r)

        act_last = lax.fori_loop(1, nb, body, up(0))
        emit(nb - 1, act_last)

        @pl.when(nb >= 2)
        def _():
            y_copy(nb - 2, nb % 2).wait()

        y_copy(nb - 1, (nb - 1) % 2).wait()


def _moe(g_expert, g_start, g_nblk, g_nzero, x_rows, w_gu, b_gu3, w_down, b_down3, n_groups):
    p_alloc = x_rows.shape[0]

    def f_eff(s, f, gn):
        return jnp.where(gn[s] > 0, f, MOE_NF - 1)

    grid_spec = pltpu.PrefetchScalarGridSpec(
        num_scalar_prefetch=4,
        grid=(n_groups, MOE_NF),
        in_specs=[
            pl.BlockSpec(memory_space=pl.ANY),
            pl.BlockSpec((None, D, MOE_TF), lambda s, f, ge, gs, gn, gz: (ge[s], 0, f_eff(s, f, gn))),
            pl.BlockSpec((None, D, MOE_TF), lambda s, f, ge, gs, gn, gz: (ge[s], 0, MOE_NF + f_eff(s, f, gn))),
            pl.BlockSpec((None, 1, MOE_TF), lambda s, f, ge, gs, gn, gz: (ge[s], 0, f_eff(s, f, gn))),
            pl.BlockSpec((None, 1, MOE_TF), lambda s, f, ge, gs, gn, gz: (ge[s], 0, MOE_NF + f_eff(s, f, gn))),
            pl.BlockSpec((None, MOE_TF, D), lambda s, f, ge, gs, gn, gz: (ge[s], f_eff(s, f, gn), 0)),
            pl.BlockSpec((None, 1, D), lambda s, f, ge, gs, gn, gz: (ge[s], 0, 0)),
        ],
        out_specs=pl.BlockSpec(memory_space=pl.ANY),
        scratch_shapes=[
            pltpu.VMEM((MOE_RB, MOE_TM, D), bf16),
            pltpu.VMEM((MOE_RB, MOE_TM, D), f32),
            pltpu.VMEM((2, MOE_TM, SLAB, LANES), f32),
            pltpu.SemaphoreType.DMA(()),
            pltpu.SemaphoreType.DMA((2,)),
        ],
    )
    return pl.pallas_call(
        _moe_kernel,
        grid_spec=grid_spec,
        out_shape=jax.ShapeDtypeStruct((p_alloc, SLAB, LANES), f32),
        compiler_params=_cparams(("arbitrary", "arbitrary"), MOE_VMEM),
        name="moe_experts",
    )(g_expert, g_start, g_nblk, g_nzero, x_rows, w_gu, w_gu, b_gu3, b_gu3, w_down, b_down3)


CB_TM = 256


def _combine_kernel(dest_ref, dest_next_ref, y_hbm, h1_ref, route_ref, mod_ref, g_ref, o_ref, ybuf, sems):
    i = pl.program_id(0)
    n = pl.num_programs(0)
    slot = i % 2

    def copies(idx_ref, sl):
        return [pltpu.make_async_copy(y_hbm.at[idx_ref[0, 0, t * TOP_K + k]], ybuf.at[sl, k, t], sems.at[sl])
                for t in range(CB_TM) for k in range(TOP_K)]

    def start(idx_ref, sl):
        for j, c in enumerate(copies(idx_ref, sl)):
            c.start(priority=j % 2)

    @pl.when(i == 0)
    def _():
        start(dest_ref, 0)

    for sl in range(2):
        @pl.when((i + 1 < n) & (slot == 1 - sl))
        def _():
            start(dest_next_ref, sl)

    route = route_ref[...]
    gt2 = mod_ref[:, 5 * D:6 * D]
    for sl in range(2):
        @pl.when(slot == sl)
        def _():
            for k in range(TOP_K):
                pltpu.make_async_copy(y_hbm.at[pl.ds(0, CB_TM)], ybuf.at[sl, k], sems.at[sl]).wait()
            moe = jnp.zeros((CB_TM, D), f32)
            for k in range(TOP_K):
                moe = moe + route[:, R_GATE + k:R_GATE + k + 1] * ybuf[sl, k].reshape(CB_TM, D)
            h = h1_ref[...] + gt2 * moe
            ms = jnp.mean(h * h, axis=-1, keepdims=True)
            o_ref[...] = h * lax.rsqrt(ms + RMS_EPS) * g_ref[...]


def _combine(dest3, y_rows, h1, route, mod3, g_final, seq):
    n = h1.shape[0]
    tiles_per_batch = seq // CB_TM
    n_tiles = n // CB_TM
    return pl.pallas_call(
        _combine_kernel,
        grid=(n_tiles,),
        in_specs=[
            pl.BlockSpec((1, 1, CB_TM * TOP_K), lambda i: (i, 0, 0), memory_space=pltpu.SMEM),
            pl.BlockSpec((1, 1, CB_TM * TOP_K), lambda i: (jnp.minimum(i + 1, n_tiles - 1), 0, 0),
                         memory_space=pltpu.SMEM),
            pl.BlockSpec(memory_space=pl.ANY),
            pl.BlockSpec((CB_TM, D), lambda i: (i, 0)),
            pl.BlockSpec((CB_TM, ROUTE_W), lambda i: (i, 0)),
            pl.BlockSpec((None, 1, 6 * D), lambda i: (i // tiles_per_batch, 0, 0)),
            pl.BlockSpec((1, D), lambda i: (0, 0)),
        ],
        out_specs=pl.BlockSpec((CB_TM, D), lambda i: (i, 0)),
        out_shape=jax.ShapeDtypeStruct((n, D), f32),
        scratch_shapes=[pltpu.VMEM((2, TOP_K, CB_TM, SLAB, LANES), f32), pltpu.SemaphoreType.DMA((2,))],
        compiler_params=_cparams(("arbitrary",)),
        name="moe_combine",
    )(dest3, dest3, y_rows, h1, route, mod3, g_final)


def _layer(h, c_act_pad, positions, w_ada, b_ada, g_mix, w_in, b_in, sinks, w_branch_a, w_branch_b, w_out,
           g_ffn, w_router, b_router, w_gu, b_gu, w_down, b_down):
    batch, seq, _ = h.shape
    n = batch * seq
    x2 = h.reshape(n, D)

    mod = _adaln(c_act_pad, w_ada, b_ada.reshape(1, 6 * D))
    mod3 = mod[:batch].reshape(batch, 1, 6 * D)

    w_cat = jnp.concatenate([w_in[:, GATE_OFF:], w_in[:, :QKV_W]], axis=1).astype(bf16)
    b_cat = jnp.concatenate([b_in[GATE_OFF:], b_in[:QKV_W]]).reshape(1, PROJ_W)
    w_f = w_in[:, FGATE_OFF:GATE_OFF].astype(bf16)
    b_f = b_in[FGATE_OFF:GATE_OFF].reshape(1, N_FGATE)

    inv_freq = 1.0 / (ROPE_THETA ** (jnp.arange(0, HD, 2, dtype=f32) / HD))
    ang = positions.astype(f32).reshape(n, 1) * inv_freq[None, :]
    cos_t = jnp.tile(jnp.cos(ang), (1, LANES // (HD // 2)))
    sin_h = jnp.sin(ang)
    sin_t = jnp.tile(jnp.concatenate([-sin_h, sin_h], axis=1), (1, LANES // HD))

    proj, logf = _inproj(x2, mod3, g_mix.reshape(1, D), w_cat, b_cat, w_f, b_f, cos_t, sin_t, seq)
    cum = _cumsum(logf.reshape(batch, seq, N_FGATE))

    ya = _swa(sinks, proj, batch, seq)
    yb = _fox(proj, cum, batch, seq)

    wr_hi = w_router.astype(bf16)
    wr_lo = (w_router - wr_hi.astype(f32)).astype(bf16)
    h1, xn2, route = _merge(ya, yb, proj, x2, mod3, w_branch_a.astype(bf16), w_branch_b.astype(bf16),
                            w_out.astype(bf16), g_ffn.reshape(1, D), wr_hi, wr_lo,
                            b_router.reshape(1, N_EXPERTS), seq)

    a_total = n * TOP_K
    top_idx = route[:, R_IDX:R_IDX + TOP_K].astype(i32)
    rank = route[:, R_RANK:R_RANK + TOP_K].astype(i32)
    counts = jnp.sum((top_idx[..., None] == jnp.arange(N_EXPERTS, dtype=i32)).astype(i32), axis=(0, 1))
    nblk_e = (counts + MOE_TM - 1) // MOE_TM
    blk_end = jnp.cumsum(nblk_e)
    blk_start = blk_end - nblk_e
    dest = (blk_start[top_idx] * MOE_TM + rank).reshape(a_total)
    p_alloc = a_total + N_EXPERTS * MOE_TM
    tok = jnp.arange(a_total, dtype=i32) // TOP_K
    row_tok = jnp.zeros((p_alloc,), i32).at[dest].set(tok, unique_indices=True, mode="promise_in_bounds")

    n_groups = (a_total // MOE_TM + N_EXPERTS + N_EXPERTS * (MOE_RB - 1)) // MOE_RB
    ng_e = (nblk_e + MOE_RB - 1) // MOE_RB
    g_end = jnp.cumsum(ng_e)
    gid = jnp.arange(n_groups, dtype=i32)
    g_exp_raw = jnp.searchsorted(g_end, gid, side="right").astype(i32)
    active = gid < g_end[-1]
    last_e = jnp.max(jnp.where(counts > 0, jnp.arange(N_EXPERTS, dtype=i32), 0))
    g_expert = jnp.where(active, jnp.minimum(g_exp_raw, N_EXPERTS - 1), last_e).astype(i32)
    g_local = gid - (g_end - ng_e)[g_expert]
    g_nblk = jnp.where(active, jnp.minimum(nblk_e[g_expert] - g_local * MOE_RB, MOE_RB), 0).astype(i32)
    tail0 = blk_end[-1] + (gid - g_end[-1]) * MOE_RB
    g_nzero = jnp.where(active, 0, jnp.clip(p_alloc // MOE_TM - tail0, 0, MOE_RB)).astype(i32)
    g_start = jnp.where(active, blk_start[g_expert] + g_local * MOE_RB,
                        jnp.minimum(tail0, p_alloc // MOE_TM)).astype(i32)

    nrows = (blk_end[-1] * MOE_TM).reshape(1).astype(i32)
    x_rows = _gather_rows(nrows, row_tok.reshape(p_alloc // GATHER_CH, 1, GATHER_CH), xn2, p_alloc)
    y_rows = _moe(g_expert, g_start, g_nblk, g_nzero, x_rows, w_gu, b_gu.reshape(N_EXPERTS, 1, 2 * D_EXPERT),
                  w_down, b_down.reshape(N_EXPERTS, 1, D), n_groups)
    return dest, y_rows, h1, route, mod3


def kernel(x, c, positions, w_ada, b_ada, g_mix, w_in, b_in, sinks, w_branch_a, w_branch_b, w_out, g_ffn,
           w_router, b_router, w_gu, b_gu, w_down, b_down, g_final):
    batch, seq, _ = x.shape
    depth = w_ada.shape[0]
    assert depth == 1, "single-layer trunk"
    c_pad = jnp.zeros((SUBLANES, D), f32).at[:batch].set(c)
    l = 0
    dest, y_rows, h1, route, mod3 = _layer(
        x, c_pad, positions, w_ada[l], b_ada[l], g_mix[l], w_in[l], b_in[l], sinks[l], w_branch_a[l],
        w_branch_b[l], w_out[l], g_ffn[l], w_router[l], b_router[l], w_gu[l], b_gu[l], w_down[l], b_down[l])
    n = batch * seq
    out = _combine(dest.reshape(n // CB_TM, 1, CB_TM * TOP_K), y_rows, h1, route, mod3,
                   g_final.reshape(1, D), seq)
    return out.reshape(batch, seq, D)
```
